```python
import math
import jax, jax.numpy as jnp
from jax import lax
import numpy as np

D_MODEL = 2048
BATCH = 4
SEQ = 2048
DEPTH = 4
DEC_BATCH = 128
DEC_SEQ = 4
PAST_LEN = 16384
PAGE_SIZE = 128

N_EVEN = (DEPTH + 1) // 2
N_ODD = DEPTH // 2
N_MOD = 9
D_FF = ((8 * D_MODEL // 3 + 255) // 256) * 256

D_MIX_A = D_MODEL // 2
HEAD_A = 64
H_A = D_MIX_A // HEAD_A
RW_LORA_W = max(32, int(round(1.8 * D_MIX_A ** 0.5 / 32)) * 32)
RW_LORA_A = RW_LORA_W
RW_LORA_G = max(32, int(round(0.6 * D_MIX_A ** 0.8 / 32)) * 32)
RW_COLS = 3 * D_MIX_A + RW_LORA_W + RW_LORA_A + RW_LORA_G
RW_GN_EPS = 64e-5

D_MIX_B = D_MODEL // 2
HEAD_B = 64
H_B = D_MIX_B // HEAD_B
D_STATE = 128
N_GROUPS = 2
CONV_W = 4
MB_CONV_CH = D_MIX_B + 2 * N_GROUPS * D_STATE
MB_COLS = D_MIX_B + MB_CONV_CH + H_B
EV_COLS = RW_COLS + MB_COLS

D_MIX_C = D_MODEL // 2
H_C = 8
HEAD_C = D_MIX_C // H_C
ROPE_BASE = 10000.0

D_MIX_D = D_MODEL // 2
H_D = 4
HEAD_DV = D_MIX_D // H_D
HEAD_DK = HEAD_DV // 2
GLA_LORA = 16
GLA_NORMALIZER = 16.0
OD_COLS = 4 * D_MIX_C + 2 * H_D * HEAD_DK + 2 * D_MIX_D + GLA_LORA

CHUNK = 64
GLA_CHUNK = 32
F32 = jnp.float32

kernel_name = 'hybrid_rwkv7_mamba2_retnet_gla_adaln_step'


def _rms(x, eps=1e-6):
    xf = x.astype(F32)
    return (xf * lax.rsqrt(jnp.mean(xf * xf, axis=-1, keepdims=True) + eps)).astype(x.dtype)


def _swiglu(x, wg, wu, wd):
    return (jax.nn.silu(x @ wg) * (x @ wu)) @ wd


def _to_chunks(t, chunk):
    L = t.shape[1]
    pad = (-L) % chunk
    if pad:
        t = jnp.pad(t, [(0, 0), (0, pad)] + [(0, 0)] * (t.ndim - 2))
    return t.reshape((t.shape[0], -1, chunk) + t.shape[2:])


def _scan_chunks(s0, dec, s_loc):
    def step(s, inp):
        d, sl = inp
        return s * d[..., None] + sl, s
    s_fin, s_prev = lax.scan(step, s0, (jnp.moveaxis(dec, 1, 0), jnp.moveaxis(s_loc, 1, 0)))
    return s_fin, jnp.moveaxis(s_prev, 0, 1)


def _chunked_scalar_decay(q, k, v, logd, s0, chunk):
    L = q.shape[1]
    q, k, v, logd = (_to_chunks(t.astype(F32), chunk) for t in (q, k, v, logd))
    cum = jnp.moveaxis(jnp.cumsum(logd, axis=2), -1, 2)
    causal = jnp.tril(jnp.ones((chunk, chunk), dtype=bool))
    seg = jnp.exp(jnp.where(causal, cum[..., :, None] - cum[..., None, :], -jnp.inf))
    scores = jnp.einsum('bcthn,bcshn->bchts', q, k) * seg
    o = jnp.einsum('bchts,bcshp->bcthp', scores, v)
    last = cum[..., -1:]
    s_loc = jnp.einsum('bcshn,bchs,bcshp->bchnp', k, jnp.exp(last - cum), v)
    s_fin, s_prev = _scan_chunks(s0.astype(F32), jnp.exp(last), s_loc)
    o = o + jnp.einsum('bcthn,bchnp->bcthp', q, s_prev) * jnp.exp(jnp.moveaxis(cum, 2, -1))[..., None]
    o = o.reshape((o.shape[0], -1) + o.shape[3:])[:, :L]
    return o, s_fin.astype(s0.dtype)


def _chunked_vector_decay(q, k, v, logd, s0, chunk):
    L = q.shape[1]
    q, k, v, logd = (_to_chunks(t.astype(F32), chunk) for t in (q, k, v, logd))
    cum = jnp.cumsum(logd, axis=2)
    qe = q * jnp.exp(cum)
    ke = k * jnp.exp(-cum)
    causal = jnp.tril(jnp.ones((chunk, chunk), dtype=bool))
    scores = jnp.where(causal, jnp.einsum('bcthn,bcshn->bchts', qe, ke), 0.0)
    o = jnp.einsum('bchts,bcshp->bcthp', scores, v)
    last = cum[:, :, -1]
    s_loc = jnp.einsum('bcshn,bcshp->bchnp', k * jnp.exp(last[:, :, None] - cum), v)
    s_fin, s_prev = _scan_chunks(s0.astype(F32), jnp.exp(last), s_loc)
    o = o + jnp.einsum('bcthn,bchnp->bcthp', qe, s_prev)
    o = o.reshape((o.shape[0], -1) + o.shape[3:])[:, :L]
    return o, s_fin.astype(s0.dtype)


def _rwkv7_recurrence(r, w, k, v, kk, a, s0):
    xs = tuple(jnp.moveaxis(t.astype(F32), 1, 0) for t in (r, w, k, v, kk, a))

    def step(S, inp):
        r_t, w_t, k_t, v_t, kk_t, a_t = inp
        sa = jnp.einsum('bhij,bhj->bhi', S, -kk_t)
        S = (S * w_t[:, :, None, :] + sa[..., :, None] * (kk_t * a_t)[:, :, None, :]
             + v_t[..., :, None] * k_t[:, :, None, :])
        return S, jnp.einsum('bhij,bhj->bhi', S, r_t)

    s_fin, ys = lax.scan(step, s0.astype(F32), xs)
    return jnp.moveaxis(ys, 0, 1), s_fin.astype(s0.dtype)


def _rotary(x, pos):
    half = x.shape[-1] // 2
    inv = ROPE_BASE ** (-jnp.arange(half, dtype=F32) / half)
    ang = pos[:, None] * inv[None, :]
    cos, sin = jnp.cos(ang)[:, None, :], jnp.sin(ang)[:, None, :]
    xf = x.astype(F32)
    x1, x2 = xf[..., :half], xf[..., half:]
    return jnp.concatenate([x1 * cos - x2 * sin, x1 * sin + x2 * cos], axis=-1)


def _even_mixer(xm, P, shift_buf, wkv, conv_buf, ssm):
    bsz, L, _ = xm.shape
    z = xm @ P['ev_w_in']
    zr, zm = z[..., :RW_COLS], z[..., RW_COLS:]

    prev = jnp.concatenate([shift_buf[:, None, :].astype(zr.dtype), zr[:, :-1]], axis=1)
    zx = zr + P['rw_mu'] * (prev - zr)
    r, k, v, xw, xa, xg = jnp.split(
        zx, np.cumsum([D_MIX_A, D_MIX_A, D_MIX_A, RW_LORA_W, RW_LORA_A]).tolist(), axis=-1)
    w_log = -jax.nn.softplus(-(P['rw_w0'] + jnp.tanh(xw) @ P['rw_w_up']).astype(F32)) - 0.5
    decay = jnp.exp(-jnp.exp(w_log))
    a = jax.nn.sigmoid((P['rw_a0'] + xa @ P['rw_a_up']).astype(F32))
    g = jax.nn.sigmoid(xg) @ P['rw_g_up']
    hs = lambda t: t.reshape(bsz, L, H_A, HEAD_A)
    r, k, v, decay, a = hs(r), hs(k), hs(v), hs(decay), hs(a)
    kk = (k * P['rw_k_k'].reshape(H_A, HEAD_A)).astype(F32)
    kk = kk / jnp.maximum(jnp.linalg.norm(kk, axis=-1, keepdims=True), 1e-12)
    k = k * (1.0 + (a - 1.0) * P['rw_k_a'].reshape(H_A, HEAD_A))
    y, wkv_new = _rwkv7_recurrence(r, decay, k, v, kk, a, wkv)
    mean = jnp.mean(y, axis=-1, keepdims=True)
    var = jnp.mean(jnp.square(y - mean), axis=-1, keepdims=True)
    y = ((y - mean) * lax.rsqrt(var + RW_GN_EPS) * P['rw_gn_w'].reshape(H_A, HEAD_A)
         + P['rw_gn_b'].reshape(H_A, HEAD_A))
    y = y + jnp.sum(r * k * P['rw_r_k'], axis=-1, keepdims=True) * v
    y_a = y.reshape(bsz, L, D_MIX_A) * g
    new_shift = zr[:, -1]

    zg, xbc, dt_raw = jnp.split(zm, [D_MIX_B, D_MIX_B + MB_CONV_CH], axis=-1)
    conv_in = jnp.concatenate([conv_buf.astype(xbc.dtype), xbc], axis=1)
    new_conv = conv_in[:, -(CONV_W - 1):]
    xbc = lax.conv_general_dilated(
        conv_in, P['mb_conv_w'][:, None, :].astype(conv_in.dtype), window_strides=(1,),
        padding='VALID', dimension_numbers=('NWC', 'WIO', 'NWC'), feature_group_count=MB_CONV_CH)
    xbc = jax.nn.silu(xbc + P['mb_conv_b'])
    xs, bs, cs = jnp.split(xbc, [D_MIX_B, D_MIX_B + N_GROUPS * D_STATE], axis=-1)
    dt = jax.nn.softplus((dt_raw + P['mb_dt_bias']).astype(F32))
    A = -jnp.exp(P['mb_a_log'].astype(F32))
    xs = xs.reshape(bsz, L, H_B, HEAD_B)
    rep = H_B // N_GROUPS
    bh = jnp.repeat(bs.reshape(bsz, L, N_GROUPS, D_STATE), rep, axis=2)
    ch = jnp.repeat(cs.reshape(bsz, L, N_GROUPS, D_STATE), rep, axis=2)
    yb, ssm_new = _chunked_scalar_decay(ch, bh, xs * dt[..., None], dt * A, ssm, CHUNK)
    yb = yb + P['mb_d'][:, None] * xs
    y_b = _rms(yb.reshape(bsz, L, D_MIX_B) * jax.nn.silu(zg), 1e-5) * P['mb_norm_w']

    out = jnp.concatenate([y_a, y_b], axis=-1).astype(xm.dtype) @ P['ev_w_out']
    return out, new_shift, wkv_new, new_conv, ssm_new


def _odd_mixer(xm, pos0, P, ret_state, gla_state):
    bsz, L, _ = xm.shape
    z = xm @ P['od_w_in']
    sizes = [D_MIX_C] * 4 + [H_D * HEAD_DK] * 2 + [D_MIX_D, GLA_LORA]
    rq, rk, rv, rg, gq, gk, gv, gd, gr = jnp.split(z, np.cumsum(sizes).tolist(), axis=-1)

    pos = pos0 + jnp.arange(L, dtype=F32)
    hc = lambda t: t.reshape(bsz, L, H_C, HEAD_C)
    q = _rotary(hc(rq), pos) * HEAD_C ** -0.5
    k = _rotary(hc(rk), pos)
    log_gamma = jnp.log1p(-jnp.exp2(-5.0 - jnp.arange(H_C, dtype=F32)))
    o, ret_new = _chunked_scalar_decay(q, k, hc(rv), jnp.broadcast_to(log_gamma, (bsz, L, H_C)),
                                       ret_state, CHUNK)
    y_c = (_rms(o) * P['ret_norm_w'].reshape(H_C, HEAD_C)).reshape(bsz, L, D_MIX_C) * jax.nn.silu(rg)

    log_alpha = jax.nn.log_sigmoid((gd @ P['gla_gate_up'] + P['gla_gate_b']).astype(F32)) / GLA_NORMALIZER
    hk = lambda t: t.reshape(bsz, L, H_D, HEAD_DK)
    o2, gla_new = _chunked_vector_decay(hk(gq) * HEAD_DK ** -0.5, hk(gk),
                                        gv.reshape(bsz, L, H_D, HEAD_DV), hk(log_alpha),
                                        gla_state, GLA_CHUNK)
    y_d = (_rms(o2) * P['gla_norm_w'].reshape(H_D, HEAD_DV)).reshape(bsz, L, D_MIX_D) * jax.nn.silu(gr)

    out = jnp.concatenate([y_c, y_d], axis=-1).astype(xm.dtype) @ P['od_w_out']
    return out, ret_new, gla_new


def _zero_states(bsz, dtype):
    return (jnp.zeros((N_EVEN, bsz, RW_COLS), dtype),
            jnp.zeros((N_EVEN, bsz, H_A, HEAD_A, HEAD_A), dtype),
            jnp.zeros((N_EVEN, bsz, CONV_W - 1, MB_CONV_CH), dtype),
            jnp.zeros((N_EVEN, bsz, H_B, D_STATE, HEAD_B), dtype),
            jnp.zeros((N_ODD, bsz, H_C, HEAD_C, HEAD_C), dtype),
            jnp.zeros((N_ODD, bsz, H_D, HEAD_DK, HEAD_DV), dtype))


def _trunk(h, c, pos0, states, W):
    rw_shift, rw_wkv, mb_conv, mb_ssm, ret_s, gla_s = states
    outs = ([], [], [], [], [], [])
    bsz = h.shape[0]
    silu_c = jax.nn.silu(c)
    for l in range(DEPTH):
        mod = (silu_c @ W['ada_w'][l] + W['ada_b'][l]).reshape(bsz, N_MOD, 1, D_MODEL)
        sh1, sc1, g1, sh2, sc2, g2, sh3, sc3, g3 = (mod[:, j] for j in range(N_MOD))
        xn = _rms(h) * (1.0 + sc1) + sh1
        h = h + 0.5 * (1.0 + g1) * _swiglu(xn, W['ffn_wg'][l, 0], W['ffn_wu'][l, 0], W['ffn_wd'][l, 0])
        xm = _rms(h) * (1.0 + sc2) + sh2
        i = l // 2
        if l % 2 == 0:
            P = {n: arr[i] for n, arr in W.items() if n.split('_')[0] in ('ev', 'rw', 'mb')}
            mix, n0, n1, n2, n3 = _even_mixer(xm, P, rw_shift[i], rw_wkv[i], mb_conv[i], mb_ssm[i])
            outs[0].append(n0); outs[1].append(n1); outs[2].append(n2); outs[3].append(n3)
        else:
            P = {n: arr[i] for n, arr in W.items() if n.split('_')[0] in ('od', 'ret', 'gla')}
            mix, n4, n5 = _odd_mixer(xm, pos0, P, ret_s[i], gla_s[i])
            outs[4].append(n4); outs[5].append(n5)
        h = h + (1.0 + g2) * mix
        xn = _rms(h) * (1.0 + sc3) + sh3
        h = h + 0.5 * (1.0 + g3) * _swiglu(xn, W['ffn_wg'][l, 1], W['ffn_wu'][l, 1], W['ffn_wd'][l, 1])
    y = _rms(h) * W['final_g']
    return y, tuple(jnp.stack(lst) for lst in outs)


def setup_inputs(seed: int = 0) -> dict:
    key = jax.random.key(seed)
    ks = iter(jax.random.split(key, 48))

    def nrm(shape, scale=1.0, shift=0.0):
        return shift + scale * jax.random.normal(next(ks), shape, F32)

    def uni(shape, lo, hi):
        return jax.random.uniform(next(ks), shape, F32, lo, hi)

    x_prompt = nrm((BATCH, SEQ, D_MODEL))
    x_sample = nrm((DEC_BATCH, DEC_SEQ, D_MODEL))
    state_rwkv_shift = nrm((N_EVEN, DEC_BATCH, RW_COLS))
    state_rwkv_wkv = nrm((N_EVEN, DEC_BATCH, H_A, HEAD_A, HEAD_A), 0.3)
    state_mamba_conv = nrm((N_EVEN, DEC_BATCH, CONV_W - 1, MB_CONV_CH))
    state_mamba_ssm = nrm((N_EVEN, DEC_BATCH, H_B, D_STATE, HEAD_B), 0.3)
    state_ret = nrm((N_ODD, DEC_BATCH, H_C, HEAD_C, HEAD_C))
    state_gla = nrm((N_ODD, DEC_BATCH, H_D, HEAD_DK, HEAD_DV))
    c_prompt = nrm((BATCH, D_MODEL))
    c_sample = nrm((DEC_BATCH, D_MODEL))
    dt = jnp.exp(uni((N_EVEN, H_B), math.log(1e-3), math.log(1e-1)))
    mb_dt_bias = dt + jnp.log(-jnp.expm1(-dt))
    return {
        'x_prompt': x_prompt,
        'x_sample': x_sample,
        'state_rwkv_shift': state_rwkv_shift,
        'state_rwkv_wkv': state_rwkv_wkv,
        'state_mamba_conv': state_mamba_conv,
        'state_mamba_ssm': state_mamba_ssm,
        'state_ret': state_ret,
        'state_gla': state_gla,
        'c_prompt': c_prompt,
        'c_sample': c_sample,
        'ada_w': nrm((DEPTH, D_MODEL, N_MOD * D_MODEL), 0.2 * D_MODEL ** -0.5),
        'ada_b': nrm((DEPTH, N_MOD * D_MODEL), 0.02),
        'ffn_wg': nrm((DEPTH, 2, D_MODEL, D_FF), D_MODEL ** -0.5),
        'ffn_wu': nrm((DEPTH, 2, D_MODEL, D_FF), D_MODEL ** -0.5),
        'ffn_wd': nrm((DEPTH, 2, D_FF, D_MODEL), D_FF ** -0.5),
        'ev_w_in': nrm((N_EVEN, D_MODEL, EV_COLS), D_MODEL ** -0.5),
        'ev_w_out': nrm((N_EVEN, D_MIX_A + D_MIX_B, D_MODEL), (D_MIX_A + D_MIX_B) ** -0.5),
        'rw_mu': uni((N_EVEN, RW_COLS), 0.0, 1.0),
        'rw_w0': nrm((N_EVEN, D_MIX_A), 1.0, -0.5),
        'rw_w_up': nrm((N_EVEN, RW_LORA_W, D_MIX_A), 0.5 * RW_LORA_W ** -0.5),
        'rw_a0': nrm((N_EVEN, D_MIX_A), 0.1),
        'rw_a_up': nrm((N_EVEN, RW_LORA_A, D_MIX_A), 0.5 * RW_LORA_A ** -0.5),
        'rw_g_up': nrm((N_EVEN, RW_LORA_G, D_MIX_A), RW_LORA_G ** -0.5),
        'rw_k_k': nrm((N_EVEN, D_MIX_A), 0.05, 0.85),
        'rw_k_a': nrm((N_EVEN, D_MIX_A), 0.05, 1.0),
        'rw_r_k': nrm((N_EVEN, H_A, HEAD_A), 0.1),
        'rw_gn_w': nrm((N_EVEN, D_MIX_A), 0.05, 1.0),
        'rw_gn_b': nrm((N_EVEN, D_MIX_A), 0.02),
        'mb_conv_w': nrm((N_EVEN, CONV_W, MB_CONV_CH), CONV_W ** -0.5),
        'mb_conv_b': nrm((N_EVEN, MB_CONV_CH), 0.02),
        'mb_dt_bias': mb_dt_bias,
        'mb_a_log': jnp.log(uni((N_EVEN, H_B), 1.0, 16.0)),
        'mb_d': nrm((N_EVEN, H_B), 0.05, 1.0),
        'mb_norm_w': nrm((N_EVEN, D_MIX_B), 0.05, 1.0),
        'od_w_in': nrm((N_ODD, D_MODEL, OD_COLS), D_MODEL ** -0.5),
        'od_w_out': nrm((N_ODD, D_MIX_C + D_MIX_D, D_MODEL), (D_MIX_C + D_MIX_D) ** -0.5),
        'ret_norm_w': nrm((N_ODD, D_MIX_C), 0.05, 1.0),
        'gla_gate_up': nrm((N_ODD, GLA_LORA, H_D * HEAD_DK), GLA_LORA ** -0.5),
        'gla_gate_b': nrm((N_ODD, H_D * HEAD_DK), 0.02),
        'gla_norm_w': nrm((N_ODD, D_MIX_D), 0.05, 1.0),
        'final_g': nrm((D_MODEL,), 0.05, 1.0),
    }


def reference(x_prompt, x_sample, state_rwkv_shift, state_rwkv_wkv, state_mamba_conv,
              state_mamba_ssm, state_ret, state_gla, c_prompt, c_sample,
              ada_w, ada_b, ffn_wg, ffn_wu, ffn_wd, ev_w_in, ev_w_out,
              rw_mu, rw_w0, rw_w_up, rw_a0, rw_a_up, rw_g_up, rw_k_k, rw_k_a, rw_r_k,
              rw_gn_w, rw_gn_b, mb_conv_w, mb_conv_b, mb_dt_bias, mb_a_log, mb_d, mb_norm_w,
              od_w_in, od_w_out, ret_norm_w, gla_gate_up, gla_gate_b, gla_norm_w, final_g):
    W = dict(ada_w=ada_w, ada_b=ada_b, ffn_wg=ffn_wg, ffn_wu=ffn_wu, ffn_wd=ffn_wd,
             ev_w_in=ev_w_in, ev_w_out=ev_w_out, rw_mu=rw_mu, rw_w0=rw_w0, rw_w_up=rw_w_up,
             rw_a0=rw_a0, rw_a_up=rw_a_up, rw_g_up=rw_g_up, rw_k_k=rw_k_k, rw_k_a=rw_k_a,
             rw_r_k=rw_r_k, rw_gn_w=rw_gn_w, rw_gn_b=rw_gn_b, mb_conv_w=mb_conv_w,
             mb_conv_b=mb_conv_b, mb_dt_bias=mb_dt_bias, mb_a_log=mb_a_log, mb_d=mb_d,
             mb_norm_w=mb_norm_w, od_w_in=od_w_in, od_w_out=od_w_out, ret_norm_w=ret_norm_w,
             gla_gate_up=gla_gate_up, gla_gate_b=gla_gate_b, gla_norm_w=gla_norm_w,
             final_g=final_g)
    y_prompt, (p_shift, p_wkv, p_conv, p_ssm, p_ret, p_gla) = _trunk(
        x_prompt, c_prompt, 0, _zero_states(x_prompt.shape[0], x_prompt.dtype), W)
    y_sample, (s_shift, s_wkv, s_conv, s_ssm, s_ret, s_gla) = _trunk(
        x_sample, c_sample, PAST_LEN,
        (state_rwkv_shift, state_rwkv_wkv, state_mamba_conv, state_mamba_ssm, state_ret, state_gla), W)
    return (y_prompt, y_sample, p_shift, p_wkv, p_conv, p_ssm, p_ret, p_gla,
            s_shift, s_wkv, s_conv, s_ssm, s_ret, s_gla)
```

```python
import functools
import math

import numpy as np
import jax
import jax.numpy as jnp
from jax import lax
from jax.experimental import pallas as pl
from jax.experimental.pallas import tpu as pltpu

F32 = jnp.float32
BF16 = jnp.bfloat16

D_MODEL = 2048
DEPTH = 4
PAST_LEN = 16384
N_MOD = 9
D_FF = 5632

D_MIX = 1024
H_A, HEAD_A = 16, 64
RW_LORA_W, RW_LORA_A, RW_LORA_G = 64, 64, 160
RW_COLS = 3 * D_MIX + RW_LORA_W + RW_LORA_A + RW_LORA_G
RW_GN_EPS = 64e-5
H_B, HEAD_B, D_STATE, N_GROUPS, CONV_W = 16, 64, 128, 2, 4
MB_CONV_CH = D_MIX + 2 * N_GROUPS * D_STATE
MB_COLS = D_MIX + MB_CONV_CH + H_B
H_C, HEAD_C = 8, 128
ROPE_BASE = 10000.0
H_D, HEAD_DK, HEAD_DV, GLA_LORA = 4, 128, 256, 16
GLA_NORMALIZER = 16.0
OD_COLS = 4 * D_MIX + 2 * H_D * HEAD_DK + 2 * D_MIX + GLA_LORA

LANES = 128
SUBLANES = 8
VMEM_LIMIT = 56 * 1024 * 1024

RW_PAD = 3456
ZM = RW_PAD
MB_PAD = 2688
EV_PAD = RW_PAD + MB_PAD
OD_PAD = 7296
OD_GD = 7168

PROMPT_CHUNK = 64
GLA_SUB = 32
SAMPLE_CHUNK = 8


def _mm(a, b):
    return jnp.dot(a.astype(BF16), b.astype(BF16), preferred_element_type=F32)


def _mm_nt(a, b):
    return lax.dot_general(a.astype(BF16), b.astype(BF16), (((1,), (1,)), ((), ())),
                           preferred_element_type=F32)


def _mm_tn(a, b):
    return lax.dot_general(a.astype(BF16), b.astype(BF16), (((0,), (0,)), ((), ())),
                           preferred_element_type=F32)


def _split3(x):
    hi = x.astype(BF16)
    r = x - hi.astype(F32)
    mid = r.astype(BF16)
    lo = (r - mid.astype(F32)).astype(BF16)
    return hi, mid, lo


def _mm_sel(sel, x):
    s = sel.astype(BF16)
    hi, mid, lo = _split3(x)
    return (jnp.dot(s, hi, preferred_element_type=F32) + jnp.dot(s, mid, preferred_element_type=F32)
            + jnp.dot(s, lo, preferred_element_type=F32))


def _mm_tn_sel(x, sel):
    s = sel.astype(BF16)
    dn = (((0,), (0,)), ((), ()))
    hi, mid, lo = _split3(x)
    return (lax.dot_general(hi, s, dn, preferred_element_type=F32)
            + lax.dot_general(mid, s, dn, preferred_element_type=F32)
            + lax.dot_general(lo, s, dn, preferred_element_type=F32))


def _mm_xsel(x, sel):
    s = sel.astype(BF16)
    hi, mid, lo = _split3(x)
    return (jnp.dot(hi, s, preferred_element_type=F32) + jnp.dot(mid, s, preferred_element_type=F32)
            + jnp.dot(lo, s, preferred_element_type=F32))


def _sigmoid(x):
    return 1.0 / (1.0 + jnp.exp(-x))


def _silu(x):
    return x * _sigmoid(x)


def _softplus(x):
    return jnp.maximum(x, 0.0) + jnp.log(1.0 + jnp.exp(-jnp.abs(x)))


def _rms(x, eps):
    return x * lax.rsqrt(jnp.mean(x * x, axis=-1, keepdims=True) + eps)


def _tri(n, strict):
    r = lax.broadcasted_iota(jnp.int32, (n, n), 0)
    c = lax.broadcasted_iota(jnp.int32, (n, n), 1)
    return (r > c) if strict else (r >= c)


def _mod_specs(seq_mod, tm, rows_per_seq, idxs):
    specs = []
    for idx in idxs:
        if seq_mod:
            specs.append(pl.BlockSpec(
                (1, 1, D_MODEL),
                lambda i, j, idx=idx: ((i * tm // rows_per_seq) * N_MOD + idx, 0, 0)))
        else:
            specs.append(pl.BlockSpec((tm, D_MODEL), lambda i, j, idx=idx: (i, idx)))
    return specs


def _mod_val(ref, seq_mod):
    return ref[0] if seq_mod else ref[...]


def _row_tile(m, seq_mod):
    return min(512 if seq_mod else 256, m)


def _ada_kernel(c_ref, w_ref, b_ref, o_ref):
    x = _silu(c_ref[...]).astype(BF16)
    o_ref[...] = jnp.dot(x, w_ref[...].astype(BF16), preferred_element_type=F32) + b_ref[...]


def _ada(c, ada_w, ada_b):
    rows = c.shape[0]
    n = ada_w.shape[-1]
    tn = 1024
    return pl.pallas_call(
        _ada_kernel,
        grid=(DEPTH, n // tn),
        in_specs=[pl.BlockSpec((rows, D_MODEL), lambda l, j: (0, 0)),
                  pl.BlockSpec((None, D_MODEL, tn), lambda l, j: (l, 0, j)),
                  pl.BlockSpec((None, 1, tn), lambda l, j: (l, 0, j))],
        out_specs=pl.BlockSpec((None, rows, tn), lambda l, j: (l, 0, j)),
        out_shape=jax.ShapeDtypeStruct((DEPTH, rows, n), F32),
        compiler_params=pltpu.CompilerParams(dimension_semantics=("arbitrary", "arbitrary"),
                                             vmem_limit_bytes=VMEM_LIMIT),
        name="ada",
    )(c, ada_w, ada_b.reshape(DEPTH, 1, n))


def _ffn_kernel(*refs, seq_mod, nj, final):
    if final:
        h_ref, sh_ref, sc_ref, g_ref, wg_ref, wu_ref, wd_ref, fin_ref, o_ref, xn_ref, acc_ref = refs
    else:
        h_ref, sh_ref, sc_ref, g_ref, wg_ref, wu_ref, wd_ref, o_ref, xn_ref, acc_ref = refs
    j = pl.program_id(1)

    @pl.when(j == 0)
    def _():
        xn = _rms(h_ref[...], 1e-6) * (1.0 + _mod_val(sc_ref, seq_mod)) + _mod_val(sh_ref, seq_mod)
        xn_ref[...] = xn.astype(BF16)
        acc_ref[...] = jnp.zeros_like(acc_ref)

    xn = xn_ref[...]
    gate = jnp.dot(xn, wg_ref[...], preferred_element_type=F32)
    up = jnp.dot(xn, wu_ref[...], preferred_element_type=F32)
    act = (_silu(gate) * up).astype(BF16)
    acc_ref[...] += jnp.dot(act, wd_ref[...], preferred_element_type=F32)

    @pl.when(j == nj - 1)
    def _():
        out = h_ref[...] + 0.5 * (1.0 + _mod_val(g_ref, seq_mod)) * acc_ref[...]
        if final:
            out = _rms(out, 1e-6) * fin_ref[...]
        o_ref[...] = out


def _ffn(h, mod, mod_base, wg, wu, wd, *, seq_mod, rows_per_seq, final_g=None):
    m = h.shape[0]
    tm = _row_tile(m, seq_mod)
    tf = 512
    nj = D_FF // tf
    final = final_g is not None
    in_specs = [pl.BlockSpec((tm, D_MODEL), lambda i, j: (i, 0))]
    in_specs += _mod_specs(seq_mod, tm, rows_per_seq, (mod_base, mod_base + 1, mod_base + 2))
    in_specs += [pl.BlockSpec((D_MODEL, tf), lambda i, j: (0, j)),
                 pl.BlockSpec((D_MODEL, tf), lambda i, j: (0, j)),
                 pl.BlockSpec((tf, D_MODEL), lambda i, j: (j, 0))]
    args = [h, mod, mod, mod, wg, wu, wd]
    if final:
        in_specs.append(pl.BlockSpec((1, D_MODEL), lambda i, j: (0, 0)))
        args.append(final_g.reshape(1, D_MODEL))
    return pl.pallas_call(
        functools.partial(_ffn_kernel, seq_mod=seq_mod, nj=nj, final=final),
        grid=(m // tm, nj),
        in_specs=in_specs,
        out_specs=pl.BlockSpec((tm, D_MODEL), lambda i, j: (i, 0)),
        out_shape=jax.ShapeDtypeStruct((m, D_MODEL), F32),
        scratch_shapes=[pltpu.VMEM((tm, D_MODEL), BF16), pltpu.VMEM((tm, D_MODEL), F32)],
        compiler_params=pltpu.CompilerParams(dimension_semantics=("arbitrary", "arbitrary"),
                                             vmem_limit_bytes=VMEM_LIMIT),
        name="ffn",
    )(*args)


def _inproj_kernel(h_ref, sh_ref, sc_ref, w_ref, o_ref, xn_ref, *, seq_mod):
    @pl.when(pl.program_id(1) == 0)
    def _():
        xn = _rms(h_ref[...], 1e-6) * (1.0 + _mod_val(sc_ref, seq_mod)) + _mod_val(sh_ref, seq_mod)
        xn_ref[...] = xn.astype(BF16)

    o_ref[...] = jnp.dot(xn_ref[...], w_ref[...], preferred_element_type=F32)


def _inproj(h, mod, w, *, seq_mod, rows_per_seq):
    m = h.shape[0]
    n = w.shape[1]
    tm = _row_tile(m, seq_mod)
    tn = n // 3
    in_specs = [pl.BlockSpec((tm, D_MODEL), lambda i, j: (i, 0))]
    in_specs += _mod_specs(seq_mod, tm, rows_per_seq, (3, 4))
    in_specs += [pl.BlockSpec((D_MODEL, tn), lambda i, j: (0, j))]
    return pl.pallas_call(
        functools.partial(_inproj_kernel, seq_mod=seq_mod),
        grid=(m // tm, n // tn),
        in_specs=in_specs,
        out_specs=pl.BlockSpec((tm, tn), lambda i, j: (i, j)),
        out_shape=jax.ShapeDtypeStruct((m, n), F32),
        scratch_shapes=[pltpu.VMEM((tm, D_MODEL), BF16)],
        compiler_params=pltpu.CompilerParams(dimension_semantics=("arbitrary", "arbitrary"),
                                             vmem_limit_bytes=VMEM_LIMIT),
        name="inproj",
    )(h, mod, mod, w)


def _outproj_kernel(y_ref, w_ref, h_ref, g_ref, o_ref, *, seq_mod):
    mix = jnp.dot(y_ref[...], w_ref[...], preferred_element_type=F32)
    o_ref[...] = h_ref[...] + (1.0 + _mod_val(g_ref, seq_mod)) * mix


def _outproj(y, w, h, mod, *, seq_mod, rows_per_seq):
    m = h.shape[0]
    tm = _row_tile(m, seq_mod)
    in_specs = [pl.BlockSpec((tm, D_MODEL), lambda i, j: (i, 0)),
                pl.BlockSpec((D_MODEL, D_MODEL), lambda i, j: (0, 0)),
                pl.BlockSpec((tm, D_MODEL), lambda i, j: (i, 0))]
    in_specs += _mod_specs(seq_mod, tm, rows_per_seq, (5,))
    return pl.pallas_call(
        functools.partial(_outproj_kernel, seq_mod=seq_mod),
        grid=(m // tm, 1),
        in_specs=in_specs,
        out_specs=pl.BlockSpec((tm, D_MODEL), lambda i, j: (i, 0)),
        out_shape=jax.ShapeDtypeStruct((m, D_MODEL), F32),
        compiler_params=pltpu.CompilerParams(dimension_semantics=("arbitrary", "arbitrary"),
                                             vmem_limit_bytes=VMEM_LIMIT),
        name="outproj",
    )(y, w, h, mod)


EV_W0, EV_A0, EV_KK, EV_KA, EV_RK, EV_GNW, EV_GNB, EV_D, EV_NW = range(9)
EV_VEC_ROWS = 16


def _even_kernel(*refs, t, nchunks, lv, has_init):
    if has_init:
        (z_ref, shift0_ref, wkv0_ref, conv0_ref, ssm0_ref, *rest) = refs
    else:
        z_ref, *rest = refs
    (mu_ref, vec_ref, wa_ref, gup_ref, cw_ref, cb_ref, dtb_ref, alog_ref, exp_ref,
     y_ref, shift_o, wkv_o, conv_o, ssm_o,
     carry_ref, wkv_ref, ext_ref, ssm_ref,
     r_s, kp_s, v_s, kk_s, a_s, c_s, ld_s, g_s, o_s, y_s) = rest
    ci = pl.program_id(1)
    masked = lv < t
    nlog = int(math.log2(t))

    @pl.when(ci == 0)
    def _():
        if has_init:
            carry_ref[:, 0:RW_COLS] = shift0_ref[0]
            carry_ref[:, RW_COLS:RW_PAD] = jnp.zeros((1, RW_PAD - RW_COLS), F32)
            wkv_ref[...] = wkv0_ref[0]
            ext_ref[0:SUBLANES - CONV_W + 1, :] = jnp.zeros((SUBLANES - CONV_W + 1, MB_CONV_CH), F32)
            ext_ref[SUBLANES - CONV_W + 1:SUBLANES, :] = conv0_ref[0]
            for h in range(H_B):
                ssm_ref[:, h * HEAD_B:(h + 1) * HEAD_B] = ssm0_ref[0, h]
        else:
            carry_ref[...] = jnp.zeros_like(carry_ref)
            wkv_ref[...] = jnp.zeros_like(wkv_ref)
            ext_ref[0:SUBLANES, :] = jnp.zeros((SUBLANES, MB_CONV_CH), F32)
            ssm_ref[...] = jnp.zeros_like(ssm_ref)

    row = lax.broadcasted_iota(jnp.int32, (t, 1), 0)
    valid = row < lv
    strict = _tri(t, True)
    incl = _tri(t, False)

    def vec(i, lo=0, hi=D_MIX):
        return vec_ref[i:i + 1, lo:hi]

    def shifted(a, b):
        cur = z_ref[0, :, a:b]
        prev = jnp.where(row == 0, carry_ref[:, a:b], pltpu.roll(cur, 1, axis=0))
        return cur + mu_ref[:, a:b] * (prev - cur)

    r_s[...] = shifted(0, D_MIX)
    k = shifted(D_MIX, 2 * D_MIX)
    v_s[...] = shifted(2 * D_MIX, 3 * D_MIX)
    lo = shifted(3 * D_MIX, 3 * D_MIX + LANES)
    lane = lax.broadcasted_iota(jnp.int32, (t, LANES), 1)
    wa = _mm(jnp.where(lane < RW_LORA_W, jnp.tanh(lo), lo), wa_ref[...])
    w_log = -_softplus(-(vec(EV_W0) + wa[:, 0:D_MIX])) - 0.5
    ld = -jnp.exp(w_log)
    a = _sigmoid(vec(EV_A0) + wa[:, D_MIX:2 * D_MIX])
    g_s[...] = _mm(_sigmoid(shifted(3 * D_MIX + LANES, RW_PAD)), gup_ref[...])
    kp = k * (1.0 + (a - 1.0) * vec(EV_KA))
    if masked:
        ld = jnp.where(valid, ld, 0.0)
        kp = jnp.where(valid, kp, 0.0)
        a = jnp.where(valid, a, 0.0)
    kk_s[...] = k * vec(EV_KK)
    kp_s[...] = kp
    a_s[...] = a
    ld_s[...] = ld
    c_s[...] = _mm_sel(incl, ld)
    last_row = lv - 1
    carry_ref[...] = z_ref[0, last_row:last_row + 1, 0:RW_PAD]

    for h in range(H_A):
        sl = slice(h * HEAD_A, (h + 1) * HEAD_A)
        r = r_s[:, sl]
        kp = kp_s[:, sl]
        v = v_s[:, sl]
        kkr = kk_s[:, sl]
        c = c_s[:, sl]
        nrm = jnp.sqrt(jnp.sum(kkr * kkr, axis=-1, keepdims=True))
        kk = kkr / jnp.maximum(nrm, 1e-12)
        b = kk * a_s[:, sl]
        ec = jnp.exp(c)
        eci = jnp.exp(-c)
        lhs = jnp.concatenate([kk * jnp.exp(c - ld_s[:, sl]), r * ec], axis=0)
        s0 = wkv_ref[h]
        a1 = _mm_nt(lhs, kp * eci)
        a2 = _mm_nt(lhs, b * eci)
        pm = _mm_nt(lhs, s0)
        a_kk = jnp.where(strict, a1[0:t], 0.0)
        a_rk = jnp.where(incl, a1[t:2 * t], 0.0)
        n_kb = jnp.where(strict, -a2[0:t], 0.0)
        a_rb = jnp.where(incl, a2[t:2 * t], 0.0)
        u = pm[0:t] + _mm(a_kk, v)
        u = u + _mm(n_kb, u)
        p = n_kb
        for _ in range(nlog - 1):
            p = _mm(p, p)
            u = u + _mm(p, u)
        y = pm[t:2 * t] + _mm(a_rk, v) - _mm(a_rb, u)
        dec_end = jnp.exp(c[t - 1:t] - c)
        vu = jnp.concatenate([v, u], axis=0)
        kb = jnp.concatenate([kp * dec_end, -(b * dec_end)], axis=0)
        wkv_ref[h] = s0 * ec[t - 1:t] + _mm_tn(vu, kb)
        mean = jnp.mean(y, axis=-1, keepdims=True)
        yc = y - mean
        var = jnp.mean(yc * yc, axis=-1, keepdims=True)
        yn = yc * lax.rsqrt(var + RW_GN_EPS) * vec(EV_GNW, sl.start, sl.stop) + vec(EV_GNB, sl.start, sl.stop)
        bonus = jnp.sum(r * kp * vec(EV_RK, sl.start, sl.stop), axis=-1, keepdims=True) * v
        y_s[:, sl] = (yn + bonus) * g_s[:, sl]

    zg = z_ref[0, :, ZM:ZM + D_MIX]
    ext_ref[SUBLANES:SUBLANES + t, :] = z_ref[0, :, ZM + D_MIX:ZM + D_MIX + MB_CONV_CH]
    conv = cb_ref[...]
    for w in range(CONV_W):
        off = SUBLANES - (CONV_W - 1) + w
        conv = conv + cw_ref[w:w + 1, :] * ext_ref[off:off + t, :]
    new_conv = ext_ref[SUBLANES + lv - (CONV_W - 1):SUBLANES + lv, :]
    ext_ref[SUBLANES - (CONV_W - 1):SUBLANES, :] = new_conv
    xbc = _silu(conv)
    xs = xbc[:, 0:D_MIX]
    dt = _softplus(z_ref[0, :, ZM + D_MIX + MB_CONV_CH:ZM + MB_PAD] + dtb_ref[...])
    logd = dt * (-jnp.exp(alog_ref[...]))
    if masked:
        logd = jnp.where(valid, logd, 0.0)
    c = _mm_sel(incl, logd)
    upper = lax.broadcasted_iota(jnp.int32, (t, t), 0) <= lax.broadcasted_iota(jnp.int32, (t, t), 1)
    c_t = _mm_tn_sel(logd, upper)
    wide = _mm_xsel(jnp.concatenate([dt, c, c[t - 1:t] - c], axis=0), exp_ref[...])
    xdt = xs * wide[0:t]
    if masked:
        xdt = jnp.where(valid, xdt, 0.0)
    ec_w = jnp.exp(wide[t:2 * t])
    dec_end_w = jnp.exp(wide[2 * t:3 * t])
    ecl_w = ec_w[t - 1:t]
    gw = (H_B // N_GROUPS) * HEAD_B
    for g in range(N_GROUPS):
        bg = xbc[:, D_MIX + g * D_STATE:D_MIX + (g + 1) * D_STATE]
        cg = xbc[:, D_MIX + N_GROUPS * D_STATE + g * D_STATE:D_MIX + N_GROUPS * D_STATE + (g + 1) * D_STATE]
        scores = _mm_nt(cg, bg)
        gs = slice(g * gw, (g + 1) * gw)
        s_prev = ssm_ref[:, gs]
        inter = _mm(cg, s_prev)
        for hh in range(H_B // N_GROUPS):
            h = g * (H_B // N_GROUPS) + hh
            hs = slice(h * HEAD_B, (h + 1) * HEAD_B)
            diff = jnp.where(incl, c[:, h:h + 1] - c_t[h:h + 1, :], 0.0)
            seg = jnp.where(incl, jnp.exp(diff), 0.0)
            o_s[:, hs] = _mm(scores * seg, xdt[:, hs])
        ssm_ref[:, gs] = s_prev * ecl_w[:, gs] + _mm_tn(bg, xdt[:, gs] * dec_end_w[:, gs])
        o_s[:, gs] = o_s[:, gs] + inter * ec_w[:, gs]
    yb = (o_s[...] + vec(EV_D) * xs) * _silu(zg)
    y_s[:, D_MIX:2 * D_MIX] = _rms(yb, 1e-5) * vec(EV_NW)
    y_ref[0] = y_s[...].astype(BF16)

    @pl.when(ci == nchunks - 1)
    def _():
        shift_o[0] = carry_ref[...]
        wkv_o[0] = wkv_ref[...]
        conv_o[0] = new_conv
        for h in range(H_B):
            ssm_o[0, h] = ssm_ref[:, h * HEAD_B:(h + 1) * HEAD_B]


def _even_mixer(z, states, p, *, t, lv):
    nseq, length, _ = z.shape
    nchunks = length // t
    has_init = states is not None
    assert lv == t or nchunks == 1
    full = lambda *shape: pl.BlockSpec(shape, lambda b, c: (0,) * len(shape))
    in_specs = [pl.BlockSpec((1, t, EV_PAD), lambda b, c: (b, c, 0))]
    args = [z]
    if has_init:
        shift0, wkv0, conv0, ssm0 = states
        in_specs += [pl.BlockSpec((1, 1, RW_COLS), lambda b, c: (b, 0, 0)),
                     pl.BlockSpec((1, H_A, HEAD_A, HEAD_A), lambda b, c: (b, 0, 0, 0)),
                     pl.BlockSpec((1, CONV_W - 1, MB_CONV_CH), lambda b, c: (b, 0, 0)),
                     pl.BlockSpec((1, H_B, D_STATE, HEAD_B), lambda b, c: (b, 0, 0, 0))]
        args += [shift0.reshape(nseq, 1, RW_COLS), wkv0, conv0, ssm0]
    in_specs += [full(1, RW_PAD), full(EV_VEC_ROWS, D_MIX), full(LANES, 2 * D_MIX),
                 full(RW_PAD - 3 * D_MIX - LANES, D_MIX), full(CONV_W, MB_CONV_CH),
                 full(1, MB_CONV_CH), full(1, LANES), full(1, LANES), full(LANES, D_MIX)]
    args += [p['mu'], p['vec'], p['wa'], p['gup'], p['conv_w'], p['conv_b'], p['dt_bias'],
             p['a_log'], p['expand']]
    out_shape = (jax.ShapeDtypeStruct((nseq, length, D_MODEL), BF16),
                 jax.ShapeDtypeStruct((nseq, 1, RW_PAD), F32),
                 jax.ShapeDtypeStruct((nseq, H_A, HEAD_A, HEAD_A), F32),
                 jax.ShapeDtypeStruct((nseq, CONV_W - 1, MB_CONV_CH), F32),
                 jax.ShapeDtypeStruct((nseq, H_B, D_STATE, HEAD_B), F32))
    out_specs = (pl.BlockSpec((1, t, D_MODEL), lambda b, c: (b, c, 0)),
                 pl.BlockSpec((1, 1, RW_PAD), lambda b, c: (b, 0, 0)),
                 pl.BlockSpec((1, H_A, HEAD_A, HEAD_A), lambda b, c: (b, 0, 0, 0)),
                 pl.BlockSpec((1, CONV_W - 1, MB_CONV_CH), lambda b, c: (b, 0, 0)),
                 pl.BlockSpec((1, H_B, D_STATE, HEAD_B), lambda b, c: (b, 0, 0, 0)))
    wide = lambda: pltpu.VMEM((t, D_MIX), F32)
    scratch = [pltpu.VMEM((1, RW_PAD), F32), pltpu.VMEM((H_A, HEAD_A, HEAD_A), F32),
               pltpu.VMEM((t + SUBLANES, MB_CONV_CH), F32), pltpu.VMEM((D_STATE, H_B * HEAD_B), F32)]
    scratch += [wide() for _ in range(9)] + [pltpu.VMEM((t, D_MODEL), F32)]
    y, shift, wkv, conv, ssm = pl.pallas_call(
        functools.partial(_even_kernel, t=t, nchunks=nchunks, lv=lv, has_init=has_init),
        grid=(nseq, nchunks),
        in_specs=in_specs, out_specs=out_specs, out_shape=out_shape, scratch_shapes=scratch,
        compiler_params=pltpu.CompilerParams(dimension_semantics=("arbitrary", "arbitrary"),
                                             vmem_limit_bytes=VMEM_LIMIT),
        name="even_mixer",
    )(*args)
    return y, (shift[:, 0, :RW_COLS], wkv, conv, ssm)


def _odd_kernel(*refs, t, tsub, nchunks, lv, has_init):
    if has_init:
        z_ref, ret0_ref, gla0_ref, *rest = refs
    else:
        z_ref, *rest = refs
    (cos_ref, sin_ref, seg_ref, pw_ref, dte_ref, rnw_ref, gup_ref, gb_ref, gnw_ref,
     y_ref, ret_o, gla_o, ret_ref, gla_ref, la_s, y_s) = rest
    ci = pl.program_id(1)
    masked = lv < t

    @pl.when(ci == 0)
    def _():
        if has_init:
            ret_ref[...] = ret0_ref[0]
            for h in range(H_D):
                gla_ref[h] = jnp.transpose(gla0_ref[0, h])
        else:
            ret_ref[...] = jnp.zeros_like(ret_ref)
            gla_ref[...] = jnp.zeros_like(gla_ref)

    row = lax.broadcasted_iota(jnp.int32, (t, 1), 0)
    valid = row < lv

    cosf = cos_ref[...]
    sinf = sin_ref[...]

    def rotary(x):
        return x * cosf + pltpu.roll(x, HEAD_C // 2, axis=1) * sinf

    for h in range(H_C):
        sl = slice(h * HEAD_C, (h + 1) * HEAD_C)
        gamma = 1.0 - 2.0 ** (-5.0 - h)
        q = rotary(z_ref[0, :, sl]) * HEAD_C ** -0.5
        k = rotary(z_ref[0, :, D_MIX + h * HEAD_C:D_MIX + (h + 1) * HEAD_C])
        if masked:
            k = jnp.where(valid, k, 0.0)
        v = z_ref[0, :, 2 * D_MIX + h * HEAD_C:2 * D_MIX + (h + 1) * HEAD_C]
        s0 = ret_ref[h]
        o = _mm(_mm_nt(q, k) * seg_ref[h], v) + _mm(q, s0) * pw_ref[h]
        ret_ref[h] = s0 * gamma ** lv + _mm_tn(k * dte_ref[h], v)
        rg = z_ref[0, :, 3 * D_MIX + h * HEAD_C:3 * D_MIX + (h + 1) * HEAD_C]
        y_s[:, sl] = _rms(o, 1e-6) * rnw_ref[:, sl] * _silu(rg)

    gq0 = 4 * D_MIX
    gk0 = gq0 + H_D * HEAD_DK
    gv0 = gk0 + H_D * HEAD_DK
    gr0 = gv0 + D_MIX
    gd = z_ref[0, :, OD_GD:OD_GD + LANES]
    x = _mm(gd, gup_ref[...]) + gb_ref[...]
    la = -_softplus(-x) * (1.0 / GLA_NORMALIZER)
    if masked:
        la = jnp.where(valid, la, 0.0)
    la_s[...] = la
    incl = _tri(tsub, False)
    for s in range(t // tsub):
        rs = slice(s * tsub, (s + 1) * tsub)
        cum_all = _mm_sel(incl, la_s[rs, :])
        for h in range(H_D):
            cum = cum_all[:, h * HEAD_DK:(h + 1) * HEAD_DK]
            q = z_ref[0, rs, gq0 + h * HEAD_DK:gq0 + (h + 1) * HEAD_DK] * HEAD_DK ** -0.5
            k = z_ref[0, rs, gk0 + h * HEAD_DK:gk0 + (h + 1) * HEAD_DK]
            if masked:
                k = jnp.where(valid[rs], k, 0.0)
            v = z_ref[0, rs, gv0 + h * HEAD_DV:gv0 + (h + 1) * HEAD_DV]
            qe = q * jnp.exp(cum)
            ke = k * jnp.exp(-cum)
            st = gla_ref[h]
            scores = jnp.where(incl, _mm_nt(qe, ke), 0.0)
            o = _mm(scores, v) + _mm_nt(qe, st)
            last = cum[tsub - 1:tsub]
            gla_ref[h] = st * jnp.exp(last) + _mm_tn(v, k * jnp.exp(last - cum))
            gr = z_ref[0, rs, gr0 + h * HEAD_DV:gr0 + (h + 1) * HEAD_DV]
            ys = slice(D_MIX + h * HEAD_DV, D_MIX + (h + 1) * HEAD_DV)
            y_s[rs, ys] = _rms(o, 1e-6) * gnw_ref[:, h * HEAD_DV:(h + 1) * HEAD_DV] * _silu(gr)
    y_ref[0] = y_s[...].astype(BF16)

    @pl.when(ci == nchunks - 1)
    def _():
        ret_o[0] = ret_ref[...]
        for h in range(H_D):
            gla_o[0, h] = jnp.transpose(gla_ref[h])


def _ret_tables(t, lv, pos0, nchunks):
    half = HEAD_C // 2
    inv = ROPE_BASE ** (-jnp.arange(half, dtype=F32) / half)
    pos = pos0 + jnp.arange(t * nchunks, dtype=F32)
    ang = pos[:, None] * inv[None, :]
    cos, sin = jnp.cos(ang), jnp.sin(ang)
    cosf = jnp.concatenate([cos, cos], axis=-1)
    sinf = jnp.concatenate([-sin, sin], axis=-1)
    gam = 1.0 - np.exp2(-5.0 - np.arange(H_C, dtype=np.float64))
    ti = np.arange(t)
    d = ti[:, None] - ti[None, :]
    seg = np.where(d >= 0, gam[:, None, None] ** np.maximum(d, 0)[None], 0.0)
    pw = np.broadcast_to((gam[:, None] ** (ti + 1)[None])[:, :, None], (H_C, t, LANES))
    dte = gam[:, None] ** np.maximum(lv - 1 - ti, 0)[None]
    dte = np.broadcast_to(np.where(ti < lv, dte, 0.0)[:, :, None], (H_C, t, LANES))
    return cosf, sinf, jnp.asarray(seg, F32), jnp.asarray(pw, F32), jnp.asarray(dte, F32)


def _odd_mixer(z, states, p, *, t, lv, pos0):
    nseq, length, _ = z.shape
    nchunks = length // t
    has_init = states is not None
    assert lv == t or nchunks == 1
    tsub = min(GLA_SUB, t)
    cosf, sinf, seg, pw, dte = _ret_tables(t, lv, pos0, nchunks)
    full = lambda *shape: pl.BlockSpec(shape, lambda b, c: (0,) * len(shape))
    in_specs = [pl.BlockSpec((1, t, OD_PAD), lambda b, c: (b, c, 0))]
    args = [z]
    if has_init:
        in_specs += [pl.BlockSpec((1, H_C, HEAD_C, HEAD_C), lambda b, c: (b, 0, 0, 0)),
                     pl.BlockSpec((1, H_D, HEAD_DK, HEAD_DV), lambda b, c: (b, 0, 0, 0))]
        args += list(states)
    in_specs += [pl.BlockSpec((t, HEAD_C), lambda b, c: (c, 0)),
                 pl.BlockSpec((t, HEAD_C), lambda b, c: (c, 0)),
                 full(H_C, t, t), full(H_C, t, LANES), full(H_C, t, LANES),
                 full(1, D_MIX), full(LANES, H_D * HEAD_DK), full(1, H_D * HEAD_DK), full(1, D_MIX)]
    args += [cosf, sinf, seg, pw, dte, p['ret_norm_w'], p['gate_up'], p['gate_b'], p['gla_norm_w']]
    out_shape = (jax.ShapeDtypeStruct((nseq, length, D_MODEL), BF16),
                 jax.ShapeDtypeStruct((nseq, H_C, HEAD_C, HEAD_C), F32),
                 jax.ShapeDtypeStruct((nseq, H_D, HEAD_DK, HEAD_DV), F32))
    out_specs = (pl.BlockSpec((1, t, D_MODEL), lambda b, c: (b, c, 0)),
                 pl.BlockSpec((1, H_C, HEAD_C, HEAD_C), lambda b, c: (b, 0, 0, 0)),
                 pl.BlockSpec((1, H_D, HEAD_DK, HEAD_DV), lambda b, c: (b, 0, 0, 0)))
    scratch = [pltpu.VMEM((H_C, HEAD_C, HEAD_C), F32), pltpu.VMEM((H_D, HEAD_DV, HEAD_DK), F32),
               pltpu.VMEM((t, H_D * HEAD_DK), F32), pltpu.VMEM((t, D_MODEL), F32)]
    y, ret, gla = pl.pallas_call(
        functools.partial(_odd_kernel, t=t, tsub=tsub, nchunks=nchunks, lv=lv, has_init=has_init),
        grid=(nseq, nchunks),
        in_specs=in_specs, out_specs=out_specs, out_shape=out_shape, scratch_shapes=scratch,
        compiler_params=pltpu.CompilerParams(dimension_semantics=("arbitrary", "arbitrary"),
                                             vmem_limit_bytes=VMEM_LIMIT),
        name="odd_mixer",
    )(*args)
    return y, (ret, gla)


def _prep_even(i, W):
    w_in = W['ev_w_in'][i]
    w_in = jnp.concatenate([w_in[:, :RW_COLS], jnp.zeros((D_MODEL, RW_PAD - RW_COLS), F32),
                            w_in[:, RW_COLS:], jnp.zeros((D_MODEL, MB_PAD - MB_COLS), F32)], axis=1)
    mu = jnp.pad(W['rw_mu'][i], (0, RW_PAD - RW_COLS)).reshape(1, RW_PAD)
    rep = lambda v: jnp.repeat(v, HEAD_B)
    rows = [W['rw_w0'][i], W['rw_a0'][i], W['rw_k_k'][i], W['rw_k_a'][i], W['rw_r_k'][i].reshape(-1),
            W['rw_gn_w'][i], W['rw_gn_b'][i], rep(W['mb_d'][i]), W['mb_norm_w'][i]]
    vec = jnp.concatenate([jnp.stack(rows), jnp.zeros((EV_VEC_ROWS - len(rows), D_MIX), F32)], axis=0)
    wa = jnp.zeros((LANES, 2 * D_MIX), F32)
    wa = wa.at[:RW_LORA_W, :D_MIX].set(W['rw_w_up'][i]).at[RW_LORA_W:, D_MIX:].set(W['rw_a_up'][i])
    gup = jnp.pad(W['rw_g_up'][i], ((0, RW_PAD - 3 * D_MIX - LANES - RW_LORA_G), (0, 0)))
    pad_h = lambda v: jnp.pad(v, (0, LANES - H_B)).reshape(1, LANES)
    expand = np.zeros((LANES, D_MIX), np.float32)
    for h in range(H_B):
        expand[h, h * HEAD_B:(h + 1) * HEAD_B] = 1.0
    return dict(w_in=w_in.astype(BF16), w_out=W['ev_w_out'][i].astype(BF16), mu=mu, vec=vec,
                wa=wa.astype(BF16), gup=gup.astype(BF16), conv_w=W['mb_conv_w'][i],
                conv_b=W['mb_conv_b'][i].reshape(1, MB_CONV_CH), dt_bias=pad_h(W['mb_dt_bias'][i]),
                a_log=pad_h(W['mb_a_log'][i]), expand=jnp.asarray(expand, BF16))


def _prep_odd(i, W):
    w_in = W['od_w_in'][i]
    gd0 = OD_COLS - D_MIX - GLA_LORA
    w_in = jnp.concatenate([w_in[:, :gd0], w_in[:, gd0 + GLA_LORA:], w_in[:, gd0:gd0 + GLA_LORA],
                            jnp.zeros((D_MODEL, OD_PAD - OD_COLS), F32)], axis=1)
    gate_up = jnp.pad(W['gla_gate_up'][i], ((0, LANES - GLA_LORA), (0, 0)))
    return dict(w_in=w_in.astype(BF16), w_out=W['od_w_out'][i].astype(BF16),
                ret_norm_w=W['ret_norm_w'][i].reshape(1, D_MIX), gate_up=gate_up.astype(BF16),
                gate_b=W['gla_gate_b'][i].reshape(1, H_D * HEAD_DK),
                gla_norm_w=W['gla_norm_w'][i].reshape(1, D_MIX))


def _trunk(h, mod, states, P, FW, final_g, *, nseq, length, t, lv, pos0, seq_mod):
    outs = ([], [], [], [], [], [])
    kw = dict(seq_mod=seq_mod, rows_per_seq=length)
    for l in range(DEPTH):
        i = l // 2
        p = P[l]
        wg, wu, wd = FW[l]
        h = _ffn(h, mod[l], 0, wg[0], wu[0], wd[0], **kw)
        z = _inproj(h, mod[l], p['w_in'], **kw).reshape(nseq, length, -1)
        if l % 2 == 0:
            st = None if states is None else tuple(s[i] for s in states[:4])
            y, new = _even_mixer(z, st, p, t=t, lv=lv)
            for lst, n in zip(outs[:4], new):
                lst.append(n)
        else:
            st = None if states is None else tuple(s[i] for s in states[4:])
            y, new = _odd_mixer(z, st, p, t=t, lv=lv, pos0=pos0)
            for lst, n in zip(outs[4:], new):
                lst.append(n)
        h = _outproj(y.reshape(nseq * length, D_MODEL), p['w_out'], h, mod[l], **kw)
        h = _ffn(h, mod[l], 6, wg[1], wu[1], wd[1], final_g=final_g if l == DEPTH - 1 else None, **kw)
    return h, tuple(jnp.stack(lst) for lst in outs)


def kernel(x_prompt, x_sample, state_rwkv_shift, state_rwkv_wkv, state_mamba_conv, state_mamba_ssm,
           state_ret, state_gla, c_prompt, c_sample, ada_w, ada_b, ffn_wg, ffn_wu, ffn_wd, ev_w_in,
           ev_w_out, rw_mu, rw_w0, rw_w_up, rw_a0, rw_a_up, rw_g_up, rw_k_k, rw_k_a, rw_r_k, rw_gn_w,
           rw_gn_b, mb_conv_w, mb_conv_b, mb_dt_bias, mb_a_log, mb_d, mb_norm_w, od_w_in, od_w_out,
           ret_norm_w, gla_gate_up, gla_gate_b, gla_norm_w, final_g):
    W = dict(ev_w_in=ev_w_in, ev_w_out=ev_w_out, rw_mu=rw_mu, rw_w0=rw_w0, rw_w_up=rw_w_up,
             rw_a0=rw_a0, rw_a_up=rw_a_up, rw_g_up=rw_g_up, rw_k_k=rw_k_k, rw_k_a=rw_k_a,
             rw_r_k=rw_r_k, rw_gn_w=rw_gn_w, rw_gn_b=rw_gn_b, mb_conv_w=mb_conv_w,
             mb_conv_b=mb_conv_b, mb_dt_bias=mb_dt_bias, mb_a_log=mb_a_log, mb_d=mb_d,
             mb_norm_w=mb_norm_w, od_w_in=od_w_in, od_w_out=od_w_out, ret_norm_w=ret_norm_w,
             gla_gate_up=gla_gate_up, gla_gate_b=gla_gate_b, gla_norm_w=gla_norm_w)
    nb, seq, _ = x_prompt.shape
    db, dseq, _ = x_sample.shape
    P = [(_prep_even if l % 2 == 0 else _prep_odd)(l // 2, W) for l in range(DEPTH)]
    FW = [(ffn_wg[l].astype(BF16), ffn_wu[l].astype(BF16), ffn_wd[l].astype(BF16)) for l in range(DEPTH)]

    rows = nb + db
    rows_pad = -(-rows // SUBLANES) * SUBLANES
    c_all = jnp.concatenate([c_prompt, c_sample, jnp.zeros((rows_pad - rows, D_MODEL), F32)], axis=0)
    mod_all = _ada(c_all, ada_w, ada_b)
    mod_p = [mod_all[l, :nb].reshape(nb * N_MOD, 1, D_MODEL) for l in range(DEPTH)]
    mod_s = [jnp.repeat(mod_all[l, nb:rows], SAMPLE_CHUNK, axis=0) for l in range(DEPTH)]

    y_p, st_p = _trunk(x_prompt.reshape(nb * seq, D_MODEL), mod_p, None, P, FW, final_g,
                       nseq=nb, length=seq, t=PROMPT_CHUNK, lv=PROMPT_CHUNK, pos0=0.0, seq_mod=True)

    xs = jnp.pad(x_sample, ((0, 0), (0, SAMPLE_CHUNK - dseq), (0, 0)))
    states = (state_rwkv_shift, state_rwkv_wkv, state_mamba_conv, state_mamba_ssm, state_ret, state_gla)
    y_s, st_s = _trunk(xs.reshape(db * SAMPLE_CHUNK, D_MODEL), mod_s, states, P, FW, final_g,
                       nseq=db, length=SAMPLE_CHUNK, t=SAMPLE_CHUNK, lv=dseq, pos0=float(PAST_LEN),
                       seq_mod=False)
    y_s = y_s.reshape(db, SAMPLE_CHUNK, D_MODEL)[:, :dseq]
    return (y_p.reshape(nb, seq, D_MODEL), y_s) + st_p + st_s
```

```python
import functools
import math

import numpy as np
import jax
import jax.numpy as jnp
from jax import lax
from jax.experimental import pallas as pl
from jax.experimental.pallas import tpu as pltpu

F32 = jnp.float32
BF16 = jnp.bfloat16

D_MODEL = 2048
DEPTH = 4
PAST_LEN = 16384
N_MOD = 9
D_FF = 5632

D_MIX = 1024
H_A, HEAD_A = 16, 64
RW_LORA_W, RW_LORA_A, RW_LORA_G = 64, 64, 160
RW_COLS = 3 * D_MIX + RW_LORA_W + RW_LORA_A + RW_LORA_G
RW_GN_EPS = 64e-5
H_B, HEAD_B, D_STATE, N_GROUPS, CONV_W = 16, 64, 128, 2, 4
MB_CONV_CH = D_MIX + 2 * N_GROUPS * D_STATE
MB_COLS = D_MIX + MB_CONV_CH + H_B
H_C, HEAD_C = 8, 128
ROPE_BASE = 10000.0
H_D, HEAD_DK, HEAD_DV, GLA_LORA = 4, 128, 256, 16
GLA_NORMALIZER = 16.0
OD_COLS = 4 * D_MIX + 2 * H_D * HEAD_DK + 2 * D_MIX + GLA_LORA

LANES = 128
SUBLANES = 8
VMEM_LIMIT = 56 * 1024 * 1024

RW_PAD = 3456
ZM = RW_PAD
MB_PAD = 2688
EV_PAD = RW_PAD + MB_PAD
OD_PAD = 7296
OD_GD = 7168

PROMPT_CHUNK = 64
GLA_SUB = 32
SAMPLE_CHUNK = 8


def _mm(a, b):
    return jnp.dot(a.astype(BF16), b.astype(BF16), preferred_element_type=F32)


def _mm_nt(a, b):
    return lax.dot_general(a.astype(BF16), b.astype(BF16), (((1,), (1,)), ((), ())),
                           preferred_element_type=F32)


def _mm_tn(a, b):
    return lax.dot_general(a.astype(BF16), b.astype(BF16), (((0,), (0,)), ((), ())),
                           preferred_element_type=F32)


def _split3(x):
    hi = x.astype(BF16)
    r = x - hi.astype(F32)
    mid = r.astype(BF16)
    lo = (r - mid.astype(F32)).astype(BF16)
    return hi, mid, lo


def _mm_sel(sel, x):
    s = sel.astype(BF16)
    hi, mid, lo = _split3(x)
    return (jnp.dot(s, hi, preferred_element_type=F32) + jnp.dot(s, mid, preferred_element_type=F32)
            + jnp.dot(s, lo, preferred_element_type=F32))


def _mm_tn_sel(x, sel):
    s = sel.astype(BF16)
    dn = (((0,), (0,)), ((), ()))
    hi, mid, lo = _split3(x)
    return (lax.dot_general(hi, s, dn, preferred_element_type=F32)
            + lax.dot_general(mid, s, dn, preferred_element_type=F32)
            + lax.dot_general(lo, s, dn, preferred_element_type=F32))


def _mm_xsel(x, sel):
    s = sel.astype(BF16)
    hi, mid, lo = _split3(x)
    return (jnp.dot(hi, s, preferred_element_type=F32) + jnp.dot(mid, s, preferred_element_type=F32)
            + jnp.dot(lo, s, preferred_element_type=F32))


def _sigmoid(x):
    return 1.0 / (1.0 + jnp.exp(-x))


def _silu(x):
    return x * _sigmoid(x)


def _softplus(x):
    return jnp.maximum(x, 0.0) + jnp.log(1.0 + jnp.exp(-jnp.abs(x)))


def _rms(x, eps):
    return x * lax.rsqrt(jnp.mean(x * x, axis=-1, keepdims=True) + eps)


def _tri(n, strict):
    r = lax.broadcasted_iota(jnp.int32, (n, n), 0)
    c = lax.broadcasted_iota(jnp.int32, (n, n), 1)
    return (r > c) if strict else (r >= c)


def _mod_specs(mc, tm, idxs):
    specs = []
    l = mc['l']
    for idx in idxs:
        if mc['seq_mod']:
            base, rps = l * mc['seqs'], mc['rows_per_seq']
            specs.append(pl.BlockSpec(
                (1, 1, D_MODEL),
                lambda i, j, idx=idx, base=base, rps=rps: ((base + i * tm // rps) * N_MOD + idx, 0, 0)))
        else:
            specs.append(pl.BlockSpec((None, tm, D_MODEL), lambda i, j, idx=idx, l=l: (l, i, idx)))
    return specs


def _mod_val(ref, seq_mod):
    return ref[0] if seq_mod else ref[...]


def _row_tile(m, seq_mod):
    return min(512 if seq_mod else 256, m)


def _ada_kernel(c_ref, w_ref, b_ref, o_ref):
    x = _silu(c_ref[...]).astype(BF16)
    o_ref[...] = jnp.dot(x, w_ref[...].astype(BF16), preferred_element_type=F32) + b_ref[...]


def _ada(c, ada_w, ada_b):
    rows = c.shape[0]
    n = ada_w.shape[-1]
    tn = 1024
    return pl.pallas_call(
        _ada_kernel,
        grid=(DEPTH, n // tn),
        in_specs=[pl.BlockSpec((rows, D_MODEL), lambda l, j: (0, 0)),
                  pl.BlockSpec((None, D_MODEL, tn), lambda l, j: (l, 0, j)),
                  pl.BlockSpec((None, 1, tn), lambda l, j: (l, 0, j))],
        out_specs=pl.BlockSpec((None, rows, tn), lambda l, j: (l, 0, j)),
        out_shape=jax.ShapeDtypeStruct((DEPTH, rows, n), F32),
        compiler_params=pltpu.CompilerParams(dimension_semantics=("arbitrary", "arbitrary"),
                                             vmem_limit_bytes=VMEM_LIMIT),
        name="ada",
    )(c, ada_w, ada_b.reshape(DEPTH, 1, n))


def _ffn_kernel(*refs, seq_mod, nj, final):
    if final:
        h_ref, sh_ref, sc_ref, g_ref, wg_ref, wu_ref, wd_ref, fin_ref, o_ref, xn_ref, acc_ref = refs
    else:
        h_ref, sh_ref, sc_ref, g_ref, wg_ref, wu_ref, wd_ref, o_ref, xn_ref, acc_ref = refs
    j = pl.program_id(1)

    @pl.when(j == 0)
    def _():
        xn = _rms(h_ref[...], 1e-6) * (1.0 + _mod_val(sc_ref, seq_mod)) + _mod_val(sh_ref, seq_mod)
        xn_ref[...] = xn.astype(BF16)
        acc_ref[...] = jnp.zeros_like(acc_ref)

    xn = xn_ref[...]
    gate = jnp.dot(xn, wg_ref[...], preferred_element_type=F32)
    up = jnp.dot(xn, wu_ref[...], preferred_element_type=F32)
    act = (_silu(gate) * up).astype(BF16)
    acc_ref[...] += jnp.dot(act, wd_ref[...], preferred_element_type=F32)

    @pl.when(j == nj - 1)
    def _():
        out = h_ref[...] + 0.5 * (1.0 + _mod_val(g_ref, seq_mod)) * acc_ref[...]
        if final:
            out = _rms(out, 1e-6) * fin_ref[...]
        o_ref[...] = out


def _ffn(h, mod, mc, mod_base, wg, wu, wd, k, final_g=None):
    m = h.shape[0]
    seq_mod, l = mc['seq_mod'], mc['l']
    tm = _row_tile(m, seq_mod)
    tf = 512
    nj = D_FF // tf
    final = final_g is not None
    in_specs = [pl.BlockSpec((tm, D_MODEL), lambda i, j: (i, 0))]
    in_specs += _mod_specs(mc, tm, (mod_base, mod_base + 1, mod_base + 2))
    in_specs += [pl.BlockSpec((None, None, D_MODEL, tf), lambda i, j: (l, k, 0, j)),
                 pl.BlockSpec((None, None, D_MODEL, tf), lambda i, j: (l, k, 0, j)),
                 pl.BlockSpec((None, None, tf, D_MODEL), lambda i, j: (l, k, j, 0))]
    args = [h, mod, mod, mod, wg, wu, wd]
    if final:
        in_specs.append(pl.BlockSpec((1, D_MODEL), lambda i, j: (0, 0)))
        args.append(final_g.reshape(1, D_MODEL))
    return pl.pallas_call(
        functools.partial(_ffn_kernel, seq_mod=seq_mod, nj=nj, final=final),
        grid=(m // tm, nj),
        in_specs=in_specs,
        out_specs=pl.BlockSpec((tm, D_MODEL), lambda i, j: (i, 0)),
        out_shape=jax.ShapeDtypeStruct((m, D_MODEL), F32),
        scratch_shapes=[pltpu.VMEM((tm, D_MODEL), BF16), pltpu.VMEM((tm, D_MODEL), F32)],
        compiler_params=pltpu.CompilerParams(dimension_semantics=("arbitrary", "arbitrary"),
                                             vmem_limit_bytes=VMEM_LIMIT),
        name="ffn",
    )(*args)


def _inproj_kernel(h_ref, sh_ref, sc_ref, w_ref, o_ref, xn_ref, *, seq_mod):
    @pl.when(pl.program_id(1) == 0)
    def _():
        xn = _rms(h_ref[...], 1e-6) * (1.0 + _mod_val(sc_ref, seq_mod)) + _mod_val(sh_ref, seq_mod)
        xn_ref[...] = xn.astype(BF16)

    o_ref[...] = jnp.dot(xn_ref[...], w_ref[...], preferred_element_type=F32)


def _inproj(h, mod, mc, w, li):
    m = h.shape[0]
    n = w.shape[2]
    seq_mod = mc['seq_mod']
    tm = _row_tile(m, seq_mod)
    tn = n // 3
    in_specs = [pl.BlockSpec((tm, D_MODEL), lambda i, j: (i, 0))]
    in_specs += _mod_specs(mc, tm, (3, 4))
    in_specs += [pl.BlockSpec((None, D_MODEL, tn), lambda i, j: (li, 0, j))]
    return pl.pallas_call(
        functools.partial(_inproj_kernel, seq_mod=seq_mod),
        grid=(m // tm, n // tn),
        in_specs=in_specs,
        out_specs=pl.BlockSpec((tm, tn), lambda i, j: (i, j)),
        out_shape=jax.ShapeDtypeStruct((m, n), F32),
        scratch_shapes=[pltpu.VMEM((tm, D_MODEL), BF16)],
        compiler_params=pltpu.CompilerParams(dimension_semantics=("arbitrary", "arbitrary"),
                                             vmem_limit_bytes=VMEM_LIMIT),
        name="inproj",
    )(h, mod, mod, w)


def _outproj_kernel(y_ref, w_ref, h_ref, g_ref, o_ref, *, seq_mod):
    mix = jnp.dot(y_ref[...], w_ref[...], preferred_element_type=F32)
    o_ref[...] = h_ref[...] + (1.0 + _mod_val(g_ref, seq_mod)) * mix


def _outproj(y, w, li, h, mod, mc):
    m = h.shape[0]
    seq_mod = mc['seq_mod']
    tm = _row_tile(m, seq_mod)
    in_specs = [pl.BlockSpec((tm, D_MODEL), lambda i, j: (i, 0)),
                pl.BlockSpec((None, D_MODEL, D_MODEL), lambda i, j: (li, 0, 0)),
                pl.BlockSpec((tm, D_MODEL), lambda i, j: (i, 0))]
    in_specs += _mod_specs(mc, tm, (5,))
    return pl.pallas_call(
        functools.partial(_outproj_kernel, seq_mod=seq_mod),
        grid=(m // tm, 1),
        in_specs=in_specs,
        out_specs=pl.BlockSpec((tm, D_MODEL), lambda i, j: (i, 0)),
        out_shape=jax.ShapeDtypeStruct((m, D_MODEL), F32),
        compiler_params=pltpu.CompilerParams(dimension_semantics=("arbitrary", "arbitrary"),
                                             vmem_limit_bytes=VMEM_LIMIT),
        name="outproj",
    )(y, w, h, mod)


EV_W0, EV_A0, EV_KK, EV_KA, EV_RK, EV_GNW, EV_GNB, EV_D, EV_NW = range(9)
EV_VEC_ROWS = 16


def _even_kernel(*refs, t, nchunks, lv, has_init, hg):
    if has_init:
        (z_ref, shift0_ref, wkv0_ref, conv0_ref, ssm0_ref, *rest) = refs
    else:
        z_ref, *rest = refs
    (mu_ref, vec_ref, wa_ref, gup_ref, cw_ref, cb_ref, dtb_ref, alog_ref, exp_ref,
     y_ref, shift_o, wkv_o, conv_o, ssm_o,
     carry_ref, wkv_ref, ext_ref, ssm_ref,
     r_s, kp_s, v_s, kk_s, a_s, c_s, ld_s, g_s, y_s) = rest
    ci = pl.program_id(1)
    masked = lv < t
    nlog = int(math.log2(t))

    @pl.when(ci == 0)
    def _():
        if has_init:
            carry_ref[:, 0:RW_COLS] = shift0_ref[0]
            carry_ref[:, RW_COLS:RW_PAD] = jnp.zeros((1, RW_PAD - RW_COLS), F32)
            wkv_ref[...] = wkv0_ref[0]
            ext_ref[0:SUBLANES - CONV_W + 1, :] = jnp.zeros((SUBLANES - CONV_W + 1, MB_CONV_CH), F32)
            ext_ref[SUBLANES - CONV_W + 1:SUBLANES, :] = conv0_ref[0]
            for h in range(H_B):
                ssm_ref[:, h * HEAD_B:(h + 1) * HEAD_B] = ssm0_ref[0, h]
        else:
            carry_ref[...] = jnp.zeros_like(carry_ref)
            wkv_ref[...] = jnp.zeros_like(wkv_ref)
            ext_ref[0:SUBLANES, :] = jnp.zeros((SUBLANES, MB_CONV_CH), F32)
            ssm_ref[...] = jnp.zeros_like(ssm_ref)

    row = lax.broadcasted_iota(jnp.int32, (t, 1), 0)
    valid = row < lv
    strict = _tri(t, True)
    incl = _tri(t, False)

    def vec(i, lo=0, hi=D_MIX):
        return vec_ref[i:i + 1, lo:hi]

    def shifted(a, b):
        cur = z_ref[0, :, a:b]
        prev = jnp.where(row == 0, carry_ref[:, a:b], pltpu.roll(cur, 1, axis=0))
        return cur + mu_ref[:, a:b] * (prev - cur)

    r_s[...] = shifted(0, D_MIX)
    k = shifted(D_MIX, 2 * D_MIX)
    v_s[...] = shifted(2 * D_MIX, 3 * D_MIX)
    lo = shifted(3 * D_MIX, 3 * D_MIX + LANES)
    lane = lax.broadcasted_iota(jnp.int32, (t, LANES), 1)
    wa = _mm(jnp.where(lane < RW_LORA_W, jnp.tanh(lo), lo), wa_ref[...])
    w_log = -_softplus(-(vec(EV_W0) + wa[:, 0:D_MIX])) - 0.5
    ld = -jnp.exp(w_log)
    a = _sigmoid(vec(EV_A0) + wa[:, D_MIX:2 * D_MIX])
    g_s[...] = _mm(_sigmoid(shifted(3 * D_MIX + LANES, RW_PAD)), gup_ref[...])
    kp = k * (1.0 + (a - 1.0) * vec(EV_KA))
    if masked:
        ld = jnp.where(valid, ld, 0.0)
        kp = jnp.where(valid, kp, 0.0)
        a = jnp.where(valid, a, 0.0)
    kk_s[...] = k * vec(EV_KK)
    kp_s[...] = kp
    a_s[...] = a
    ld_s[...] = ld
    c_s[...] = _mm_sel(incl, ld)
    last_row = lv - 1
    carry_ref[...] = z_ref[0, last_row:last_row + 1, 0:RW_PAD]

    for h0 in range(0, H_A, hg):
        heads = range(h0, h0 + hg)
        sls = [slice(h * HEAD_A, (h + 1) * HEAD_A) for h in heads]
        rs = [r_s[:, sl] for sl in sls]
        kps = [kp_s[:, sl] for sl in sls]
        vs = [v_s[:, sl] for sl in sls]
        cs = [c_s[:, sl] for sl in sls]
        kks, bs = [], []
        for sl in sls:
            kkr = kk_s[:, sl]
            nrm = jnp.sqrt(jnp.sum(kkr * kkr, axis=-1, keepdims=True))
            kks.append(kkr / jnp.maximum(nrm, 1e-12))
            bs.append(kks[-1] * a_s[:, sl])
        ecs = [jnp.exp(c) for c in cs]
        ecis = [jnp.exp(-c) for c in cs]
        lhss = [jnp.concatenate([kk * jnp.exp(c - ld_s[:, sl]), r * ec], axis=0)
                for kk, c, sl, r, ec in zip(kks, cs, sls, rs, ecs)]
        s0s = [wkv_ref[h] for h in heads]
        a1s = [_mm_nt(lhs, kp * eci) for lhs, kp, eci in zip(lhss, kps, ecis)]
        a2s = [_mm_nt(lhs, b * eci) for lhs, b, eci in zip(lhss, bs, ecis)]
        pms = [_mm_nt(lhs, s0) for lhs, s0 in zip(lhss, s0s)]
        us = [pm[0:t] + _mm(jnp.where(strict, a1[0:t], 0.0), v) for pm, a1, v in zip(pms, a1s, vs)]
        ps = [jnp.where(strict, -a2[0:t], 0.0) for a2 in a2s]
        us = [u + _mm(p, u) for u, p in zip(us, ps)]
        for _ in range(nlog - 1):
            ps = [_mm(p, p) for p in ps]
            us = [u + _mm(p, u) for u, p in zip(us, ps)]
        ys = [pm[t:2 * t] + _mm(jnp.where(incl, a1[t:2 * t], 0.0), v) - _mm(jnp.where(incl, a2[t:2 * t], 0.0), u)
              for pm, a1, a2, v, u in zip(pms, a1s, a2s, vs, us)]
        for i, h in enumerate(heads):
            c, kp, b, v, u = cs[i], kps[i], bs[i], vs[i], us[i]
            dec_end = jnp.exp(c[t - 1:t] - c)
            vu = jnp.concatenate([v, u], axis=0)
            kb = jnp.concatenate([kp * dec_end, -(b * dec_end)], axis=0)
            wkv_ref[h] = s0s[i] * ecs[i][t - 1:t] + _mm_tn(vu, kb)
        for i, sl in enumerate(sls):
            y, r, kp, v = ys[i], rs[i], kps[i], vs[i]
            mean = jnp.mean(y, axis=-1, keepdims=True)
            yc = y - mean
            var = jnp.mean(yc * yc, axis=-1, keepdims=True)
            yn = (yc * lax.rsqrt(var + RW_GN_EPS) * vec(EV_GNW, sl.start, sl.stop)
                  + vec(EV_GNB, sl.start, sl.stop))
            bonus = jnp.sum(r * kp * vec(EV_RK, sl.start, sl.stop), axis=-1, keepdims=True) * v
            y_s[:, sl] = (yn + bonus) * g_s[:, sl]

    zg = z_ref[0, :, ZM:ZM + D_MIX]
    ext_ref[SUBLANES:SUBLANES + t, :] = z_ref[0, :, ZM + D_MIX:ZM + D_MIX + MB_CONV_CH]
    conv = cb_ref[...]
    for w in range(CONV_W):
        off = SUBLANES - (CONV_W - 1) + w
        conv = conv + cw_ref[w:w + 1, :] * ext_ref[off:off + t, :]
    new_conv = ext_ref[SUBLANES + lv - (CONV_W - 1):SUBLANES + lv, :]
    ext_ref[SUBLANES - (CONV_W - 1):SUBLANES, :] = new_conv
    xbc = _silu(conv)
    xs = xbc[:, 0:D_MIX]
    dt = _softplus(z_ref[0, :, ZM + D_MIX + MB_CONV_CH:ZM + MB_PAD] + dtb_ref[...])
    logd = dt * (-jnp.exp(alog_ref[...]))
    if masked:
        logd = jnp.where(valid, logd, 0.0)
    c = _mm_sel(incl, logd)
    upper = lax.broadcasted_iota(jnp.int32, (t, t), 0) <= lax.broadcasted_iota(jnp.int32, (t, t), 1)
    c_t = _mm_tn_sel(logd, upper)
    wide = _mm_xsel(jnp.concatenate([dt, c, c[t - 1:t] - c], axis=0), exp_ref[...])
    xdt = xs * wide[0:t]
    if masked:
        xdt = jnp.where(valid, xdt, 0.0)
    ec_w = jnp.exp(wide[t:2 * t])
    dec_end_w = jnp.exp(wide[2 * t:3 * t])
    ecl_w = ec_w[t - 1:t]
    gw = (H_B // N_GROUPS) * HEAD_B
    hpg = H_B // N_GROUPS
    bgs = [xbc[:, D_MIX + g * D_STATE:D_MIX + (g + 1) * D_STATE] for g in range(N_GROUPS)]
    cgs = [xbc[:, D_MIX + (N_GROUPS + g) * D_STATE:D_MIX + (N_GROUPS + g + 1) * D_STATE]
           for g in range(N_GROUPS)]
    scores = [_mm_nt(cg, bg) for cg, bg in zip(cgs, bgs)]
    s_prevs = [ssm_ref[:, g * gw:(g + 1) * gw] for g in range(N_GROUPS)]
    inters = [_mm(cg, sp) for cg, sp in zip(cgs, s_prevs)]
    segs = [jnp.where(incl, jnp.exp(jnp.where(incl, c[:, h:h + 1] - c_t[h:h + 1, :], 0.0)), 0.0)
            for h in range(H_B)]
    intra = [_mm(scores[h // hpg] * segs[h], xdt[:, h * HEAD_B:(h + 1) * HEAD_B]) for h in range(H_B)]
    for g in range(N_GROUPS):
        gs = slice(g * gw, (g + 1) * gw)
        ssm_ref[:, gs] = s_prevs[g] * ecl_w[:, gs] + _mm_tn(bgs[g], xdt[:, gs] * dec_end_w[:, gs])
    o = jnp.concatenate(intra, axis=1) + jnp.concatenate(inters, axis=1) * ec_w
    yb = (o + vec(EV_D) * xs) * _silu(zg)
    y_s[:, D_MIX:2 * D_MIX] = _rms(yb, 1e-5) * vec(EV_NW)
    y_ref[0] = y_s[...].astype(BF16)

    @pl.when(ci == nchunks - 1)
    def _():
        shift_o[0] = carry_ref[...]
        wkv_o[0] = wkv_ref[...]
        conv_o[0] = new_conv
        for h in range(H_B):
            ssm_o[0, h] = ssm_ref[:, h * HEAD_B:(h + 1) * HEAD_B]


def _even_mixer(z, states, li, p, *, t, lv):
    nseq, length, _ = z.shape
    nchunks = length // t
    has_init = states is not None
    assert lv == t or nchunks == 1
    full = lambda *shape: pl.BlockSpec(shape, lambda b, c: (0,) * len(shape))
    in_specs = [pl.BlockSpec((1, t, EV_PAD), lambda b, c: (b, c, 0))]
    args = [z]
    if has_init:
        shift0, wkv0, conv0, ssm0 = states
        in_specs += [pl.BlockSpec((None, 1, 1, RW_COLS), lambda b, c: (li, b, 0, 0)),
                     pl.BlockSpec((None, 1, H_A, HEAD_A, HEAD_A), lambda b, c: (li, b, 0, 0, 0)),
                     pl.BlockSpec((None, 1, CONV_W - 1, MB_CONV_CH), lambda b, c: (li, b, 0, 0)),
                     pl.BlockSpec((None, 1, H_B, D_STATE, HEAD_B), lambda b, c: (li, b, 0, 0, 0))]
        args += [shift0.reshape(-1, nseq, 1, RW_COLS), wkv0, conv0, ssm0]
    in_specs += [full(1, RW_PAD), full(EV_VEC_ROWS, D_MIX), full(LANES, 2 * D_MIX),
                 full(RW_PAD - 3 * D_MIX - LANES, D_MIX), full(CONV_W, MB_CONV_CH),
                 full(1, MB_CONV_CH), full(1, LANES), full(1, LANES), full(LANES, D_MIX)]
    args += [p['mu'], p['vec'], p['wa'], p['gup'], p['conv_w'], p['conv_b'], p['dt_bias'],
             p['a_log'], p['expand']]
    out_shape = (jax.ShapeDtypeStruct((nseq, length, D_MODEL), BF16),
                 jax.ShapeDtypeStruct((nseq, 1, RW_PAD), F32),
                 jax.ShapeDtypeStruct((nseq, H_A, HEAD_A, HEAD_A), F32),
                 jax.ShapeDtypeStruct((nseq, CONV_W - 1, MB_CONV_CH), F32),
                 jax.ShapeDtypeStruct((nseq, H_B, D_STATE, HEAD_B), F32))
    out_specs = (pl.BlockSpec((1, t, D_MODEL), lambda b, c: (b, c, 0)),
                 pl.BlockSpec((1, 1, RW_PAD), lambda b, c: (b, 0, 0)),
                 pl.BlockSpec((1, H_A, HEAD_A, HEAD_A), lambda b, c: (b, 0, 0, 0)),
                 pl.BlockSpec((1, CONV_W - 1, MB_CONV_CH), lambda b, c: (b, 0, 0)),
                 pl.BlockSpec((1, H_B, D_STATE, HEAD_B), lambda b, c: (b, 0, 0, 0)))
    wide = lambda: pltpu.VMEM((t, D_MIX), F32)
    scratch = [pltpu.VMEM((1, RW_PAD), F32), pltpu.VMEM((H_A, HEAD_A, HEAD_A), F32),
               pltpu.VMEM((t + SUBLANES, MB_CONV_CH), F32), pltpu.VMEM((D_STATE, H_B * HEAD_B), F32)]
    scratch += [wide() for _ in range(8)] + [pltpu.VMEM((t, D_MODEL), F32)]
    y, shift, wkv, conv, ssm = pl.pallas_call(
        functools.partial(_even_kernel, t=t, nchunks=nchunks, lv=lv, has_init=has_init,
                          hg=H_A),
        grid=(nseq, nchunks),
        in_specs=in_specs, out_specs=out_specs, out_shape=out_shape, scratch_shapes=scratch,
        compiler_params=pltpu.CompilerParams(dimension_semantics=("arbitrary", "arbitrary"),
                                             vmem_limit_bytes=VMEM_LIMIT),
        name="even_mixer",
    )(*args)
    return y, (shift[:, 0, :RW_COLS], wkv, conv, ssm)


def _odd_kernel(*refs, t, tsub, nchunks, lv, has_init):
    if has_init:
        z_ref, ret0_ref, gla0_ref, *rest = refs
    else:
        z_ref, *rest = refs
    (cos_ref, sin_ref, seg_ref, pw_ref, dte_ref, rnw_ref, gup_ref, gb_ref, gnw_ref,
     y_ref, ret_o, gla_o, ret_ref, gla_ref, la_s, y_s) = rest
    ci = pl.program_id(1)
    masked = lv < t

    @pl.when(ci == 0)
    def _():
        if has_init:
            ret_ref[...] = ret0_ref[0]
            for h in range(H_D):
                gla_ref[h] = jnp.transpose(gla0_ref[0, h])
        else:
            ret_ref[...] = jnp.zeros_like(ret_ref)
            gla_ref[...] = jnp.zeros_like(gla_ref)

    row = lax.broadcasted_iota(jnp.int32, (t, 1), 0)
    valid = row < lv

    cosf = cos_ref[...]
    sinf = sin_ref[...]

    def rotary(x):
        return x * cosf + pltpu.roll(x, HEAD_C // 2, axis=1) * sinf

    hc = range(H_C)
    qs = [rotary(z_ref[0, :, h * HEAD_C:(h + 1) * HEAD_C]) * HEAD_C ** -0.5 for h in hc]
    ks = [rotary(z_ref[0, :, D_MIX + h * HEAD_C:D_MIX + (h + 1) * HEAD_C]) for h in hc]
    if masked:
        ks = [jnp.where(valid, k, 0.0) for k in ks]
    vs = [z_ref[0, :, 2 * D_MIX + h * HEAD_C:2 * D_MIX + (h + 1) * HEAD_C] for h in hc]
    s0s = [ret_ref[h] for h in hc]
    sc = [_mm_nt(q, k) * seg_ref[h] for h, q, k in zip(hc, qs, ks)]
    inter = [_mm(q, s0) * pw_ref[h] for h, q, s0 in zip(hc, qs, s0s)]
    outs = [_mm(s, v) + i for s, v, i in zip(sc, vs, inter)]
    for h in hc:
        gamma = 1.0 - 2.0 ** (-5.0 - h)
        ret_ref[h] = s0s[h] * gamma ** lv + _mm_tn(ks[h] * dte_ref[h], vs[h])
    for h in hc:
        sl = slice(h * HEAD_C, (h + 1) * HEAD_C)
        rg = z_ref[0, :, 3 * D_MIX + h * HEAD_C:3 * D_MIX + (h + 1) * HEAD_C]
        y_s[:, sl] = _rms(outs[h], 1e-6) * rnw_ref[:, sl] * _silu(rg)

    gq0 = 4 * D_MIX
    gk0 = gq0 + H_D * HEAD_DK
    gv0 = gk0 + H_D * HEAD_DK
    gr0 = gv0 + D_MIX
    gd = z_ref[0, :, OD_GD:OD_GD + LANES]
    x = _mm(gd, gup_ref[...]) + gb_ref[...]
    la = -_softplus(-x) * (1.0 / GLA_NORMALIZER)
    if masked:
        la = jnp.where(valid, la, 0.0)
    la_s[...] = la
    incl = _tri(tsub, False)
    for s in range(t // tsub):
        rs = slice(s * tsub, (s + 1) * tsub)
        cum_all = _mm_sel(incl, la_s[rs, :])
        hd = range(H_D)
        cums = [cum_all[:, h * HEAD_DK:(h + 1) * HEAD_DK] for h in hd]
        qes = [z_ref[0, rs, gq0 + h * HEAD_DK:gq0 + (h + 1) * HEAD_DK] * HEAD_DK ** -0.5 * jnp.exp(cums[h])
               for h in hd]
        ks = [z_ref[0, rs, gk0 + h * HEAD_DK:gk0 + (h + 1) * HEAD_DK] for h in hd]
        if masked:
            ks = [jnp.where(valid[rs], k, 0.0) for k in ks]
        vs = [z_ref[0, rs, gv0 + h * HEAD_DV:gv0 + (h + 1) * HEAD_DV] for h in hd]
        sts = [gla_ref[h] for h in hd]
        scores = [jnp.where(incl, _mm_nt(qes[h], ks[h] * jnp.exp(-cums[h])), 0.0) for h in hd]
        outs = [_mm(scores[h], vs[h]) + _mm_nt(qes[h], sts[h]) for h in hd]
        for h in hd:
            last = cums[h][tsub - 1:tsub]
            gla_ref[h] = sts[h] * jnp.exp(last) + _mm_tn(vs[h], ks[h] * jnp.exp(last - cums[h]))
        for h in hd:
            gr = z_ref[0, rs, gr0 + h * HEAD_DV:gr0 + (h + 1) * HEAD_DV]
            ys = slice(D_MIX + h * HEAD_DV, D_MIX + (h + 1) * HEAD_DV)
            y_s[rs, ys] = _rms(outs[h], 1e-6) * gnw_ref[:, h * HEAD_DV:(h + 1) * HEAD_DV] * _silu(gr)
    y_ref[0] = y_s[...].astype(BF16)

    @pl.when(ci == nchunks - 1)
    def _():
        ret_o[0] = ret_ref[...]
        for h in range(H_D):
            gla_o[0, h] = jnp.transpose(gla_ref[h])


def _ret_tables(t, lv, pos0, nchunks):
    half = HEAD_C // 2
    inv = ROPE_BASE ** (-jnp.arange(half, dtype=F32) / half)
    pos = pos0 + jnp.arange(t * nchunks, dtype=F32)
    ang = pos[:, None] * inv[None, :]
    cos, sin = jnp.cos(ang), jnp.sin(ang)
    cosf = jnp.concatenate([cos, cos], axis=-1)
    sinf = jnp.concatenate([-sin, sin], axis=-1)
    gam = 1.0 - np.exp2(-5.0 - np.arange(H_C, dtype=np.float64))
    ti = np.arange(t)
    d = ti[:, None] - ti[None, :]
    seg = np.where(d >= 0, gam[:, None, None] ** np.maximum(d, 0)[None], 0.0)
    pw = np.broadcast_to((gam[:, None] ** (ti + 1)[None])[:, :, None], (H_C, t, LANES))
    dte = gam[:, None] ** np.maximum(lv - 1 - ti, 0)[None]
    dte = np.broadcast_to(np.where(ti < lv, dte, 0.0)[:, :, None], (H_C, t, LANES))
    return cosf, sinf, jnp.asarray(seg, F32), jnp.asarray(pw, F32), jnp.asarray(dte, F32)


def _odd_mixer(z, states, li, p, *, t, lv, pos0):
    nseq, length, _ = z.shape
    nchunks = length // t
    has_init = states is not None
    assert lv == t or nchunks == 1
    tsub = min(GLA_SUB, t)
    cosf, sinf, seg, pw, dte = _ret_tables(t, lv, pos0, nchunks)
    full = lambda *shape: pl.BlockSpec(shape, lambda b, c: (0,) * len(shape))
    in_specs = [pl.BlockSpec((1, t, OD_PAD), lambda b, c: (b, c, 0))]
    args = [z]
    if has_init:
        in_specs += [pl.BlockSpec((None, 1, H_C, HEAD_C, HEAD_C), lambda b, c: (li, b, 0, 0, 0)),
                     pl.BlockSpec((None, 1, H_D, HEAD_DK, HEAD_DV), lambda b, c: (li, b, 0, 0, 0))]
        args += list(states)
    in_specs += [pl.BlockSpec((t, HEAD_C), lambda b, c: (c, 0)),
                 pl.BlockSpec((t, HEAD_C), lambda b, c: (c, 0)),
                 full(H_C, t, t), full(H_C, t, LANES), full(H_C, t, LANES),
                 full(1, D_MIX), full(LANES, H_D * HEAD_DK), full(1, H_D * HEAD_DK), full(1, D_MIX)]
    args += [cosf, sinf, seg, pw, dte, p['ret_norm_w'], p['gate_up'], p['gate_b'], p['gla_norm_w']]
    out_shape = (jax.ShapeDtypeStruct((nseq, length, D_MODEL), BF16),
                 jax.ShapeDtypeStruct((nseq, H_C, HEAD_C, HEAD_C), F32),
                 jax.ShapeDtypeStruct((nseq, H_D, HEAD_DK, HEAD_DV), F32))
    out_specs = (pl.BlockSpec((1, t, D_MODEL), lambda b, c: (b, c, 0)),
                 pl.BlockSpec((1, H_C, HEAD_C, HEAD_C), lambda b, c: (b, 0, 0, 0)),
                 pl.BlockSpec((1, H_D, HEAD_DK, HEAD_DV), lambda b, c: (b, 0, 0, 0)))
    scratch = [pltpu.VMEM((H_C, HEAD_C, HEAD_C), F32), pltpu.VMEM((H_D, HEAD_DV, HEAD_DK), F32),
               pltpu.VMEM((t, H_D * HEAD_DK), F32), pltpu.VMEM((t, D_MODEL), F32)]
    y, ret, gla = pl.pallas_call(
        functools.partial(_odd_kernel, t=t, tsub=tsub, nchunks=nchunks, lv=lv, has_init=has_init),
        grid=(nseq, nchunks),
        in_specs=in_specs, out_specs=out_specs, out_shape=out_shape, scratch_shapes=scratch,
        compiler_params=pltpu.CompilerParams(dimension_semantics=("arbitrary", "arbitrary"),
                                             vmem_limit_bytes=VMEM_LIMIT),
        name="odd_mixer",
    )(*args)
    return y, (ret, gla)


def _prep_proj(W):
    w = W['ev_w_in']
    n = w.shape[0]
    ev_in = jnp.concatenate([w[:, :, :RW_COLS], jnp.zeros((n, D_MODEL, RW_PAD - RW_COLS), F32),
                             w[:, :, RW_COLS:], jnp.zeros((n, D_MODEL, MB_PAD - MB_COLS), F32)], axis=2)
    w = W['od_w_in']
    n = w.shape[0]
    gd0 = OD_COLS - D_MIX - GLA_LORA
    od_in = jnp.concatenate([w[:, :, :gd0], w[:, :, gd0 + GLA_LORA:], w[:, :, gd0:gd0 + GLA_LORA],
                             jnp.zeros((n, D_MODEL, OD_PAD - OD_COLS), F32)], axis=2)
    return dict(ev_in=ev_in.astype(BF16), od_in=od_in.astype(BF16),
                ev_out=W['ev_w_out'].astype(BF16), od_out=W['od_w_out'].astype(BF16))


def _prep_even(i, W):
    mu =jnp.pad(W['rw_mu'][i], (0, RW_PAD - RW_COLS)).reshape(1, RW_PAD)
    rep = lambda v: jnp.repeat(v, HEAD_B)
    rows = [W['rw_w0'][i], W['rw_a0'][i], W['rw_k_k'][i], W['rw_k_a'][i], W['rw_r_k'][i].reshape(-1),
            W['rw_gn_w'][i], W['rw_gn_b'][i], rep(W['mb_d'][i]), W['mb_norm_w'][i]]
    vec = jnp.concatenate([jnp.stack(rows), jnp.zeros((EV_VEC_ROWS - len(rows), D_MIX), F32)], axis=0)
    wa = jnp.zeros((LANES, 2 * D_MIX), F32)
    wa = wa.at[:RW_LORA_W, :D_MIX].set(W['rw_w_up'][i]).at[RW_LORA_W:, D_MIX:].set(W['rw_a_up'][i])
    gup = jnp.pad(W['rw_g_up'][i], ((0, RW_PAD - 3 * D_MIX - LANES - RW_LORA_G), (0, 0)))
    pad_h = lambda v: jnp.pad(v, (0, LANES - H_B)).reshape(1, LANES)
    expand = np.zeros((LANES, D_MIX), np.float32)
    for h in range(H_B):
        expand[h, h * HEAD_B:(h + 1) * HEAD_B] = 1.0
    return dict(mu=mu, vec=vec, wa=wa.astype(BF16), gup=gup.astype(BF16), conv_w=W['mb_conv_w'][i],
                conv_b=W['mb_conv_b'][i].reshape(1, MB_CONV_CH), dt_bias=pad_h(W['mb_dt_bias'][i]),
                a_log=pad_h(W['mb_a_log'][i]), expand=jnp.asarray(expand, BF16))


def _prep_odd(i, W):
    gate_up = jnp.pad(W['gla_gate_up'][i], ((0, LANES - GLA_LORA), (0, 0)))
    return dict(ret_norm_w=W['ret_norm_w'][i].reshape(1, D_MIX), gate_up=gate_up.astype(BF16),
                gate_b=W['gla_gate_b'][i].reshape(1, H_D * HEAD_DK),
                gla_norm_w=W['gla_norm_w'][i].reshape(1, D_MIX))


def _trunk(h, mod, mc0, states, P, PW, FW, final_g, *, nseq, length, t, lv, pos0):
    outs = ([], [], [], [], [], [])
    wg, wu, wd = FW
    for l in range(DEPTH):
        i = l // 2
        p = P[l]
        mc = dict(mc0, l=l)
        even = l % 2 == 0
        h = _ffn(h, mod, mc, 0, wg, wu, wd, 0)
        z = _inproj(h, mod, mc, PW['ev_in' if even else 'od_in'], i).reshape(nseq, length, -1)
        if even:
            y, new = _even_mixer(z, None if states is None else states[:4], i, p, t=t, lv=lv)
            for lst, n in zip(outs[:4], new):
                lst.append(n)
        else:
            y, new = _odd_mixer(z, None if states is None else states[4:], i, p, t=t, lv=lv, pos0=pos0)
            for lst, n in zip(outs[4:], new):
                lst.append(n)
        h = _outproj(y.reshape(nseq * length, D_MODEL), PW['ev_out' if even else 'od_out'], i, h, mod, mc)
        h = _ffn(h, mod, mc, 6, wg, wu, wd, 1, final_g=final_g if l == DEPTH - 1 else None)
    return h, tuple(jnp.stack(lst) for lst in outs)


def kernel(x_prompt, x_sample, state_rwkv_shift, state_rwkv_wkv, state_mamba_conv, state_mamba_ssm,
           state_ret, state_gla, c_prompt, c_sample, ada_w, ada_b, ffn_wg, ffn_wu, ffn_wd, ev_w_in,
           ev_w_out, rw_mu, rw_w0, rw_w_up, rw_a0, rw_a_up, rw_g_up, rw_k_k, rw_k_a, rw_r_k, rw_gn_w,
           rw_gn_b, mb_conv_w, mb_conv_b, mb_dt_bias, mb_a_log, mb_d, mb_norm_w, od_w_in, od_w_out,
           ret_norm_w, gla_gate_up, gla_gate_b, gla_norm_w, final_g):
    W = dict(ev_w_in=ev_w_in, ev_w_out=ev_w_out, rw_mu=rw_mu, rw_w0=rw_w0, rw_w_up=rw_w_up,
             rw_a0=rw_a0, rw_a_up=rw_a_up, rw_g_up=rw_g_up, rw_k_k=rw_k_k, rw_k_a=rw_k_a,
             rw_r_k=rw_r_k, rw_gn_w=rw_gn_w, rw_gn_b=rw_gn_b, mb_conv_w=mb_conv_w,
             mb_conv_b=mb_conv_b, mb_dt_bias=mb_dt_bias, mb_a_log=mb_a_log, mb_d=mb_d,
             mb_norm_w=mb_norm_w, od_w_in=od_w_in, od_w_out=od_w_out, ret_norm_w=ret_norm_w,
             gla_gate_up=gla_gate_up, gla_gate_b=gla_gate_b, gla_norm_w=gla_norm_w)
    nb, seq, _ = x_prompt.shape
    db, dseq, _ = x_sample.shape
    P = [(_prep_even if l % 2 == 0 else _prep_odd)(l // 2, W) for l in range(DEPTH)]
    PW = _prep_proj(W)
    FW = (ffn_wg.astype(BF16), ffn_wu.astype(BF16), ffn_wd.astype(BF16))

    rows = nb + db
    rows_pad = -(-rows // SUBLANES) * SUBLANES
    c_all = jnp.concatenate([c_prompt, c_sample, jnp.zeros((rows_pad - rows, D_MODEL), F32)], axis=0)
    mod_all = _ada(c_all, ada_w, ada_b)
    mod_p = mod_all.reshape(DEPTH * rows_pad * N_MOD, 1, D_MODEL)
    mod_s = jnp.repeat(mod_all[:, nb:rows], SAMPLE_CHUNK, axis=1)

    y_p, st_p = _trunk(x_prompt.reshape(nb * seq, D_MODEL), mod_p,
                       dict(seq_mod=True, seqs=rows_pad, rows_per_seq=seq), None, P, PW, FW, final_g,
                       nseq=nb, length=seq, t=PROMPT_CHUNK, lv=PROMPT_CHUNK, pos0=0.0)

    xs = jnp.pad(x_sample, ((0, 0), (0, SAMPLE_CHUNK - dseq), (0, 0)))
    states = (state_rwkv_shift, state_rwkv_wkv, state_mamba_conv, state_mamba_ssm, state_ret, state_gla)
    y_s, st_s = _trunk(xs.reshape(db * SAMPLE_CHUNK, D_MODEL), mod_s, dict(seq_mod=False), states,
                       P, PW, FW, final_g, nseq=db, length=SAMPLE_CHUNK, t=SAMPLE_CHUNK, lv=dseq,
                       pos0=float(PAST_LEN))
    y_s = y_s.reshape(db, SAMPLE_CHUNK, D_MODEL)[:, :dseq]
    return (y_p.reshape(nb, seq, D_MODEL), y_s) + st_p + st_s
```

```python
import functools
import math

import numpy as np
import jax
import jax.numpy as jnp
from jax import lax
from jax.experimental import pallas as pl
from jax.experimental.pallas import tpu as pltpu

F32 = jnp.float32
BF16 = jnp.bfloat16

D_MODEL = 2048
DEPTH = 4
PAST_LEN = 16384
N_MOD = 9
D_FF = 5632

D_MIX = 1024
H_A, HEAD_A = 16, 64
RW_LORA_W, RW_LORA_A, RW_LORA_G = 64, 64, 160
RW_COLS = 3 * D_MIX + RW_LORA_W + RW_LORA_A + RW_LORA_G
RW_GN_EPS = 64e-5
H_B, HEAD_B, D_STATE, N_GROUPS, CONV_W = 16, 64, 128, 2, 4
MB_CONV_CH = D_MIX + 2 * N_GROUPS * D_STATE
MB_COLS = D_MIX + MB_CONV_CH + H_B
H_C, HEAD_C = 8, 128
ROPE_BASE = 10000.0
H_D, HEAD_DK, HEAD_DV, GLA_LORA = 4, 128, 256, 16
GLA_NORMALIZER = 16.0
OD_COLS = 4 * D_MIX + 2 * H_D * HEAD_DK + 2 * D_MIX + GLA_LORA

LANES = 128
SUBLANES = 8
VMEM_LIMIT = 56 * 1024 * 1024

RW_PAD = 3456
ZM = RW_PAD
MB_PAD = 2688
EV_PAD = RW_PAD + MB_PAD
OD_PAD = 7296
OD_GD = 7168

PROMPT_CHUNK = 64
GLA_SUB = 32
SAMPLE_CHUNK = 8


def _mm(a, b):
    return jnp.dot(a.astype(BF16), b.astype(BF16), preferred_element_type=F32)


def _mm_nt(a, b):
    return lax.dot_general(a.astype(BF16), b.astype(BF16), (((1,), (1,)), ((), ())),
                           preferred_element_type=F32)


def _mm_tn(a, b):
    return lax.dot_general(a.astype(BF16), b.astype(BF16), (((0,), (0,)), ((), ())),
                           preferred_element_type=F32)


def _split3(x):
    hi = x.astype(BF16)
    r = x - hi.astype(F32)
    mid = r.astype(BF16)
    lo = (r - mid.astype(F32)).astype(BF16)
    return hi, mid, lo


def _mm_sel(sel, x):
    s = sel.astype(BF16)
    hi, mid, lo = _split3(x)
    return (jnp.dot(s, hi, preferred_element_type=F32) + jnp.dot(s, mid, preferred_element_type=F32)
            + jnp.dot(s, lo, preferred_element_type=F32))


def _mm_tn_sel(x, sel):
    s = sel.astype(BF16)
    dn = (((0,), (0,)), ((), ()))
    hi, mid, lo = _split3(x)
    return (lax.dot_general(hi, s, dn, preferred_element_type=F32)
            + lax.dot_general(mid, s, dn, preferred_element_type=F32)
            + lax.dot_general(lo, s, dn, preferred_element_type=F32))


def _mm_xsel(x, sel):
    s = sel.astype(BF16)
    hi, mid, lo = _split3(x)
    return (jnp.dot(hi, s, preferred_element_type=F32) + jnp.dot(mid, s, preferred_element_type=F32)
            + jnp.dot(lo, s, preferred_element_type=F32))


def _sigmoid(x):
    return 1.0 / (1.0 + jnp.exp(-x))


def _silu(x):
    return x * _sigmoid(x)


def _softplus(x):
    return jnp.maximum(x, 0.0) + jnp.log(1.0 + jnp.exp(-jnp.abs(x)))


def _rms(x, eps):
    return x * lax.rsqrt(jnp.mean(x * x, axis=-1, keepdims=True) + eps)


def _tri(n, strict):
    r = lax.broadcasted_iota(jnp.int32, (n, n), 0)
    c = lax.broadcasted_iota(jnp.int32, (n, n), 1)
    return (r > c) if strict else (r >= c)


ROW_TILE = 512


def _mod_specs(mc, tm, idxs):
    l, rps, row0 = mc['l'], mc['rps'], mc['row0']
    specs = []
    for idx in idxs:
        if rps >= tm:
            specs.append(pl.BlockSpec(
                (None, None, SUBLANES, D_MODEL),
                lambda i, j, idx=idx: (l, idx, (row0 + i * tm // rps) // SUBLANES, 0)))
        else:
            assert row0 == 0
            specs.append(pl.BlockSpec((None, None, tm // rps, D_MODEL), lambda i, j, idx=idx: (l, idx, i, 0)))
    return specs


def _mod_val(ref, mc, tm):
    rps = mc['rps']
    if rps >= tm:
        r = (mc['row0'] + pl.program_id(0) * tm // rps) % SUBLANES
        return ref[pl.ds(r, 1), :]
    nseq = tm // rps
    row = lax.broadcasted_iota(jnp.int32, (tm, nseq), 0)
    first = lax.broadcasted_iota(jnp.int32, (tm, nseq), 1) * rps
    return _mm_sel((row >= first) & (row < first + rps), ref[...])


def _ada_kernel(c_ref, w_ref, b_ref, o_ref):
    x = _silu(c_ref[...]).astype(BF16)
    o_ref[...] = jnp.dot(x, w_ref[...].astype(BF16), preferred_element_type=F32) + b_ref[...]


def _ada(c, ada_w, ada_b):
    rows = c.shape[0]
    n = ada_w.shape[-1]
    tn = 1024
    per_mod = D_MODEL // tn
    return pl.pallas_call(
        _ada_kernel,
        grid=(DEPTH, n // tn),
        in_specs=[pl.BlockSpec((rows, D_MODEL), lambda l, j: (0, 0)),
                  pl.BlockSpec((None, D_MODEL, tn), lambda l, j: (l, 0, j)),
                  pl.BlockSpec((None, 1, tn), lambda l, j: (l, 0, j))],
        out_specs=pl.BlockSpec((None, None, rows, tn), lambda l, j: (l, j // per_mod, 0, j % per_mod)),
        out_shape=jax.ShapeDtypeStruct((DEPTH, N_MOD, rows, D_MODEL), F32),
        compiler_params=pltpu.CompilerParams(dimension_semantics=("arbitrary", "arbitrary"),
                                             vmem_limit_bytes=VMEM_LIMIT),
        name="ada",
    )(c, ada_w, ada_b.reshape(DEPTH, 1, n))


def _ffn_kernel(*refs, mc, tm, nj, final):
    if final:
        h_ref, sh_ref, sc_ref, g_ref, wg_ref, wu_ref, wd_ref, fin_ref, o_ref, xn_ref, acc_ref = refs
    else:
        h_ref, sh_ref, sc_ref, g_ref, wg_ref, wu_ref, wd_ref, o_ref, xn_ref, acc_ref = refs
    j = pl.program_id(1)

    @pl.when(j == 0)
    def _():
        xn = _rms(h_ref[...], 1e-6) * (1.0 + _mod_val(sc_ref, mc, tm)) + _mod_val(sh_ref, mc, tm)
        xn_ref[...] = xn.astype(BF16)
        acc_ref[...] = jnp.zeros_like(acc_ref)

    xn = xn_ref[...]
    gate = jnp.dot(xn, wg_ref[...], preferred_element_type=F32)
    up = jnp.dot(xn, wu_ref[...], preferred_element_type=F32)
    act = (_silu(gate) * up).astype(BF16)
    acc_ref[...] += jnp.dot(act, wd_ref[...], preferred_element_type=F32)

    @pl.when(j == nj - 1)
    def _():
        out = h_ref[...] + 0.5 * (1.0 + _mod_val(g_ref, mc, tm)) * acc_ref[...]
        if final:
            out = _rms(out, 1e-6) * fin_ref[...]
        o_ref[...] = out


def _ffn(h, mod, mc, mod_base, wg, wu, wd, k, final_g=None):
    m = h.shape[0]
    l = mc['l']
    tm = min(ROW_TILE, m)
    tf = 512
    nj = D_FF // tf
    final = final_g is not None
    in_specs = [pl.BlockSpec((tm, D_MODEL), lambda i, j: (i, 0))]
    in_specs += _mod_specs(mc, tm, (mod_base, mod_base + 1, mod_base + 2))
    in_specs += [pl.BlockSpec((None, None, D_MODEL, tf), lambda i, j: (l, k, 0, j)),
                 pl.BlockSpec((None, None, D_MODEL, tf), lambda i, j: (l, k, 0, j)),
                 pl.BlockSpec((None, None, tf, D_MODEL), lambda i, j: (l, k, j, 0))]
    args = [h, mod, mod, mod, wg, wu, wd]
    if final:
        in_specs.append(pl.BlockSpec((1, D_MODEL), lambda i, j: (0, 0)))
        args.append(final_g.reshape(1, D_MODEL))
    return pl.pallas_call(
        functools.partial(_ffn_kernel, mc=mc, tm=tm, nj=nj, final=final),
        grid=(m // tm, nj),
        in_specs=in_specs,
        out_specs=pl.BlockSpec((tm, D_MODEL), lambda i, j: (i, 0)),
        out_shape=jax.ShapeDtypeStruct((m, D_MODEL), F32),
        scratch_shapes=[pltpu.VMEM((tm, D_MODEL), BF16), pltpu.VMEM((tm, D_MODEL), F32)],
        compiler_params=pltpu.CompilerParams(dimension_semantics=("arbitrary", "arbitrary"),
                                             vmem_limit_bytes=VMEM_LIMIT),
        name="ffn",
    )(*args)


def _inproj_kernel(h_ref, sh_ref, sc_ref, w_ref, o_ref, xn_ref, *, mc, tm):
    @pl.when(pl.program_id(1) == 0)
    def _():
        xn = _rms(h_ref[...], 1e-6) * (1.0 + _mod_val(sc_ref, mc, tm)) + _mod_val(sh_ref, mc, tm)
        xn_ref[...] = xn.astype(BF16)

    o_ref[...] = jnp.dot(xn_ref[...], w_ref[...], preferred_element_type=F32)


def _inproj(h, mod, mc, w, li):
    m = h.shape[0]
    n = w.shape[2]
    tm = min(ROW_TILE, m)
    tn = n // 3
    in_specs = [pl.BlockSpec((tm, D_MODEL), lambda i, j: (i, 0))]
    in_specs += _mod_specs(mc, tm, (3, 4))
    in_specs += [pl.BlockSpec((None, D_MODEL, tn), lambda i, j: (li, 0, j))]
    return pl.pallas_call(
        functools.partial(_inproj_kernel, mc=mc, tm=tm),
        grid=(m // tm, n // tn),
        in_specs=in_specs,
        out_specs=pl.BlockSpec((tm, tn), lambda i, j: (i, j)),
        out_shape=jax.ShapeDtypeStruct((m, n), F32),
        scratch_shapes=[pltpu.VMEM((tm, D_MODEL), BF16)],
        compiler_params=pltpu.CompilerParams(dimension_semantics=("arbitrary", "arbitrary"),
                                             vmem_limit_bytes=VMEM_LIMIT),
        name="inproj",
    )(h, mod, mod, w)


def _outproj_kernel(y_ref, w_ref, h_ref, g_ref, o_ref, *, mc, tm):
    mix = jnp.dot(y_ref[...], w_ref[...], preferred_element_type=F32)
    o_ref[...] = h_ref[...] + (1.0 + _mod_val(g_ref, mc, tm)) * mix


def _outproj(y, w, li, h, mod, mc):
    m = h.shape[0]
    tm = min(ROW_TILE, m)
    in_specs = [pl.BlockSpec((tm, D_MODEL), lambda i, j: (i, 0)),
                pl.BlockSpec((None, D_MODEL, D_MODEL), lambda i, j: (li, 0, 0)),
                pl.BlockSpec((tm, D_MODEL), lambda i, j: (i, 0))]
    in_specs += _mod_specs(mc, tm, (5,))
    return pl.pallas_call(
        functools.partial(_outproj_kernel, mc=mc, tm=tm),
        grid=(m // tm, 1),
        in_specs=in_specs,
        out_specs=pl.BlockSpec((tm, D_MODEL), lambda i, j: (i, 0)),
        out_shape=jax.ShapeDtypeStruct((m, D_MODEL), F32),
        compiler_params=pltpu.CompilerParams(dimension_semantics=("arbitrary", "arbitrary"),
                                             vmem_limit_bytes=VMEM_LIMIT),
        name="outproj",
    )(y, w, h, mod)


def _state_out(stacked, li, nseq, tail):
    zeros = (0,) * len(tail)
    if stacked is None:
        return (jax.ShapeDtypeStruct((nseq,) + tail, F32),
                pl.BlockSpec((1,) + tail, lambda b, c: (b,) + zeros))
    return (jax.ShapeDtypeStruct(stacked.shape, F32),
            pl.BlockSpec((None, 1) + tail, lambda b, c: (li, b) + zeros))


EV_W0, EV_A0, EV_KK, EV_KA, EV_RK, EV_GNW, EV_GNB, EV_D, EV_NW = range(9)
EV_VEC_ROWS = 16


def _even_kernel(*refs, t, nchunks, lv, has_init, hg):
    if has_init:
        (z_ref, shift0_ref, wkv0_ref, conv0_ref, ssm0_ref, *rest) = refs
    else:
        z_ref, *rest = refs
    (mu_ref, vec_ref, wa_ref, gup_ref, cw_ref, cb_ref, dtb_ref, alog_ref, exp_ref,
     y_ref, shift_o, wkv_o, conv_o, ssm_o,
     carry_ref, wkv_ref, ext_ref, ssm_ref,
     r_s, kp_s, v_s, kk_s, a_s, c_s, ld_s, g_s, y_s) = rest
    ci = pl.program_id(1)
    masked = lv < t
    nlog = int(math.log2(t))

    @pl.when(ci == 0)
    def _():
        if has_init:
            carry_ref[:, 0:RW_COLS] = shift0_ref[0]
            carry_ref[:, RW_COLS:RW_PAD] = jnp.zeros((1, RW_PAD - RW_COLS), F32)
            wkv_ref[...] = wkv0_ref[0]
            ext_ref[0:SUBLANES - CONV_W + 1, :] = jnp.zeros((SUBLANES - CONV_W + 1, MB_CONV_CH), F32)
            ext_ref[SUBLANES - CONV_W + 1:SUBLANES, :] = conv0_ref[0]
            for h in range(H_B):
                ssm_ref[:, h * HEAD_B:(h + 1) * HEAD_B] = ssm0_ref[0, h]
        else:
            carry_ref[...] = jnp.zeros_like(carry_ref)
            wkv_ref[...] = jnp.zeros_like(wkv_ref)
            ext_ref[0:SUBLANES, :] = jnp.zeros((SUBLANES, MB_CONV_CH), F32)
            ssm_ref[...] = jnp.zeros_like(ssm_ref)

    row = lax.broadcasted_iota(jnp.int32, (t, 1), 0)
    valid = row < lv
    strict = _tri(t, True)
    incl = _tri(t, False)

    def vec(i, lo=0, hi=D_MIX):
        return vec_ref[i:i + 1, lo:hi]

    def shifted(a, b):
        cur = z_ref[0, :, a:b]
        prev = jnp.where(row == 0, carry_ref[:, a:b], pltpu.roll(cur, 1, axis=0))
        return cur + mu_ref[:, a:b] * (prev - cur)

    r_s[...] = shifted(0, D_MIX)
    k = shifted(D_MIX, 2 * D_MIX)
    v_s[...] = shifted(2 * D_MIX, 3 * D_MIX)
    lo = shifted(3 * D_MIX, 3 * D_MIX + LANES)
    lane = lax.broadcasted_iota(jnp.int32, (t, LANES), 1)
    wa = _mm(jnp.where(lane < RW_LORA_W, jnp.tanh(lo), lo), wa_ref[...])
    w_log = -_softplus(-(vec(EV_W0) + wa[:, 0:D_MIX])) - 0.5
    ld = -jnp.exp(w_log)
    a = _sigmoid(vec(EV_A0) + wa[:, D_MIX:2 * D_MIX])
    g_s[...] = _mm(_sigmoid(shifted(3 * D_MIX + LANES, RW_PAD)), gup_ref[...])
    kp = k * (1.0 + (a - 1.0) * vec(EV_KA))
    if masked:
        ld = jnp.where(valid, ld, 0.0)
        kp = jnp.where(valid, kp, 0.0)
        a = jnp.where(valid, a, 0.0)
    kk_s[...] = k * vec(EV_KK)
    kp_s[...] = kp
    a_s[...] = a
    ld_s[...] = ld
    c_s[...] = _mm_sel(incl, ld)
    last_row = lv - 1
    carry_ref[...] = z_ref[0, last_row:last_row + 1, 0:RW_PAD]

    for h0 in range(0, H_A, hg):
        heads = range(h0, h0 + hg)
        sls = [slice(h * HEAD_A, (h + 1) * HEAD_A) for h in heads]
        rs = [r_s[:, sl] for sl in sls]
        kps = [kp_s[:, sl] for sl in sls]
        vs = [v_s[:, sl] for sl in sls]
        cs = [c_s[:, sl] for sl in sls]
        kks, bs = [], []
        for sl in sls:
            kkr = kk_s[:, sl]
            nrm = jnp.sqrt(jnp.sum(kkr * kkr, axis=-1, keepdims=True))
            kks.append(kkr / jnp.maximum(nrm, 1e-12))
            bs.append(kks[-1] * a_s[:, sl])
        ecs = [jnp.exp(c) for c in cs]
        ecis = [jnp.exp(-c) for c in cs]
        lhss = [jnp.concatenate([kk * jnp.exp(c - ld_s[:, sl]), r * ec], axis=0)
                for kk, c, sl, r, ec in zip(kks, cs, sls, rs, ecs)]
        s0s = [wkv_ref[h] for h in heads]
        a1s = [_mm_nt(lhs, kp * eci) for lhs, kp, eci in zip(lhss, kps, ecis)]
        a2s = [_mm_nt(lhs, b * eci) for lhs, b, eci in zip(lhss, bs, ecis)]
        pms = [_mm_nt(lhs, s0) for lhs, s0 in zip(lhss, s0s)]
        us = [pm[0:t] + _mm(jnp.where(strict, a1[0:t], 0.0), v) for pm, a1, v in zip(pms, a1s, vs)]
        ps = [jnp.where(strict, -a2[0:t], 0.0) for a2 in a2s]
        us = [u + _mm(p, u) for u, p in zip(us, ps)]
        for _ in range(nlog - 1):
            ps = [_mm(p, p) for p in ps]
            us = [u + _mm(p, u) for u, p in zip(us, ps)]
        ys = [pm[t:2 * t] + _mm(jnp.where(incl, a1[t:2 * t], 0.0), v) - _mm(jnp.where(incl, a2[t:2 * t], 0.0), u)
              for pm, a1, a2, v, u in zip(pms, a1s, a2s, vs, us)]
        for i, h in enumerate(heads):
            c, kp, b, v, u = cs[i], kps[i], bs[i], vs[i], us[i]
            dec_end = jnp.exp(c[t - 1:t] - c)
            vu = jnp.concatenate([v, u], axis=0)
            kb = jnp.concatenate([kp * dec_end, -(b * dec_end)], axis=0)
            wkv_ref[h] = s0s[i] * ecs[i][t - 1:t] + _mm_tn(vu, kb)
        for i, sl in enumerate(sls):
            y, r, kp, v = ys[i], rs[i], kps[i], vs[i]
            mean = jnp.mean(y, axis=-1, keepdims=True)
            yc = y - mean
            var = jnp.mean(yc * yc, axis=-1, keepdims=True)
            yn = (yc * lax.rsqrt(var + RW_GN_EPS) * vec(EV_GNW, sl.start, sl.stop)
                  + vec(EV_GNB, sl.start, sl.stop))
            bonus = jnp.sum(r * kp * vec(EV_RK, sl.start, sl.stop), axis=-1, keepdims=True) * v
            y_s[:, sl] = (yn + bonus) * g_s[:, sl]

    zg = z_ref[0, :, ZM:ZM + D_MIX]
    ext_ref[SUBLANES:SUBLANES + t, :] = z_ref[0, :, ZM + D_MIX:ZM + D_MIX + MB_CONV_CH]
    conv = cb_ref[...]
    for w in range(CONV_W):
        off = SUBLANES - (CONV_W - 1) + w
        conv = conv + cw_ref[w:w + 1, :] * ext_ref[off:off + t, :]
    new_conv = ext_ref[SUBLANES + lv - (CONV_W - 1):SUBLANES + lv, :]
    ext_ref[SUBLANES - (CONV_W - 1):SUBLANES, :] = new_conv
    xbc = _silu(conv)
    xs = xbc[:, 0:D_MIX]
    dt = _softplus(z_ref[0, :, ZM + D_MIX + MB_CONV_CH:ZM + MB_PAD] + dtb_ref[...])
    logd = dt * (-jnp.exp(alog_ref[...]))
    if masked:
        logd = jnp.where(valid, logd, 0.0)
    c = _mm_sel(incl, logd)
    upper = lax.broadcasted_iota(jnp.int32, (t, t), 0) <= lax.broadcasted_iota(jnp.int32, (t, t), 1)
    c_t = _mm_tn_sel(logd, upper)
    wide = _mm_xsel(jnp.concatenate([dt, c, c[t - 1:t] - c], axis=0), exp_ref[...])
    xdt = xs * wide[0:t]
    if masked:
        xdt = jnp.where(valid, xdt, 0.0)
    ec_w = jnp.exp(wide[t:2 * t])
    dec_end_w = jnp.exp(wide[2 * t:3 * t])
    ecl_w = ec_w[t - 1:t]
    gw = (H_B // N_GROUPS) * HEAD_B
    hpg = H_B // N_GROUPS
    bgs = [xbc[:, D_MIX + g * D_STATE:D_MIX + (g + 1) * D_STATE] for g in range(N_GROUPS)]
    cgs = [xbc[:, D_MIX + (N_GROUPS + g) * D_STATE:D_MIX + (N_GROUPS + g + 1) * D_STATE]
           for g in range(N_GROUPS)]
    scores = [_mm_nt(cg, bg) for cg, bg in zip(cgs, bgs)]
    s_prevs = [ssm_ref[:, g * gw:(g + 1) * gw] for g in range(N_GROUPS)]
    inters = [_mm(cg, sp) for cg, sp in zip(cgs, s_prevs)]
    segs = [jnp.where(incl, jnp.exp(jnp.where(incl, c[:, h:h + 1] - c_t[h:h + 1, :], 0.0)), 0.0)
            for h in range(H_B)]
    intra = [_mm(scores[h // hpg] * segs[h], xdt[:, h * HEAD_B:(h + 1) * HEAD_B]) for h in range(H_B)]
    for g in range(N_GROUPS):
        gs = slice(g * gw, (g + 1) * gw)
        ssm_ref[:, gs] = s_prevs[g] * ecl_w[:, gs] + _mm_tn(bgs[g], xdt[:, gs] * dec_end_w[:, gs])
    o = jnp.concatenate(intra, axis=1) + jnp.concatenate(inters, axis=1) * ec_w
    yb = (o + vec(EV_D) * xs) * _silu(zg)
    y_s[:, D_MIX:2 * D_MIX] = _rms(yb, 1e-5) * vec(EV_NW)
    y_ref[0] = y_s[...].astype(BF16)

    @pl.when(ci == nchunks - 1)
    def _():
        shift_o[0] = carry_ref[...]
        wkv_o[0] = wkv_ref[...]
        conv_o[0] = new_conv
        for h in range(H_B):
            ssm_o[0, h] = ssm_ref[:, h * HEAD_B:(h + 1) * HEAD_B]


def _even_mixer(z, states, li, p, *, t, lv):
    nseq, length, _ = z.shape
    nchunks = length // t
    has_init = states is not None
    assert lv == t or nchunks == 1
    full = lambda *shape: pl.BlockSpec(shape, lambda b, c: (0,) * len(shape))
    in_specs = [pl.BlockSpec((1, t, EV_PAD), lambda b, c: (b, c, 0))]
    args = [z]
    if has_init:
        shift0, wkv0, conv0, ssm0 = states
        in_specs += [pl.BlockSpec((None, 1, 1, RW_COLS), lambda b, c: (li, b, 0, 0)),
                     pl.BlockSpec((None, 1, H_A, HEAD_A, HEAD_A), lambda b, c: (li, b, 0, 0, 0)),
                     pl.BlockSpec((None, 1, CONV_W - 1, MB_CONV_CH), lambda b, c: (li, b, 0, 0)),
                     pl.BlockSpec((None, 1, H_B, D_STATE, HEAD_B), lambda b, c: (li, b, 0, 0, 0))]
        args += [shift0.reshape(-1, nseq, 1, RW_COLS), wkv0, conv0, ssm0]
    in_specs += [full(1, RW_PAD), full(EV_VEC_ROWS, D_MIX), full(LANES, 2 * D_MIX),
                 full(RW_PAD - 3 * D_MIX - LANES, D_MIX), full(CONV_W, MB_CONV_CH),
                 full(1, MB_CONV_CH), full(1, LANES), full(1, LANES), full(LANES, D_MIX)]
    args += [p['mu'], p['vec'], p['wa'], p['gup'], p['conv_w'], p['conv_b'], p['dt_bias'],
             p['a_log'], p['expand']]
    wkv_shape, wkv_spec = _state_out(states and states[1], li, nseq, (H_A, HEAD_A, HEAD_A))
    ssm_shape, ssm_spec = _state_out(states and states[3], li, nseq, (H_B, D_STATE, HEAD_B))
    out_shape = (jax.ShapeDtypeStruct((nseq, length, D_MODEL), BF16),
                 jax.ShapeDtypeStruct((nseq, 1, RW_PAD), F32), wkv_shape,
                 jax.ShapeDtypeStruct((nseq, CONV_W - 1, MB_CONV_CH), F32), ssm_shape)
    out_specs = (pl.BlockSpec((1, t, D_MODEL), lambda b, c: (b, c, 0)),
                 pl.BlockSpec((1, 1, RW_PAD), lambda b, c: (b, 0, 0)), wkv_spec,
                 pl.BlockSpec((1, CONV_W - 1, MB_CONV_CH), lambda b, c: (b, 0, 0)), ssm_spec)
    wide = lambda: pltpu.VMEM((t, D_MIX), F32)
    scratch = [pltpu.VMEM((1, RW_PAD), F32), pltpu.VMEM((H_A, HEAD_A, HEAD_A), F32),
               pltpu.VMEM((t + SUBLANES, MB_CONV_CH), F32), pltpu.VMEM((D_STATE, H_B * HEAD_B), F32)]
    scratch += [wide() for _ in range(8)] + [pltpu.VMEM((t, D_MODEL), F32)]
    y, shift, wkv, conv, ssm = pl.pallas_call(
        functools.partial(_even_kernel, t=t, nchunks=nchunks, lv=lv, has_init=has_init,
                          hg=H_A),
        grid=(nseq, nchunks),
        in_specs=in_specs, out_specs=out_specs, out_shape=out_shape, scratch_shapes=scratch,
        input_output_aliases={2: 2, 4: 4} if has_init else {},
        compiler_params=pltpu.CompilerParams(dimension_semantics=("arbitrary", "arbitrary"),
                                             vmem_limit_bytes=VMEM_LIMIT),
        name="even_mixer",
    )(*args)
    return y, (shift[:, 0, :RW_COLS], wkv, conv, ssm)


def _odd_kernel(*refs, t, tsub, nchunks, lv, has_init):
    if has_init:
        z_ref, ret0_ref, gla0_ref, *rest = refs
    else:
        z_ref, *rest = refs
    (cos_ref, sin_ref, seg_ref, pw_ref, dte_ref, rnw_ref, gup_ref, gb_ref, gnw_ref,
     y_ref, ret_o, gla_o, ret_ref, gla_ref, la_s, y_s) = rest
    ci = pl.program_id(1)
    masked = lv < t

    @pl.when(ci == 0)
    def _():
        if has_init:
            ret_ref[...] = ret0_ref[0]
            for h in range(H_D):
                gla_ref[h] = jnp.transpose(gla0_ref[0, h])
        else:
            ret_ref[...] = jnp.zeros_like(ret_ref)
            gla_ref[...] = jnp.zeros_like(gla_ref)

    row = lax.broadcasted_iota(jnp.int32, (t, 1), 0)
    valid = row < lv

    cosf = cos_ref[...]
    sinf = sin_ref[...]

    def rotary(x):
        return x * cosf + pltpu.roll(x, HEAD_C // 2, axis=1) * sinf

    hc = range(H_C)
    qs = [rotary(z_ref[0, :, h * HEAD_C:(h + 1) * HEAD_C]) * HEAD_C ** -0.5 for h in hc]
    ks = [rotary(z_ref[0, :, D_MIX + h * HEAD_C:D_MIX + (h + 1) * HEAD_C]) for h in hc]
    if masked:
        ks = [jnp.where(valid, k, 0.0) for k in ks]
    vs = [z_ref[0, :, 2 * D_MIX + h * HEAD_C:2 * D_MIX + (h + 1) * HEAD_C] for h in hc]
    s0s = [ret_ref[h] for h in hc]
    sc = [_mm_nt(q, k) * seg_ref[h] for h, q, k in zip(hc, qs, ks)]
    inter = [_mm(q, s0) * pw_ref[h] for h, q, s0 in zip(hc, qs, s0s)]
    outs = [_mm(s, v) + i for s, v, i in zip(sc, vs, inter)]
    for h in hc:
        gamma = 1.0 - 2.0 ** (-5.0 - h)
        ret_ref[h] = s0s[h] * gamma ** lv + _mm_tn(ks[h] * dte_ref[h], vs[h])
    for h in hc:
        sl = slice(h * HEAD_C, (h + 1) * HEAD_C)
        rg = z_ref[0, :, 3 * D_MIX + h * HEAD_C:3 * D_MIX + (h + 1) * HEAD_C]
        y_s[:, sl] = _rms(outs[h], 1e-6) * rnw_ref[:, sl] * _silu(rg)

    gq0 = 4 * D_MIX
    gk0 = gq0 + H_D * HEAD_DK
    gv0 = gk0 + H_D * HEAD_DK
    gr0 = gv0 + D_MIX
    gd = z_ref[0, :, OD_GD:OD_GD + LANES]
    x = _mm(gd, gup_ref[...]) + gb_ref[...]
    la = -_softplus(-x) * (1.0 / GLA_NORMALIZER)
    if masked:
        la = jnp.where(valid, la, 0.0)
    la_s[...] = la
    incl = _tri(tsub, False)
    for s in range(t // tsub):
        rs = slice(s * tsub, (s + 1) * tsub)
        cum_all = _mm_sel(incl, la_s[rs, :])
        hd = range(H_D)
        cums = [cum_all[:, h * HEAD_DK:(h + 1) * HEAD_DK] for h in hd]
        qes = [z_ref[0, rs, gq0 + h * HEAD_DK:gq0 + (h + 1) * HEAD_DK] * HEAD_DK ** -0.5 * jnp.exp(cums[h])
               for h in hd]
        ks = [z_ref[0, rs, gk0 + h * HEAD_DK:gk0 + (h + 1) * HEAD_DK] for h in hd]
        if masked:
            ks = [jnp.where(valid[rs], k, 0.0) for k in ks]
        vs = [z_ref[0, rs, gv0 + h * HEAD_DV:gv0 + (h + 1) * HEAD_DV] for h in hd]
        sts = [gla_ref[h] for h in hd]
        scores = [jnp.where(incl, _mm_nt(qes[h], ks[h] * jnp.exp(-cums[h])), 0.0) for h in hd]
        outs = [_mm(scores[h], vs[h]) + _mm_nt(qes[h], sts[h]) for h in hd]
        for h in hd:
            last = cums[h][tsub - 1:tsub]
            gla_ref[h] = sts[h] * jnp.exp(last) + _mm_tn(vs[h], ks[h] * jnp.exp(last - cums[h]))
        for h in hd:
            gr = z_ref[0, rs, gr0 + h * HEAD_DV:gr0 + (h + 1) * HEAD_DV]
            ys = slice(D_MIX + h * HEAD_DV, D_MIX + (h + 1) * HEAD_DV)
            y_s[rs, ys] = _rms(outs[h], 1e-6) * gnw_ref[:, h * HEAD_DV:(h + 1) * HEAD_DV] * _silu(gr)
    y_ref[0] = y_s[...].astype(BF16)

    @pl.when(ci == nchunks - 1)
    def _():
        ret_o[0] = ret_ref[...]
        for h in range(H_D):
            gla_o[0, h] = jnp.transpose(gla_ref[h])


def _ret_tables(t, lv, pos0, nchunks):
    half = HEAD_C // 2
    inv = ROPE_BASE ** (-jnp.arange(half, dtype=F32) / half)
    pos = pos0 + jnp.arange(t * nchunks, dtype=F32)
    ang = pos[:, None] * inv[None, :]
    cos, sin = jnp.cos(ang), jnp.sin(ang)
    cosf = jnp.concatenate([cos, cos], axis=-1)
    sinf = jnp.concatenate([-sin, sin], axis=-1)
    gam = 1.0 - np.exp2(-5.0 - np.arange(H_C, dtype=np.float64))
    ti = np.arange(t)
    d = ti[:, None] - ti[None, :]
    seg = np.where(d >= 0, gam[:, None, None] ** np.maximum(d, 0)[None], 0.0)
    pw = np.broadcast_to((gam[:, None] ** (ti + 1)[None])[:, :, None], (H_C, t, LANES))
    dte = gam[:, None] ** np.maximum(lv - 1 - ti, 0)[None]
    dte = np.broadcast_to(np.where(ti < lv, dte, 0.0)[:, :, None], (H_C, t, LANES))
    return cosf, sinf, jnp.asarray(seg, F32), jnp.asarray(pw, F32), jnp.asarray(dte, F32)


def _odd_mixer(z, states, li, p, *, t, lv, pos0):
    nseq, length, _ = z.shape
    nchunks = length // t
    has_init = states is not None
    assert lv == t or nchunks == 1
    tsub = min(GLA_SUB, t)
    cosf, sinf, seg, pw, dte = _ret_tables(t, lv, pos0, nchunks)
    full = lambda *shape: pl.BlockSpec(shape, lambda b, c: (0,) * len(shape))
    in_specs = [pl.BlockSpec((1, t, OD_PAD), lambda b, c: (b, c, 0))]
    args = [z]
    if has_init:
        in_specs += [pl.BlockSpec((None, 1, H_C, HEAD_C, HEAD_C), lambda b, c: (li, b, 0, 0, 0)),
                     pl.BlockSpec((None, 1, H_D, HEAD_DK, HEAD_DV), lambda b, c: (li, b, 0, 0, 0))]
        args += list(states)
    in_specs += [pl.BlockSpec((t, HEAD_C), lambda b, c: (c, 0)),
                 pl.BlockSpec((t, HEAD_C), lambda b, c: (c, 0)),
                 full(H_C, t, t), full(H_C, t, LANES), full(H_C, t, LANES),
                 full(1, D_MIX), full(LANES, H_D * HEAD_DK), full(1, H_D * HEAD_DK), full(1, D_MIX)]
    args += [cosf, sinf, seg, pw, dte, p['ret_norm_w'], p['gate_up'], p['gate_b'], p['gla_norm_w']]
    ret_shape, ret_spec = _state_out(states and states[0], li, nseq, (H_C, HEAD_C, HEAD_C))
    gla_shape, gla_spec = _state_out(states and states[1], li, nseq, (H_D, HEAD_DK, HEAD_DV))
    out_shape = (jax.ShapeDtypeStruct((nseq, length, D_MODEL), BF16), ret_shape, gla_shape)
    out_specs = (pl.BlockSpec((1, t, D_MODEL), lambda b, c: (b, c, 0)), ret_spec, gla_spec)
    scratch = [pltpu.VMEM((H_C, HEAD_C, HEAD_C), F32), pltpu.VMEM((H_D, HEAD_DV, HEAD_DK), F32),
               pltpu.VMEM((t, H_D * HEAD_DK), F32), pltpu.VMEM((t, D_MODEL), F32)]
    y, ret, gla = pl.pallas_call(
        functools.partial(_odd_kernel, t=t, tsub=tsub, nchunks=nchunks, lv=lv, has_init=has_init),
        grid=(nseq, nchunks),
        in_specs=in_specs, out_specs=out_specs, out_shape=out_shape, scratch_shapes=scratch,
        input_output_aliases={1: 1, 2: 2} if has_init else {},
        compiler_params=pltpu.CompilerParams(dimension_semantics=("arbitrary", "arbitrary"),
                                             vmem_limit_bytes=VMEM_LIMIT),
        name="odd_mixer",
    )(*args)
    return y, (ret, gla)


def _prep_proj(W):
    w = W['ev_w_in']
    n = w.shape[0]
    ev_in = jnp.concatenate([w[:, :, :RW_COLS], jnp.zeros((n, D_MODEL, RW_PAD - RW_COLS), F32),
                             w[:, :, RW_COLS:], jnp.zeros((n, D_MODEL, MB_PAD - MB_COLS), F32)], axis=2)
    w = W['od_w_in']
    n = w.shape[0]
    gd0 = OD_COLS - D_MIX - GLA_LORA
    od_in = jnp.concatenate([w[:, :, :gd0], w[:, :, gd0 + GLA_LORA:], w[:, :, gd0:gd0 + GLA_LORA],
                             jnp.zeros((n, D_MODEL, OD_PAD - OD_COLS), F32)], axis=2)
    return dict(ev_in=ev_in.astype(BF16), od_in=od_in.astype(BF16),
                ev_out=W['ev_w_out'].astype(BF16), od_out=W['od_w_out'].astype(BF16))


def _prep_even(i, W):
    mu =jnp.pad(W['rw_mu'][i], (0, RW_PAD - RW_COLS)).reshape(1, RW_PAD)
    rep = lambda v: jnp.repeat(v, HEAD_B)
    rows = [W['rw_w0'][i], W['rw_a0'][i], W['rw_k_k'][i], W['rw_k_a'][i], W['rw_r_k'][i].reshape(-1),
            W['rw_gn_w'][i], W['rw_gn_b'][i], rep(W['mb_d'][i]), W['mb_norm_w'][i]]
    vec = jnp.concatenate([jnp.stack(rows), jnp.zeros((EV_VEC_ROWS - len(rows), D_MIX), F32)], axis=0)
    wa = jnp.zeros((LANES, 2 * D_MIX), F32)
    wa = wa.at[:RW_LORA_W, :D_MIX].set(W['rw_w_up'][i]).at[RW_LORA_W:, D_MIX:].set(W['rw_a_up'][i])
    gup = jnp.pad(W['rw_g_up'][i], ((0, RW_PAD - 3 * D_MIX - LANES - RW_LORA_G), (0, 0)))
    pad_h = lambda v: jnp.pad(v, (0, LANES - H_B)).reshape(1, LANES)
    expand = np.zeros((LANES, D_MIX), np.float32)
    for h in range(H_B):
        expand[h, h * HEAD_B:(h + 1) * HEAD_B] = 1.0
    return dict(mu=mu, vec=vec, wa=wa.astype(BF16), gup=gup.astype(BF16), conv_w=W['mb_conv_w'][i],
                conv_b=W['mb_conv_b'][i].reshape(1, MB_CONV_CH), dt_bias=pad_h(W['mb_dt_bias'][i]),
                a_log=pad_h(W['mb_a_log'][i]), expand=jnp.asarray(expand, BF16))


def _prep_odd(i, W):
    gate_up = jnp.pad(W['gla_gate_up'][i], ((0, LANES - GLA_LORA), (0, 0)))
    return dict(ret_norm_w=W['ret_norm_w'][i].reshape(1, D_MIX), gate_up=gate_up.astype(BF16),
                gate_b=W['gla_gate_b'][i].reshape(1, H_D * HEAD_DK),
                gla_norm_w=W['gla_norm_w'][i].reshape(1, D_MIX))


def _trunk(h, mod, mc0, states, P, PW, FW, final_g, *, nseq, length, t, lv, pos0):
    outs = ([], [], [], [], [], [])
    wg, wu, wd = FW
    padded = -(-length // t) * t
    carried = None if states is None else list(states)
    for l in range(DEPTH):
        i = l // 2
        p = P[l]
        mc = dict(mc0, l=l)
        even = l % 2 == 0
        h = _ffn(h, mod, mc, 0, wg, wu, wd, 0)
        z = _inproj(h, mod, mc, PW['ev_in' if even else 'od_in'], i).reshape(nseq, length, -1)
        if padded != length:
            z = jnp.pad(z, ((0, 0), (0, padded - length), (0, 0)))
        if even:
            y, new = _even_mixer(z, carried and tuple(carried[:4]), i, p, t=t, lv=lv)
            slots = (0, 1, 2, 3)
        else:
            y, new = _odd_mixer(z, carried and tuple(carried[4:]), i, p, t=t, lv=lv, pos0=pos0)
            slots = (4, 5)
        for k, n in zip(slots, new):
            if carried is not None and k in (1, 3, 4, 5):
                carried[k] = n
            else:
                outs[k].append(n)
        y = y[:, :length].reshape(nseq * length, D_MODEL)
        h = _outproj(y, PW['ev_out' if even else 'od_out'], i, h, mod, mc)
        h = _ffn(h, mod, mc, 6, wg, wu, wd, 1, final_g=final_g if l == DEPTH - 1 else None)
    return h, tuple(jnp.stack(lst) if lst else carried[k] for k, lst in enumerate(outs))


def kernel(x_prompt, x_sample, state_rwkv_shift, state_rwkv_wkv, state_mamba_conv, state_mamba_ssm,
           state_ret, state_gla, c_prompt, c_sample, ada_w, ada_b, ffn_wg, ffn_wu, ffn_wd, ev_w_in,
           ev_w_out, rw_mu, rw_w0, rw_w_up, rw_a0, rw_a_up, rw_g_up, rw_k_k, rw_k_a, rw_r_k, rw_gn_w,
           rw_gn_b, mb_conv_w, mb_conv_b, mb_dt_bias, mb_a_log, mb_d, mb_norm_w, od_w_in, od_w_out,
           ret_norm_w, gla_gate_up, gla_gate_b, gla_norm_w, final_g):
    W = dict(ev_w_in=ev_w_in, ev_w_out=ev_w_out, rw_mu=rw_mu, rw_w0=rw_w0, rw_w_up=rw_w_up,
             rw_a0=rw_a0, rw_a_up=rw_a_up, rw_g_up=rw_g_up, rw_k_k=rw_k_k, rw_k_a=rw_k_a,
             rw_r_k=rw_r_k, rw_gn_w=rw_gn_w, rw_gn_b=rw_gn_b, mb_conv_w=mb_conv_w,
             mb_conv_b=mb_conv_b, mb_dt_bias=mb_dt_bias, mb_a_log=mb_a_log, mb_d=mb_d,
             mb_norm_w=mb_norm_w, od_w_in=od_w_in, od_w_out=od_w_out, ret_norm_w=ret_norm_w,
             gla_gate_up=gla_gate_up, gla_gate_b=gla_gate_b, gla_norm_w=gla_norm_w)
    nb, seq, _ = x_prompt.shape
    db, dseq, _ = x_sample.shape
    P = [(_prep_even if l % 2 == 0 else _prep_odd)(l // 2, W) for l in range(DEPTH)]
    PW = _prep_proj(W)
    FW = (ffn_wg.astype(BF16), ffn_wu.astype(BF16), ffn_wd.astype(BF16))

    rows = db + nb
    rows_pad = -(-rows // SUBLANES) * SUBLANES
    c_all = jnp.concatenate([c_sample, c_prompt, jnp.zeros((rows_pad - rows, D_MODEL), F32)], axis=0)
    mod = _ada(c_all, ada_w, ada_b)

    y_p, st_p = _trunk(x_prompt.reshape(nb * seq, D_MODEL), mod, dict(row0=db, rps=seq), None,
                       P, PW, FW, final_g, nseq=nb, length=seq, t=PROMPT_CHUNK, lv=PROMPT_CHUNK, pos0=0.0)

    states = (state_rwkv_shift, state_rwkv_wkv, state_mamba_conv, state_mamba_ssm, state_ret, state_gla)
    y_s, st_s = _trunk(x_sample.reshape(db * dseq, D_MODEL), mod, dict(row0=0, rps=dseq), states,
                       P, PW, FW, final_g, nseq=db, length=dseq, t=SAMPLE_CHUNK, lv=dseq,
                       pos0=float(PAST_LEN))
    return (y_p.reshape(nb, seq, D_MODEL), y_s.reshape(db, dseq, D_MODEL)) + st_p + st_s
```

```python
import functools
import math

import numpy as np
import jax
import jax.numpy as jnp
from jax import lax
from jax.experimental import pallas as pl
from jax.experimental.pallas import tpu as pltpu

F32 = jnp.float32
BF16 = jnp.bfloat16

D_MODEL = 2048
DEPTH = 4
PAST_LEN = 16384
N_MOD = 9
D_FF = 5632

D_MIX = 1024
H_A, HEAD_A = 16, 64
RW_LORA_W, RW_LORA_A, RW_LORA_G = 64, 64, 160
RW_COLS = 3 * D_MIX + RW_LORA_W + RW_LORA_A + RW_LORA_G
RW_GN_EPS = 64e-5
H_B, HEAD_B, D_STATE, N_GROUPS, CONV_W = 16, 64, 128, 2, 4
MB_CONV_CH = D_MIX + 2 * N_GROUPS * D_STATE
MB_COLS = D_MIX + MB_CONV_CH + H_B
H_C, HEAD_C = 8, 128
ROPE_BASE = 10000.0
H_D, HEAD_DK, HEAD_DV, GLA_LORA = 4, 128, 256, 16
GLA_NORMALIZER = 16.0
OD_COLS = 4 * D_MIX + 2 * H_D * HEAD_DK + 2 * D_MIX + GLA_LORA

LANES = 128
SUBLANES = 8
VMEM_LIMIT = 60 * 1024 * 1024

RW_PAD = 3456
ZM = RW_PAD
MB_PAD = 2688
EV_PAD = RW_PAD + MB_PAD
OD_PAD = 7296
OD_GD = 7168

PROMPT_CHUNK = 64
GLA_SUB = 32
SAMPLE_CHUNK = 8


def _mm(a, b):
    return jnp.dot(a.astype(BF16), b.astype(BF16), preferred_element_type=F32)


def _mm_nt(a, b):
    return lax.dot_general(a.astype(BF16), b.astype(BF16), (((1,), (1,)), ((), ())),
                           preferred_element_type=F32)


def _mm_tn(a, b):
    return lax.dot_general(a.astype(BF16), b.astype(BF16), (((0,), (0,)), ((), ())),
                           preferred_element_type=F32)


def _split3(x):
    hi = x.astype(BF16)
    r = x - hi.astype(F32)
    mid = r.astype(BF16)
    lo = (r - mid.astype(F32)).astype(BF16)
    return hi, mid, lo


def _mm_sel(sel, x):
    s = sel.astype(BF16)
    hi, mid, lo = _split3(x)
    return (jnp.dot(s, hi, preferred_element_type=F32) + jnp.dot(s, mid, preferred_element_type=F32)
            + jnp.dot(s, lo, preferred_element_type=F32))


def _mm_tn_sel(x, sel):
    s = sel.astype(BF16)
    dn = (((0,), (0,)), ((), ()))
    hi, mid, lo = _split3(x)
    return (lax.dot_general(hi, s, dn, preferred_element_type=F32)
            + lax.dot_general(mid, s, dn, preferred_element_type=F32)
            + lax.dot_general(lo, s, dn, preferred_element_type=F32))


def _mm_xsel(x, sel):
    s = sel.astype(BF16)
    hi, mid, lo = _split3(x)
    return (jnp.dot(hi, s, preferred_element_type=F32) + jnp.dot(mid, s, preferred_element_type=F32)
            + jnp.dot(lo, s, preferred_element_type=F32))


def _sigmoid(x):
    return 1.0 / (1.0 + jnp.exp(-x))


def _silu(x):
    return x * _sigmoid(x)


def _softplus(x):
    return jnp.maximum(x, 0.0) + jnp.log(1.0 + jnp.exp(-jnp.abs(x)))


def _rms(x, eps):
    return x * lax.rsqrt(jnp.mean(x * x, axis=-1, keepdims=True) + eps)


def _tri(n, strict):
    r = lax.broadcasted_iota(jnp.int32, (n, n), 0)
    c = lax.broadcasted_iota(jnp.int32, (n, n), 1)
    return (r > c) if strict else (r >= c)


ROW_TILE = 512
FFN_ROW_TILE = 512


def _mod_specs(mc, tm, idxs):
    l, rps, row0 = mc['l'], mc['rps'], mc['row0']
    specs = []
    for idx in idxs:
        if rps >= tm:
            specs.append(pl.BlockSpec(
                (None, None, SUBLANES, D_MODEL),
                lambda i, j, idx=idx: (l, idx, (row0 + i * tm // rps) // SUBLANES, 0)))
        else:
            assert row0 == 0
            specs.append(pl.BlockSpec((None, None, tm // rps, D_MODEL), lambda i, j, idx=idx: (l, idx, i, 0)))
    return specs


def _mod_val(ref, mc, tm):
    rps = mc['rps']
    if rps >= tm:
        r = (mc['row0'] + pl.program_id(0) * tm // rps) % SUBLANES
        return ref[pl.ds(r, 1), :]
    nseq = tm // rps
    row = lax.broadcasted_iota(jnp.int32, (tm, nseq), 0)
    first = lax.broadcasted_iota(jnp.int32, (tm, nseq), 1) * rps
    return _mm_sel((row >= first) & (row < first + rps), ref[...])


def _ada_kernel(c_ref, w_ref, b_ref, o_ref):
    x = _silu(c_ref[...]).astype(BF16)
    o_ref[...] = jnp.dot(x, w_ref[...].astype(BF16), preferred_element_type=F32) + b_ref[...]


def _ada(c, ada_w, ada_b):
    rows = c.shape[0]
    n = ada_w.shape[-1]
    tn = 1024
    per_mod = D_MODEL // tn
    return pl.pallas_call(
        _ada_kernel,
        grid=(DEPTH, n // tn),
        in_specs=[pl.BlockSpec((rows, D_MODEL), lambda l, j: (0, 0)),
                  pl.BlockSpec((None, D_MODEL, tn), lambda l, j: (l, 0, j)),
                  pl.BlockSpec((None, 1, tn), lambda l, j: (l, 0, j))],
        out_specs=pl.BlockSpec((None, None, rows, tn), lambda l, j: (l, j // per_mod, 0, j % per_mod)),
        out_shape=jax.ShapeDtypeStruct((DEPTH, N_MOD, rows, D_MODEL), F32),
        compiler_params=pltpu.CompilerParams(dimension_semantics=("arbitrary", "arbitrary"),
                                             vmem_limit_bytes=VMEM_LIMIT),
        name="ada",
    )(c, ada_w, ada_b.reshape(DEPTH, 1, n))


def _ffn_kernel(*refs, mc, tm, nj, final, emit):
    refs = list(refs)
    h_ref, sh_ref, sc_ref, g_ref, wg_ref, wu_ref, wd_ref = refs[:7]
    fin_ref = refs[7] if final else None
    o_ref = refs[7 + final]
    xn_ref = refs[-1]
    j = pl.program_id(1)

    @pl.when(j == 0)
    def _():
        xn = _rms(h_ref[...], 1e-6) * (1.0 + _mod_val(sc_ref, mc, tm)) + _mod_val(sh_ref, mc, tm)
        xn_ref[...] = xn.astype(BF16)
        o_ref[...] = jnp.zeros_like(o_ref)

    wg, wu, wd = wg_ref[...], wu_ref[...], wd_ref[...]
    if emit:
        wg, wu, wd = wg.astype(BF16), wu.astype(BF16), wd.astype(BF16)
        refs[8 + final][...] = wg
        refs[9 + final][...] = wu
        refs[10 + final][...] = wd
    for r in range(tm // ROW_TILE):
        rows = slice(r * ROW_TILE, (r + 1) * ROW_TILE)
        xn = xn_ref[rows, :]
        gate = jnp.dot(xn, wg, preferred_element_type=F32)
        up = jnp.dot(xn, wu, preferred_element_type=F32)
        act = (_silu(gate) * up).astype(BF16)
        o_ref[rows, :] += jnp.dot(act, wd, preferred_element_type=F32)

    @pl.when(j == nj - 1)
    def _():
        out = h_ref[...] + 0.5 * (1.0 + _mod_val(g_ref, mc, tm)) * o_ref[...]
        if final:
            out = _rms(out, 1e-6) * fin_ref[...]
        o_ref[...] = out


def _ffn(h, mod, mc, mod_base, w, k, final_g=None):
    m = h.shape[0]
    l = mc['l']
    emit = w[0].ndim == 4
    tm = min(FFN_ROW_TILE, m)
    tf = 256 if emit else 512
    nj = D_FF // tf
    final = final_g is not None
    in_specs = [pl.BlockSpec((tm, D_MODEL), lambda i, j: (i, 0))]
    in_specs += _mod_specs(mc, tm, (mod_base, mod_base + 1, mod_base + 2))
    col = pl.BlockSpec((D_MODEL, tf), lambda i, j: (0, j))
    row = pl.BlockSpec((tf, D_MODEL), lambda i, j: (j, 0))
    if emit:
        assert m == tm
        in_specs += [pl.BlockSpec((None, None, D_MODEL, tf), lambda i, j: (l, k, 0, j)),
                     pl.BlockSpec((None, None, D_MODEL, tf), lambda i, j: (l, k, 0, j)),
                     pl.BlockSpec((None, None, tf, D_MODEL), lambda i, j: (l, k, j, 0))]
    else:
        in_specs += [col, col, row]
    args = [h, mod, mod, mod, *w]
    if final:
        in_specs.append(pl.BlockSpec((1, D_MODEL), lambda i, j: (0, 0)))
        args.append(final_g.reshape(1, D_MODEL))
    out_specs = [pl.BlockSpec((tm, D_MODEL), lambda i, j: (i, 0))]
    out_shape = [jax.ShapeDtypeStruct((m, D_MODEL), F32)]
    if emit:
        out_specs += [col, col, row]
        out_shape += [jax.ShapeDtypeStruct((D_MODEL, D_FF), BF16), jax.ShapeDtypeStruct((D_MODEL, D_FF), BF16),
                      jax.ShapeDtypeStruct((D_FF, D_MODEL), BF16)]
    out = pl.pallas_call(
        functools.partial(_ffn_kernel, mc=mc, tm=tm, nj=nj, final=final, emit=emit),
        grid=(m // tm, nj),
        in_specs=in_specs, out_specs=out_specs, out_shape=out_shape,
        scratch_shapes=[pltpu.VMEM((tm, D_MODEL), BF16)],
        compiler_params=pltpu.CompilerParams(dimension_semantics=("arbitrary", "arbitrary"),
                                             vmem_limit_bytes=VMEM_LIMIT),
        name="ffn",
    )(*args)
    return (out[0], tuple(out[1:])) if emit else out[0]


def _inproj_kernel(h_ref, sh_ref, sc_ref, w_ref, o_ref, xn_ref, *, mc, tm):
    @pl.when(pl.program_id(1) == 0)
    def _():
        xn = _rms(h_ref[...], 1e-6) * (1.0 + _mod_val(sc_ref, mc, tm)) + _mod_val(sh_ref, mc, tm)
        xn_ref[...] = xn.astype(BF16)

    o_ref[...] = jnp.dot(xn_ref[...], w_ref[...], preferred_element_type=F32)


def _inproj(h, mod, mc, w, li):
    m = h.shape[0]
    n = w.shape[2]
    tm = min(ROW_TILE, m)
    tn = n // 3
    if n % (6 * LANES) == 0 and m % (2 * ROW_TILE) == 0:
        tm, tn = 2 * ROW_TILE, n // 6
    in_specs = [pl.BlockSpec((tm, D_MODEL), lambda i, j: (i, 0))]
    in_specs += _mod_specs(mc, tm, (3, 4))
    in_specs += [pl.BlockSpec((None, D_MODEL, tn), lambda i, j: (li, 0, j))]
    return pl.pallas_call(
        functools.partial(_inproj_kernel, mc=mc, tm=tm),
        grid=(m // tm, n // tn),
        in_specs=in_specs,
        out_specs=pl.BlockSpec((tm, tn), lambda i, j: (i, j)),
        out_shape=jax.ShapeDtypeStruct((m, n), F32),
        scratch_shapes=[pltpu.VMEM((tm, D_MODEL), BF16)],
        compiler_params=pltpu.CompilerParams(dimension_semantics=("arbitrary", "arbitrary"),
                                             vmem_limit_bytes=VMEM_LIMIT),
        name="inproj",
    )(h, mod, mod, w)


def _outproj_kernel(y_ref, w_ref, h_ref, g_ref, o_ref, *, mc, tm):
    mix = jnp.dot(y_ref[...], w_ref[...], preferred_element_type=F32)
    o_ref[...] = h_ref[...] + (1.0 + _mod_val(g_ref, mc, tm)) * mix


def _outproj(y, w, li, h, mod, mc):
    m = h.shape[0]
    tm = min(ROW_TILE, m)
    in_specs = [pl.BlockSpec((tm, D_MODEL), lambda i, j: (i, 0)),
                pl.BlockSpec((None, D_MODEL, D_MODEL), lambda i, j: (li, 0, 0)),
                pl.BlockSpec((tm, D_MODEL), lambda i, j: (i, 0))]
    in_specs += _mod_specs(mc, tm, (5,))
    return pl.pallas_call(
        functools.partial(_outproj_kernel, mc=mc, tm=tm),
        grid=(m // tm, 1),
        in_specs=in_specs,
        out_specs=pl.BlockSpec((tm, D_MODEL), lambda i, j: (i, 0)),
        out_shape=jax.ShapeDtypeStruct((m, D_MODEL), F32),
        compiler_params=pltpu.CompilerParams(dimension_semantics=("arbitrary", "arbitrary"),
                                             vmem_limit_bytes=VMEM_LIMIT),
        name="outproj",
    )(y, w, h, mod)


def _state_out(stacked, li, nseq, tail):
    zeros = (0,) * len(tail)
    if stacked is None:
        return (jax.ShapeDtypeStruct((nseq,) + tail, F32),
                pl.BlockSpec((1,) + tail, lambda b, c: (b,) + zeros))
    return (jax.ShapeDtypeStruct(stacked.shape, F32),
            pl.BlockSpec((None, 1) + tail, lambda b, c: (li, b) + zeros))


EV_W0, EV_A0, EV_KK, EV_KA, EV_RK, EV_GNW, EV_GNB, EV_D, EV_NW = range(9)
EV_VEC_ROWS = 16


def _even_kernel(*refs, t, nchunks, lv, has_init, hg):
    if has_init:
        (z_ref, shift0_ref, wkv0_ref, conv0_ref, ssm0_ref, *rest) = refs
    else:
        z_ref, *rest = refs
    (mu_ref, vec_ref, wa_ref, gup_ref, cw_ref, cb_ref, dtb_ref, alog_ref, exp_ref,
     y_ref, shift_o, wkv_o, conv_o, ssm_o,
     carry_ref, wkv_ref, ext_ref, ssm_ref,
     r_s, kp_s, v_s, kk_s, a_s, c_s, ld_s, g_s, y_s) = rest
    ci = pl.program_id(1)
    masked = lv < t
    nlog = int(math.log2(t))

    @pl.when(ci == 0)
    def _():
        if has_init:
            carry_ref[:, 0:RW_COLS] = shift0_ref[0]
            carry_ref[:, RW_COLS:RW_PAD] = jnp.zeros((1, RW_PAD - RW_COLS), F32)
            wkv_ref[...] = wkv0_ref[0]
            ext_ref[0:SUBLANES - CONV_W + 1, :] = jnp.zeros((SUBLANES - CONV_W + 1, MB_CONV_CH), F32)
            ext_ref[SUBLANES - CONV_W + 1:SUBLANES, :] = conv0_ref[0]
            for h in range(H_B):
                ssm_ref[:, h * HEAD_B:(h + 1) * HEAD_B] = ssm0_ref[0, h]
        else:
            carry_ref[...] = jnp.zeros_like(carry_ref)
            wkv_ref[...] = jnp.zeros_like(wkv_ref)
            ext_ref[0:SUBLANES, :] = jnp.zeros((SUBLANES, MB_CONV_CH), F32)
            ssm_ref[...] = jnp.zeros_like(ssm_ref)

    row = lax.broadcasted_iota(jnp.int32, (t, 1), 0)
    valid = row < lv
    strict = _tri(t, True)
    incl = _tri(t, False)

    def vec(i, lo=0, hi=D_MIX):
        return vec_ref[i:i + 1, lo:hi]

    def shifted(a, b):
        cur = z_ref[0, :, a:b]
        prev = jnp.where(row == 0, carry_ref[:, a:b], pltpu.roll(cur, 1, axis=0))
        return cur + mu_ref[:, a:b] * (prev - cur)

    r_s[...] = shifted(0, D_MIX)
    k = shifted(D_MIX, 2 * D_MIX)
    v_s[...] = shifted(2 * D_MIX, 3 * D_MIX)
    lo = shifted(3 * D_MIX, 3 * D_MIX + LANES)
    lane = lax.broadcasted_iota(jnp.int32, (t, LANES), 1)
    wa = _mm(jnp.where(lane < RW_LORA_W, jnp.tanh(lo), lo), wa_ref[...])
    w_log = -_softplus(-(vec(EV_W0) + wa[:, 0:D_MIX])) - 0.5
    ld = -jnp.exp(w_log)
    a = _sigmoid(vec(EV_A0) + wa[:, D_MIX:2 * D_MIX])
    g_s[...] = _mm(_sigmoid(shifted(3 * D_MIX + LANES, RW_PAD)), gup_ref[...])
    kp = k * (1.0 + (a - 1.0) * vec(EV_KA))
    if masked:
        ld = jnp.where(valid, ld, 0.0)
        kp = jnp.where(valid, kp, 0.0)
        a = jnp.where(valid, a, 0.0)
    kk_s[...] = k * vec(EV_KK)
    kp_s[...] = kp
    a_s[...] = a
    ld_s[...] = ld
    c_s[...] = _mm_sel(incl, ld)
    last_row = lv - 1
    carry_ref[...] = z_ref[0, last_row:last_row + 1, 0:RW_PAD]

    for h0 in range(0, H_A, hg):
        heads = range(h0, h0 + hg)
        sls = [slice(h * HEAD_A, (h + 1) * HEAD_A) for h in heads]
        rs = [r_s[:, sl] for sl in sls]
        kps = [kp_s[:, sl] for sl in sls]
        vs = [v_s[:, sl] for sl in sls]
        cs = [c_s[:, sl] for sl in sls]
        kks, bs = [], []
        for sl in sls:
            kkr = kk_s[:, sl]
            nrm = jnp.sqrt(jnp.sum(kkr * kkr, axis=-1, keepdims=True))
            kks.append(kkr / jnp.maximum(nrm, 1e-12))
            bs.append(kks[-1] * a_s[:, sl])
        ecs = [jnp.exp(c) for c in cs]
        ecis = [jnp.exp(-c) for c in cs]
        lhss = [jnp.concatenate([kk * jnp.exp(c - ld_s[:, sl]), r * ec], axis=0)
                for kk, c, sl, r, ec in zip(kks, cs, sls, rs, ecs)]
        s0s = [wkv_ref[h] for h in heads]
        a1s = [_mm_nt(lhs, kp * eci) for lhs, kp, eci in zip(lhss, kps, ecis)]
        a2s = [_mm_nt(lhs, b * eci) for lhs, b, eci in zip(lhss, bs, ecis)]
        pms = [_mm_nt(lhs, s0) for lhs, s0 in zip(lhss, s0s)]
        us = [pm[0:t] + _mm(jnp.where(strict, a1[0:t], 0.0), v) for pm, a1, v in zip(pms, a1s, vs)]
        ps = [jnp.where(strict, -a2[0:t], 0.0) for a2 in a2s]
        us = [u + _mm(p, u) for u, p in zip(us, ps)]
        for _ in range(nlog - 1):
            ps = [_mm(p, p) for p in ps]
            us = [u + _mm(p, u) for u, p in zip(us, ps)]
        ys = [pm[t:2 * t] + _mm(jnp.where(incl, a1[t:2 * t], 0.0), v) - _mm(jnp.where(incl, a2[t:2 * t], 0.0), u)
              for pm, a1, a2, v, u in zip(pms, a1s, a2s, vs, us)]
        for i, h in enumerate(heads):
            c, kp, b, v, u = cs[i], kps[i], bs[i], vs[i], us[i]
            dec_end = jnp.exp(c[t - 1:t] - c)
            vu = jnp.concatenate([v, u], axis=0)
            kb = jnp.concatenate([kp * dec_end, -(b * dec_end)], axis=0)
            wkv_ref[h] = s0s[i] * ecs[i][t - 1:t] + _mm_tn(vu, kb)
        for i, sl in enumerate(sls):
            y, r, kp, v = ys[i], rs[i], kps[i], vs[i]
            mean = jnp.mean(y, axis=-1, keepdims=True)
            yc = y - mean
            var = jnp.mean(yc * yc, axis=-1, keepdims=True)
            yn = (yc * lax.rsqrt(var + RW_GN_EPS) * vec(EV_GNW, sl.start, sl.stop)
                  + vec(EV_GNB, sl.start, sl.stop))
            bonus = jnp.sum(r * kp * vec(EV_RK, sl.start, sl.stop), axis=-1, keepdims=True) * v
            y_s[:, sl] = (yn + bonus) * g_s[:, sl]

    zg = z_ref[0, :, ZM:ZM + D_MIX]
    ext_ref[SUBLANES:SUBLANES + t, :] = z_ref[0, :, ZM + D_MIX:ZM + D_MIX + MB_CONV_CH]
    conv = cb_ref[...]
    for w in range(CONV_W):
        off = SUBLANES - (CONV_W - 1) + w
        conv = conv + cw_ref[w:w + 1, :] * ext_ref[off:off + t, :]
    new_conv = ext_ref[SUBLANES + lv - (CONV_W - 1):SUBLANES + lv, :]
    ext_ref[SUBLANES - (CONV_W - 1):SUBLANES, :] = new_conv
    xbc = _silu(conv)
    xs = xbc[:, 0:D_MIX]
    dt = _softplus(z_ref[0, :, ZM + D_MIX + MB_CONV_CH:ZM + MB_PAD] + dtb_ref[...])
    logd = dt * (-jnp.exp(alog_ref[...]))
    if masked:
        logd = jnp.where(valid, logd, 0.0)
    c = _mm_sel(incl, logd)
    upper = lax.broadcasted_iota(jnp.int32, (t, t), 0) <= lax.broadcasted_iota(jnp.int32, (t, t), 1)
    c_t = _mm_tn_sel(logd, upper)
    wide = _mm_xsel(jnp.concatenate([dt, c, c[t - 1:t] - c], axis=0), exp_ref[...])
    xdt = xs * wide[0:t]
    if masked:
        xdt = jnp.where(valid, xdt, 0.0)
    ec_w = jnp.exp(wide[t:2 * t])
    dec_end_w = jnp.exp(wide[2 * t:3 * t])
    ecl_w = ec_w[t - 1:t]
    gw = (H_B // N_GROUPS) * HEAD_B
    hpg = H_B // N_GROUPS
    bgs = [xbc[:, D_MIX + g * D_STATE:D_MIX + (g + 1) * D_STATE] for g in range(N_GROUPS)]
    cgs = [xbc[:, D_MIX + (N_GROUPS + g) * D_STATE:D_MIX + (N_GROUPS + g + 1) * D_STATE]
           for g in range(N_GROUPS)]
    scores = [_mm_nt(cg, bg) for cg, bg in zip(cgs, bgs)]
    s_prevs = [ssm_ref[:, g * gw:(g + 1) * gw] for g in range(N_GROUPS)]
    inters = [_mm(cg, sp) for cg, sp in zip(cgs, s_prevs)]
    segs = [jnp.where(incl, jnp.exp(jnp.where(incl, c[:, h:h + 1] - c_t[h:h + 1, :], 0.0)), 0.0)
            for h in range(H_B)]
    intra = [_mm(scores[h // hpg] * segs[h], xdt[:, h * HEAD_B:(h + 1) * HEAD_B]) for h in range(H_B)]
    for g in range(N_GROUPS):
        gs = slice(g * gw, (g + 1) * gw)
        ssm_ref[:, gs] = s_prevs[g] * ecl_w[:, gs] + _mm_tn(bgs[g], xdt[:, gs] * dec_end_w[:, gs])
    o = jnp.concatenate(intra, axis=1) + jnp.concatenate(inters, axis=1) * ec_w
    yb = (o + vec(EV_D) * xs) * _silu(zg)
    y_s[:, D_MIX:2 * D_MIX] = _rms(yb, 1e-5) * vec(EV_NW)
    y_ref[0] = y_s[...].astype(BF16)

    @pl.when(ci == nchunks - 1)
    def _():
        shift_o[0] = carry_ref[...]
        wkv_o[0] = wkv_ref[...]
        conv_o[0] = new_conv
        for h in range(H_B):
            ssm_o[0, h] = ssm_ref[:, h * HEAD_B:(h + 1) * HEAD_B]


def _even_mixer(z, states, li, p, *, t, lv):
    nseq, length, _ = z.shape
    nchunks = length // t
    has_init = states is not None
    assert lv == t or nchunks == 1
    full = lambda *shape: pl.BlockSpec(shape, lambda b, c: (0,) * len(shape))
    in_specs = [pl.BlockSpec((1, t, EV_PAD), lambda b, c: (b, c, 0))]
    args = [z]
    if has_init:
        shift0, wkv0, conv0, ssm0 = states
        in_specs += [pl.BlockSpec((None, 1, 1, RW_COLS), lambda b, c: (li, b, 0, 0)),
                     pl.BlockSpec((None, 1, H_A, HEAD_A, HEAD_A), lambda b, c: (li, b, 0, 0, 0)),
                     pl.BlockSpec((None, 1, CONV_W - 1, MB_CONV_CH), lambda b, c: (li, b, 0, 0)),
                     pl.BlockSpec((None, 1, H_B, D_STATE, HEAD_B), lambda b, c: (li, b, 0, 0, 0))]
        args += [shift0.reshape(-1, nseq, 1, RW_COLS), wkv0, conv0, ssm0]
    in_specs += [full(1, RW_PAD), full(EV_VEC_ROWS, D_MIX), full(LANES, 2 * D_MIX),
                 full(RW_PAD - 3 * D_MIX - LANES, D_MIX), full(CONV_W, MB_CONV_CH),
                 full(1, MB_CONV_CH), full(1, LANES), full(1, LANES), full(LANES, D_MIX)]
    args += [p['mu'], p['vec'], p['wa'], p['gup'], p['conv_w'], p['conv_b'], p['dt_bias'],
             p['a_log'], p['expand']]
    wkv_shape, wkv_spec = _state_out(states and states[1], li, nseq, (H_A, HEAD_A, HEAD_A))
    ssm_shape, ssm_spec = _state_out(states and states[3], li, nseq, (H_B, D_STATE, HEAD_B))
    out_shape = (jax.ShapeDtypeStruct((nseq, length, D_MODEL), BF16),
                 jax.ShapeDtypeStruct((nseq, 1, RW_PAD), F32), wkv_shape,
                 jax.ShapeDtypeStruct((nseq, CONV_W - 1, MB_CONV_CH), F32), ssm_shape)
    out_specs = (pl.BlockSpec((1, t, D_MODEL), lambda b, c: (b, c, 0)),
                 pl.BlockSpec((1, 1, RW_PAD), lambda b, c: (b, 0, 0)), wkv_spec,
                 pl.BlockSpec((1, CONV_W - 1, MB_CONV_CH), lambda b, c: (b, 0, 0)), ssm_spec)
    wide = lambda: pltpu.VMEM((t, D_MIX), F32)
    scratch = [pltpu.VMEM((1, RW_PAD), F32), pltpu.VMEM((H_A, HEAD_A, HEAD_A), F32),
               pltpu.VMEM((t + SUBLANES, MB_CONV_CH), F32), pltpu.VMEM((D_STATE, H_B * HEAD_B), F32)]
    scratch += [wide() for _ in range(8)] + [pltpu.VMEM((t, D_MODEL), F32)]
    y, shift, wkv, conv, ssm = pl.pallas_call(
        functools.partial(_even_kernel, t=t, nchunks=nchunks, lv=lv, has_init=has_init,
                          hg=H_A),
        grid=(nseq, nchunks),
        in_specs=in_specs, out_specs=out_specs, out_shape=out_shape, scratch_shapes=scratch,
        input_output_aliases={2: 2, 4: 4} if has_init else {},
        compiler_params=pltpu.CompilerParams(dimension_semantics=("arbitrary", "arbitrary"),
                                             vmem_limit_bytes=VMEM_LIMIT),
        name="even_mixer",
    )(*args)
    return y, (shift[:, 0, :RW_COLS], wkv, conv, ssm)


def _odd_kernel(*refs, t, tsub, nchunks, lv, has_init):
    if has_init:
        z_ref, ret0_ref, gla0_ref, *rest = refs
    else:
        z_ref, *rest = refs
    (cos_ref, sin_ref, seg_ref, pw_ref, dte_ref, rnw_ref, gup_ref, gb_ref, gnw_ref,
     y_ref, ret_o, gla_o, ret_ref, gla_ref, la_s, y_s) = rest
    ci = pl.program_id(1)
    masked = lv < t

    @pl.when(ci == 0)
    def _():
        if has_init:
            ret_ref[...] = ret0_ref[0]
            for h in range(H_D):
                gla_ref[h] = jnp.transpose(gla0_ref[0, h])
        else:
            ret_ref[...] = jnp.zeros_like(ret_ref)
            gla_ref[...] = jnp.zeros_like(gla_ref)

    row = lax.broadcasted_iota(jnp.int32, (t, 1), 0)
    valid = row < lv

    cosf = cos_ref[...]
    sinf = sin_ref[...]

    def rotary(x):
        return x * cosf + pltpu.roll(x, HEAD_C // 2, axis=1) * sinf

    hc = range(H_C)
    qs = [rotary(z_ref[0, :, h * HEAD_C:(h + 1) * HEAD_C]) * HEAD_C ** -0.5 for h in hc]
    ks = [rotary(z_ref[0, :, D_MIX + h * HEAD_C:D_MIX + (h + 1) * HEAD_C]) for h in hc]
    if masked:
        ks = [jnp.where(valid, k, 0.0) for k in ks]
    vs = [z_ref[0, :, 2 * D_MIX + h * HEAD_C:2 * D_MIX + (h + 1) * HEAD_C] for h in hc]
    s0s = [ret_ref[h] for h in hc]
    sc = [_mm_nt(q, k) * seg_ref[h] for h, q, k in zip(hc, qs, ks)]
    inter = [_mm(q, s0) * pw_ref[h] for h, q, s0 in zip(hc, qs, s0s)]
    outs = [_mm(s, v) + i for s, v, i in zip(sc, vs, inter)]
    for h in hc:
        gamma = 1.0 - 2.0 ** (-5.0 - h)
        ret_ref[h] = s0s[h] * gamma ** lv + _mm_tn(ks[h] * dte_ref[h], vs[h])
    for h in hc:
        sl = slice(h * HEAD_C, (h + 1) * HEAD_C)
        rg = z_ref[0, :, 3 * D_MIX + h * HEAD_C:3 * D_MIX + (h + 1) * HEAD_C]
        y_s[:, sl] = _rms(outs[h], 1e-6) * rnw_ref[:, sl] * _silu(rg)

    gq0 = 4 * D_MIX
    gk0 = gq0 + H_D * HEAD_DK
    gv0 = gk0 + H_D * HEAD_DK
    gr0 = gv0 + D_MIX
    gd = z_ref[0, :, OD_GD:OD_GD + LANES]
    x = _mm(gd, gup_ref[...]) + gb_ref[...]
    la = -_softplus(-x) * (1.0 / GLA_NORMALIZER)
    if masked:
        la = jnp.where(valid, la, 0.0)
    la_s[...] = la
    incl = _tri(tsub, False)
    for s in range(t // tsub):
        rs = slice(s * tsub, (s + 1) * tsub)
        cum_all = _mm_sel(incl, la_s[rs, :])
        hd = range(H_D)
        cums = [cum_all[:, h * HEAD_DK:(h + 1) * HEAD_DK] for h in hd]
        qes = [z_ref[0, rs, gq0 + h * HEAD_DK:gq0 + (h + 1) * HEAD_DK] * HEAD_DK ** -0.5 * jnp.exp(cums[h])
               for h in hd]
        ks = [z_ref[0, rs, gk0 + h * HEAD_DK:gk0 + (h + 1) * HEAD_DK] for h in hd]
        if masked:
            ks = [jnp.where(valid[rs], k, 0.0) for k in ks]
        vs = [z_ref[0, rs, gv0 + h * HEAD_DV:gv0 + (h + 1) * HEAD_DV] for h in hd]
        sts = [gla_ref[h] for h in hd]
        scores = [jnp.where(incl, _mm_nt(qes[h], ks[h] * jnp.exp(-cums[h])), 0.0) for h in hd]
        outs = [_mm(scores[h], vs[h]) + _mm_nt(qes[h], sts[h]) for h in hd]
        for h in hd:
            last = cums[h][tsub - 1:tsub]
            gla_ref[h] = sts[h] * jnp.exp(last) + _mm_tn(vs[h], ks[h] * jnp.exp(last - cums[h]))
        for h in hd:
            gr = z_ref[0, rs, gr0 + h * HEAD_DV:gr0 + (h + 1) * HEAD_DV]
            ys = slice(D_MIX + h * HEAD_DV, D_MIX + (h + 1) * HEAD_DV)
            y_s[rs, ys] = _rms(outs[h], 1e-6) * gnw_ref[:, h * HEAD_DV:(h + 1) * HEAD_DV] * _silu(gr)
    y_ref[0] = y_s[...].astype(BF16)

    @pl.when(ci == nchunks - 1)
    def _():
        ret_o[0] = ret_ref[...]
        for h in range(H_D):
            gla_o[0, h] = jnp.transpose(gla_ref[h])


def _ret_tables(t, lv, pos0, nchunks):
    half = HEAD_C // 2
    inv = ROPE_BASE ** (-jnp.arange(half, dtype=F32) / half)
    pos = pos0 + jnp.arange(t * nchunks, dtype=F32)
    ang = pos[:, None] * inv[None, :]
    cos, sin = jnp.cos(ang), jnp.sin(ang)
    cosf = jnp.concatenate([cos, cos], axis=-1)
    sinf = jnp.concatenate([-sin, sin], axis=-1)
    gam = 1.0 - np.exp2(-5.0 - np.arange(H_C, dtype=np.float64))
    ti = np.arange(t)
    d = ti[:, None] - ti[None, :]
    seg = np.where(d >= 0, gam[:, None, None] ** np.maximum(d, 0)[None], 0.0)
    pw = np.broadcast_to((gam[:, None] ** (ti + 1)[None])[:, :, None], (H_C, t, LANES))
    dte = gam[:, None] ** np.maximum(lv - 1 - ti, 0)[None]
    dte = np.broadcast_to(np.where(ti < lv, dte, 0.0)[:, :, None], (H_C, t, LANES))
    return cosf, sinf, jnp.asarray(seg, F32), jnp.asarray(pw, F32), jnp.asarray(dte, F32)


def _odd_mixer(z, states, li, p, *, t, lv, pos0):
    nseq, length, _ = z.shape
    nchunks = length // t
    has_init = states is not None
    assert lv == t or nchunks == 1
    tsub = min(GLA_SUB, t)
    cosf, sinf, seg, pw, dte = _ret_tables(t, lv, pos0, nchunks)
    full = lambda *shape: pl.BlockSpec(shape, lambda b, c: (0,) * len(shape))
    in_specs = [pl.BlockSpec((1, t, OD_PAD), lambda b, c: (b, c, 0))]
    args = [z]
    if has_init:
        in_specs += [pl.BlockSpec((None, 1, H_C, HEAD_C, HEAD_C), lambda b, c: (li, b, 0, 0, 0)),
                     pl.BlockSpec((None, 1, H_D, HEAD_DK, HEAD_DV), lambda b, c: (li, b, 0, 0, 0))]
        args += list(states)
    in_specs += [pl.BlockSpec((t, HEAD_C), lambda b, c: (c, 0)),
                 pl.BlockSpec((t, HEAD_C), lambda b, c: (c, 0)),
                 full(H_C, t, t), full(H_C, t, LANES), full(H_C, t, LANES),
                 full(1, D_MIX), full(LANES, H_D * HEAD_DK), full(1, H_D * HEAD_DK), full(1, D_MIX)]
    args += [cosf, sinf, seg, pw, dte, p['ret_norm_w'], p['gate_up'], p['gate_b'], p['gla_norm_w']]
    ret_shape, ret_spec = _state_out(states and states[0], li, nseq, (H_C, HEAD_C, HEAD_C))
    gla_shape, gla_spec = _state_out(states and states[1], li, nseq, (H_D, HEAD_DK, HEAD_DV))
    out_shape = (jax.ShapeDtypeStruct((nseq, length, D_MODEL), BF16), ret_shape, gla_shape)
    out_specs = (pl.BlockSpec((1, t, D_MODEL), lambda b, c: (b, c, 0)), ret_spec, gla_spec)
    scratch = [pltpu.VMEM((H_C, HEAD_C, HEAD_C), F32), pltpu.VMEM((H_D, HEAD_DV, HEAD_DK), F32),
               pltpu.VMEM((t, H_D * HEAD_DK), F32), pltpu.VMEM((t, D_MODEL), F32)]
    y, ret, gla = pl.pallas_call(
        functools.partial(_odd_kernel, t=t, tsub=tsub, nchunks=nchunks, lv=lv, has_init=has_init),
        grid=(nseq, nchunks),
        in_specs=in_specs, out_specs=out_specs, out_shape=out_shape, scratch_shapes=scratch,
        input_output_aliases={1: 1, 2: 2} if has_init else {},
        compiler_params=pltpu.CompilerParams(dimension_semantics=("arbitrary", "arbitrary"),
                                             vmem_limit_bytes=VMEM_LIMIT),
        name="odd_mixer",
    )(*args)
    return y, (ret, gla)


def _prep_proj(W):
    w = W['ev_w_in']
    n = w.shape[0]
    ev_in = jnp.concatenate([w[:, :, :RW_COLS], jnp.zeros((n, D_MODEL, RW_PAD - RW_COLS), F32),
                             w[:, :, RW_COLS:], jnp.zeros((n, D_MODEL, MB_PAD - MB_COLS), F32)], axis=2)
    w = W['od_w_in']
    n = w.shape[0]
    gd0 = OD_COLS - D_MIX - GLA_LORA
    od_in = jnp.concatenate([w[:, :, :gd0], w[:, :, gd0 + GLA_LORA:], w[:, :, gd0:gd0 + GLA_LORA],
                             jnp.zeros((n, D_MODEL, OD_PAD - OD_COLS), F32)], axis=2)
    return dict(ev_in=ev_in.astype(BF16), od_in=od_in.astype(BF16),
                ev_out=W['ev_w_out'].astype(BF16), od_out=W['od_w_out'].astype(BF16))


def _prep_even(i, W):
    mu =jnp.pad(W['rw_mu'][i], (0, RW_PAD - RW_COLS)).reshape(1, RW_PAD)
    rep = lambda v: jnp.repeat(v, HEAD_B)
    rows = [W['rw_w0'][i], W['rw_a0'][i], W['rw_k_k'][i], W['rw_k_a'][i], W['rw_r_k'][i].reshape(-1),
            W['rw_gn_w'][i], W['rw_gn_b'][i], rep(W['mb_d'][i]), W['mb_norm_w'][i]]
    vec = jnp.concatenate([jnp.stack(rows), jnp.zeros((EV_VEC_ROWS - len(rows), D_MIX), F32)], axis=0)
    wa = jnp.zeros((LANES, 2 * D_MIX), F32)
    wa = wa.at[:RW_LORA_W, :D_MIX].set(W['rw_w_up'][i]).at[RW_LORA_W:, D_MIX:].set(W['rw_a_up'][i])
    gup = jnp.pad(W['rw_g_up'][i], ((0, RW_PAD - 3 * D_MIX - LANES - RW_LORA_G), (0, 0)))
    pad_h = lambda v: jnp.pad(v, (0, LANES - H_B)).reshape(1, LANES)
    expand = np.zeros((LANES, D_MIX), np.float32)
    for h in range(H_B):
        expand[h, h * HEAD_B:(h + 1) * HEAD_B] = 1.0
    return dict(mu=mu, vec=vec, wa=wa.astype(BF16), gup=gup.astype(BF16), conv_w=W['mb_conv_w'][i],
                conv_b=W['mb_conv_b'][i].reshape(1, MB_CONV_CH), dt_bias=pad_h(W['mb_dt_bias'][i]),
                a_log=pad_h(W['mb_a_log'][i]), expand=jnp.asarray(expand, BF16))


def _prep_odd(i, W):
    gate_up = jnp.pad(W['gla_gate_up'][i], ((0, LANES - GLA_LORA), (0, 0)))
    return dict(ret_norm_w=W['ret_norm_w'][i].reshape(1, D_MIX), gate_up=gate_up.astype(BF16),
                gate_b=W['gla_gate_b'][i].reshape(1, H_D * HEAD_DK),
                gla_norm_w=W['gla_norm_w'][i].reshape(1, D_MIX))


def _trunk(h, mod, mc0, states, P, PW, FW, final_g, *, nseq, length, t, lv, pos0):
    outs = ([], [], [], [], [], [])
    padded = -(-length // t) * t
    carried = None if states is None else list(states)
    stacked_f32 = isinstance(FW, tuple)
    bf16_weights = []

    def ffn(h, mc, mod_base, k, final_g=None):
        if not stacked_f32:
            return _ffn(h, mod, mc, mod_base, FW[2 * mc['l'] + k], k, final_g)
        h, w16 = _ffn(h, mod, mc, mod_base, FW, k, final_g)
        bf16_weights.append(w16)
        return h

    for l in range(DEPTH):
        i = l // 2
        p = P[l]
        mc = dict(mc0, l=l)
        even = l % 2 == 0
        h = ffn(h, mc, 0, 0)
        z = _inproj(h, mod, mc, PW['ev_in' if even else 'od_in'], i).reshape(nseq, length, -1)
        if padded != length:
            z = jnp.pad(z, ((0, 0), (0, padded - length), (0, 0)))
        if even:
            y, new = _even_mixer(z, carried and tuple(carried[:4]), i, p, t=t, lv=lv)
            slots = (0, 1, 2, 3)
        else:
            y, new = _odd_mixer(z, carried and tuple(carried[4:]), i, p, t=t, lv=lv, pos0=pos0)
            slots = (4, 5)
        for k, n in zip(slots, new):
            if carried is not None and k in (1, 3, 4, 5):
                carried[k] = n
            else:
                outs[k].append(n)
        y = y[:, :length].reshape(nseq * length, D_MODEL)
        h = _outproj(y, PW['ev_out' if even else 'od_out'], i, h, mod, mc)
        h = ffn(h, mc, 6, 1, final_g=final_g if l == DEPTH - 1 else None)
    return h, tuple(jnp.stack(lst) if lst else carried[k] for k, lst in enumerate(outs)), bf16_weights


def kernel(x_prompt, x_sample, state_rwkv_shift, state_rwkv_wkv, state_mamba_conv, state_mamba_ssm,
           state_ret, state_gla, c_prompt, c_sample, ada_w, ada_b, ffn_wg, ffn_wu, ffn_wd, ev_w_in,
           ev_w_out, rw_mu, rw_w0, rw_w_up, rw_a0, rw_a_up, rw_g_up, rw_k_k, rw_k_a, rw_r_k, rw_gn_w,
           rw_gn_b, mb_conv_w, mb_conv_b, mb_dt_bias, mb_a_log, mb_d, mb_norm_w, od_w_in, od_w_out,
           ret_norm_w, gla_gate_up, gla_gate_b, gla_norm_w, final_g):
    W = dict(ev_w_in=ev_w_in, ev_w_out=ev_w_out, rw_mu=rw_mu, rw_w0=rw_w0, rw_w_up=rw_w_up,
             rw_a0=rw_a0, rw_a_up=rw_a_up, rw_g_up=rw_g_up, rw_k_k=rw_k_k, rw_k_a=rw_k_a,
             rw_r_k=rw_r_k, rw_gn_w=rw_gn_w, rw_gn_b=rw_gn_b, mb_conv_w=mb_conv_w,
             mb_conv_b=mb_conv_b, mb_dt_bias=mb_dt_bias, mb_a_log=mb_a_log, mb_d=mb_d,
             mb_norm_w=mb_norm_w, od_w_in=od_w_in, od_w_out=od_w_out, ret_norm_w=ret_norm_w,
             gla_gate_up=gla_gate_up, gla_gate_b=gla_gate_b, gla_norm_w=gla_norm_w)
    nb, seq, _ = x_prompt.shape
    db, dseq, _ = x_sample.shape
    P = [(_prep_even if l % 2 == 0 else _prep_odd)(l // 2, W) for l in range(DEPTH)]
    PW = _prep_proj(W)

    rows = db + nb
    rows_pad = -(-rows // SUBLANES) * SUBLANES
    c_all = jnp.concatenate([c_sample, c_prompt, jnp.zeros((rows_pad - rows, D_MODEL), F32)], axis=0)
    mod = _ada(c_all, ada_w, ada_b)

    states = (state_rwkv_shift, state_rwkv_wkv, state_mamba_conv, state_mamba_ssm, state_ret, state_gla)
    y_s, st_s, w16 = _trunk(x_sample.reshape(db * dseq, D_MODEL), mod, dict(row0=0, rps=dseq), states,
                            P, PW, (ffn_wg, ffn_wu, ffn_wd), final_g, nseq=db, length=dseq,
                            t=SAMPLE_CHUNK, lv=dseq, pos0=float(PAST_LEN))
    y_p, st_p, _ = _trunk(x_prompt.reshape(nb * seq, D_MODEL), mod, dict(row0=db, rps=seq), None,
                          P, PW, w16, final_g, nseq=nb, length=seq, t=PROMPT_CHUNK, lv=PROMPT_CHUNK,
                          pos0=0.0)
    return (y_p.reshape(nb, seq, D_MODEL), y_s.reshape(db, dseq, D_MODEL)) + st_p + st_s
```

```python
import functools
import math

import numpy as np
import jax
import jax.numpy as jnp
from jax import lax
from jax.experimental import pallas as pl
from jax.experimental.pallas import tpu as pltpu

F32 = jnp.float32
BF16 = jnp.bfloat16

D_MODEL = 2048
DEPTH = 4
PAST_LEN = 16384
N_MOD = 9
D_FF = 5632

D_MIX = 1024
H_A, HEAD_A = 16, 64
RW_LORA_W, RW_LORA_A, RW_LORA_G = 64, 64, 160
RW_COLS = 3 * D_MIX + RW_LORA_W + RW_LORA_A + RW_LORA_G
RW_GN_EPS = 64e-5
H_B, HEAD_B, D_STATE, N_GROUPS, CONV_W = 16, 64, 128, 2, 4
MB_CONV_CH = D_MIX + 2 * N_GROUPS * D_STATE
MB_COLS = D_MIX + MB_CONV_CH + H_B
H_C, HEAD_C = 8, 128
ROPE_BASE = 10000.0
H_D, HEAD_DK, HEAD_DV, GLA_LORA = 4, 128, 256, 16
GLA_NORMALIZER = 16.0
OD_COLS = 4 * D_MIX + 2 * H_D * HEAD_DK + 2 * D_MIX + GLA_LORA

LANES = 128
SUBLANES = 8
VMEM_LIMIT = 60 * 1024 * 1024

RW_PAD = 3456
ZM = RW_PAD
MB_PAD = 2688
EV_PAD = RW_PAD + MB_PAD
OD_PAD = 7296
OD_GD = 7168

PROMPT_CHUNK = 64
GLA_SUB = 32
SAMPLE_CHUNK = 8
SAMPLE_GROUP = 8


def _mm(a, b):
    return jnp.dot(a.astype(BF16), b.astype(BF16), preferred_element_type=F32)


def _mm_nt(a, b):
    return lax.dot_general(a.astype(BF16), b.astype(BF16), (((1,), (1,)), ((), ())),
                           preferred_element_type=F32)


def _mm_tn(a, b):
    return lax.dot_general(a.astype(BF16), b.astype(BF16), (((0,), (0,)), ((), ())),
                           preferred_element_type=F32)


def _split3(x):
    hi = x.astype(BF16)
    r = x - hi.astype(F32)
    mid = r.astype(BF16)
    lo = (r - mid.astype(F32)).astype(BF16)
    return hi, mid, lo


def _mm_sel(sel, x):
    s = sel.astype(BF16)
    hi, mid, lo = _split3(x)
    return (jnp.dot(s, hi, preferred_element_type=F32) + jnp.dot(s, mid, preferred_element_type=F32)
            + jnp.dot(s, lo, preferred_element_type=F32))


def _mm_tn_sel(x, sel):
    s = sel.astype(BF16)
    dn = (((0,), (0,)), ((), ()))
    hi, mid, lo = _split3(x)
    return (lax.dot_general(hi, s, dn, preferred_element_type=F32)
            + lax.dot_general(mid, s, dn, preferred_element_type=F32)
            + lax.dot_general(lo, s, dn, preferred_element_type=F32))


def _mm_xsel(x, sel):
    s = sel.astype(BF16)
    hi, mid, lo = _split3(x)
    return (jnp.dot(hi, s, preferred_element_type=F32) + jnp.dot(mid, s, preferred_element_type=F32)
            + jnp.dot(lo, s, preferred_element_type=F32))


def _sigmoid(x):
    return 1.0 / (1.0 + jnp.exp(-x))


def _silu(x):
    return x * _sigmoid(x)


def _softplus(x):
    return jnp.maximum(x, 0.0) + jnp.log(1.0 + jnp.exp(-jnp.abs(x)))


def _rms(x, eps):
    return x * lax.rsqrt(jnp.mean(x * x, axis=-1, keepdims=True) + eps)


def _tri(n, strict):
    r = lax.broadcasted_iota(jnp.int32, (n, n), 0)
    c = lax.broadcasted_iota(jnp.int32, (n, n), 1)
    return (r > c) if strict else (r >= c)


ROW_TILE = 512
FFN_ROW_TILE = 512


def _mod_specs(mc, tm, idxs):
    l, rps, row0 = mc['l'], mc['rps'], mc['row0']
    specs = []
    for idx in idxs:
        if rps >= tm:
            specs.append(pl.BlockSpec(
                (None, None, SUBLANES, D_MODEL),
                lambda i, j, idx=idx: (l, idx, (row0 + i * tm // rps) // SUBLANES, 0)))
        else:
            assert row0 == 0
            specs.append(pl.BlockSpec((None, None, tm // rps, D_MODEL), lambda i, j, idx=idx: (l, idx, i, 0)))
    return specs


def _mod_val(ref, mc, tm):
    rps = mc['rps']
    if rps >= tm:
        r = (mc['row0'] + pl.program_id(0) * tm // rps) % SUBLANES
        return ref[pl.ds(r, 1), :]
    nseq = tm // rps
    row = lax.broadcasted_iota(jnp.int32, (tm, nseq), 0)
    first = lax.broadcasted_iota(jnp.int32, (tm, nseq), 1) * rps
    return _mm_sel((row >= first) & (row < first + rps), ref[...])


def _ada_kernel(c_ref, w_ref, b_ref, o_ref):
    x = _silu(c_ref[...]).astype(BF16)
    o_ref[...] = jnp.dot(x, w_ref[...].astype(BF16), preferred_element_type=F32) + b_ref[...]


def _ada(c, ada_w, ada_b):
    rows = c.shape[0]
    n = ada_w.shape[-1]
    tn = 1024
    per_mod = D_MODEL // tn
    return pl.pallas_call(
        _ada_kernel,
        grid=(DEPTH, n // tn),
        in_specs=[pl.BlockSpec((rows, D_MODEL), lambda l, j: (0, 0)),
                  pl.BlockSpec((None, D_MODEL, tn), lambda l, j: (l, 0, j)),
                  pl.BlockSpec((None, 1, tn), lambda l, j: (l, 0, j))],
        out_specs=pl.BlockSpec((None, None, rows, tn), lambda l, j: (l, j // per_mod, 0, j % per_mod)),
        out_shape=jax.ShapeDtypeStruct((DEPTH, N_MOD, rows, D_MODEL), F32),
        compiler_params=pltpu.CompilerParams(dimension_semantics=("arbitrary", "arbitrary"),
                                             vmem_limit_bytes=VMEM_LIMIT),
        name="ada",
    )(c, ada_w, ada_b.reshape(DEPTH, 1, n))


def _ffn_kernel(*refs, mc, tm, nj, final, emit):
    refs = list(refs)
    h_ref, sh_ref, sc_ref, g_ref, wg_ref, wu_ref, wd_ref = refs[:7]
    fin_ref = refs[7] if final else None
    o_ref = refs[7 + final]
    xn_ref = refs[-1]
    j = pl.program_id(1)

    @pl.when(j == 0)
    def _():
        xn = _rms(h_ref[...], 1e-6) * (1.0 + _mod_val(sc_ref, mc, tm)) + _mod_val(sh_ref, mc, tm)
        xn_ref[...] = xn.astype(BF16)
        o_ref[...] = jnp.zeros_like(o_ref)

    wg, wu, wd = wg_ref[...], wu_ref[...], wd_ref[...]
    if emit:
        wg, wu, wd = wg.astype(BF16), wu.astype(BF16), wd.astype(BF16)
        refs[8 + final][...] = wg
        refs[9 + final][...] = wu
        refs[10 + final][...] = wd
    for r in range(tm // ROW_TILE):
        rows = slice(r * ROW_TILE, (r + 1) * ROW_TILE)
        xn = xn_ref[rows, :]
        gate = jnp.dot(xn, wg, preferred_element_type=F32)
        up = jnp.dot(xn, wu, preferred_element_type=F32)
        act = (_silu(gate) * up).astype(BF16)
        o_ref[rows, :] += jnp.dot(act, wd, preferred_element_type=F32)

    @pl.when(j == nj - 1)
    def _():
        out = h_ref[...] + 0.5 * (1.0 + _mod_val(g_ref, mc, tm)) * o_ref[...]
        if final:
            out = _rms(out, 1e-6) * fin_ref[...]
        o_ref[...] = out


def _ffn(h, mod, mc, mod_base, w, k, final_g=None):
    m = h.shape[0]
    l = mc['l']
    emit = w[0].ndim == 4
    tm = min(FFN_ROW_TILE, m)
    tf = 256 if emit else 512
    nj = D_FF // tf
    final = final_g is not None
    in_specs = [pl.BlockSpec((tm, D_MODEL), lambda i, j: (i, 0))]
    in_specs += _mod_specs(mc, tm, (mod_base, mod_base + 1, mod_base + 2))
    col = pl.BlockSpec((D_MODEL, tf), lambda i, j: (0, j))
    row = pl.BlockSpec((tf, D_MODEL), lambda i, j: (j, 0))
    if emit:
        assert m == tm
        in_specs += [pl.BlockSpec((None, None, D_MODEL, tf), lambda i, j: (l, k, 0, j)),
                     pl.BlockSpec((None, None, D_MODEL, tf), lambda i, j: (l, k, 0, j)),
                     pl.BlockSpec((None, None, tf, D_MODEL), lambda i, j: (l, k, j, 0))]
    else:
        in_specs += [col, col, row]
    args = [h, mod, mod, mod, *w]
    if final:
        in_specs.append(pl.BlockSpec((1, D_MODEL), lambda i, j: (0, 0)))
        args.append(final_g.reshape(1, D_MODEL))
    out_specs = [pl.BlockSpec((tm, D_MODEL), lambda i, j: (i, 0))]
    out_shape = [jax.ShapeDtypeStruct((m, D_MODEL), F32)]
    if emit:
        out_specs += [col, col, row]
        out_shape += [jax.ShapeDtypeStruct((D_MODEL, D_FF), BF16), jax.ShapeDtypeStruct((D_MODEL, D_FF), BF16),
                      jax.ShapeDtypeStruct((D_FF, D_MODEL), BF16)]
    out = pl.pallas_call(
        functools.partial(_ffn_kernel, mc=mc, tm=tm, nj=nj, final=final, emit=emit),
        grid=(m // tm, nj),
        in_specs=in_specs, out_specs=out_specs, out_shape=out_shape,
        scratch_shapes=[pltpu.VMEM((tm, D_MODEL), BF16)],
        compiler_params=pltpu.CompilerParams(dimension_semantics=("arbitrary", "arbitrary"),
                                             vmem_limit_bytes=VMEM_LIMIT),
        name="ffn",
    )(*args)
    return (out[0], tuple(out[1:])) if emit else out[0]


def _inproj_kernel(h_ref, sh_ref, sc_ref, w_ref, o_ref, xn_ref, *, mc, tm):
    @pl.when(pl.program_id(1) == 0)
    def _():
        xn = _rms(h_ref[...], 1e-6) * (1.0 + _mod_val(sc_ref, mc, tm)) + _mod_val(sh_ref, mc, tm)
        xn_ref[...] = xn.astype(BF16)

    o_ref[...] = jnp.dot(xn_ref[...], w_ref[...], preferred_element_type=F32)


def _inproj(h, mod, mc, w, li):
    m = h.shape[0]
    n = w.shape[2]
    tm = min(ROW_TILE, m)
    tn = n // 3
    if n % (6 * LANES) == 0 and m % (2 * ROW_TILE) == 0:
        tm, tn = 2 * ROW_TILE, n // 6
    in_specs = [pl.BlockSpec((tm, D_MODEL), lambda i, j: (i, 0))]
    in_specs += _mod_specs(mc, tm, (3, 4))
    in_specs += [pl.BlockSpec((None, D_MODEL, tn), lambda i, j: (li, 0, j))]
    return pl.pallas_call(
        functools.partial(_inproj_kernel, mc=mc, tm=tm),
        grid=(m // tm, n // tn),
        in_specs=in_specs,
        out_specs=pl.BlockSpec((tm, tn), lambda i, j: (i, j)),
        out_shape=jax.ShapeDtypeStruct((m, n), F32),
        scratch_shapes=[pltpu.VMEM((tm, D_MODEL), BF16)],
        compiler_params=pltpu.CompilerParams(dimension_semantics=("arbitrary", "arbitrary"),
                                             vmem_limit_bytes=VMEM_LIMIT),
        name="inproj",
    )(h, mod, mod, w)


def _outproj_kernel(y_ref, w_ref, h_ref, g_ref, o_ref, *, mc, tm):
    mix = jnp.dot(y_ref[...], w_ref[...], preferred_element_type=F32)
    o_ref[...] = h_ref[...] + (1.0 + _mod_val(g_ref, mc, tm)) * mix


def _outproj(y, w, li, h, mod, mc):
    m = h.shape[0]
    tm = min(ROW_TILE, m)
    in_specs = [pl.BlockSpec((tm, D_MODEL), lambda i, j: (i, 0)),
                pl.BlockSpec((None, D_MODEL, D_MODEL), lambda i, j: (li, 0, 0)),
                pl.BlockSpec((tm, D_MODEL), lambda i, j: (i, 0))]
    in_specs += _mod_specs(mc, tm, (5,))
    return pl.pallas_call(
        functools.partial(_outproj_kernel, mc=mc, tm=tm),
        grid=(m // tm, 1),
        in_specs=in_specs,
        out_specs=pl.BlockSpec((tm, D_MODEL), lambda i, j: (i, 0)),
        out_shape=jax.ShapeDtypeStruct((m, D_MODEL), F32),
        compiler_params=pltpu.CompilerParams(dimension_semantics=("arbitrary", "arbitrary"),
                                             vmem_limit_bytes=VMEM_LIMIT),
        name="outproj",
    )(y, w, h, mod)


def _state_out(stacked, li, nseq, g, tail):
    zeros = (0,) * len(tail)
    if stacked is None:
        return (jax.ShapeDtypeStruct((nseq,) + tail, F32),
                pl.BlockSpec((g,) + tail, lambda b, c: (b,) + zeros))
    return (jax.ShapeDtypeStruct(stacked.shape, F32),
            pl.BlockSpec((None, g) + tail, lambda b, c: (li, b) + zeros))


EV_W0, EV_A0, EV_KK, EV_KA, EV_RK, EV_GNW, EV_GNB, EV_D, EV_NW = range(9)
EV_VEC_ROWS = 16


def _even_kernel(*refs, t, nchunks, lv, has_init, hg):
    if has_init:
        (z_ref, shift0_ref, wkv0_ref, conv0_ref, ssm0_ref, *rest) = refs
    else:
        z_ref, *rest = refs
    (mu_ref, vec_ref, wa_ref, gup_ref, cw_ref, cb_ref, dtb_ref, alog_ref, exp_ref,
     y_ref, shift_o, wkv_o, conv_o, ssm_o,
     carry_ref, wkv_ref, ext_ref, ssm_ref,
     r_s, kp_s, v_s, kk_s, a_s, c_s, ld_s, g_s, y_s) = rest
    ci = pl.program_id(1)
    masked = lv < t
    nlog = int(math.log2(t))

    @pl.when(ci == 0)
    def _():
        if has_init:
            carry_ref[:, 0:RW_COLS] = shift0_ref[0]
            carry_ref[:, RW_COLS:RW_PAD] = jnp.zeros((1, RW_PAD - RW_COLS), F32)
            wkv_ref[...] = wkv0_ref[0]
            ext_ref[0:SUBLANES - CONV_W + 1, :] = jnp.zeros((SUBLANES - CONV_W + 1, MB_CONV_CH), F32)
            ext_ref[SUBLANES - CONV_W + 1:SUBLANES, :] = conv0_ref[0]
            for h in range(H_B):
                ssm_ref[:, h * HEAD_B:(h + 1) * HEAD_B] = ssm0_ref[0, h]
        else:
            carry_ref[...] = jnp.zeros_like(carry_ref)
            wkv_ref[...] = jnp.zeros_like(wkv_ref)
            ext_ref[0:SUBLANES, :] = jnp.zeros((SUBLANES, MB_CONV_CH), F32)
            ssm_ref[...] = jnp.zeros_like(ssm_ref)

    row = lax.broadcasted_iota(jnp.int32, (t, 1), 0)
    valid = row < lv
    strict = _tri(t, True)
    incl = _tri(t, False)

    def vec(i, lo=0, hi=D_MIX):
        return vec_ref[i:i + 1, lo:hi]

    def shifted(a, b):
        cur = z_ref[0, :, a:b]
        prev = jnp.where(row == 0, carry_ref[:, a:b], pltpu.roll(cur, 1, axis=0))
        return cur + mu_ref[:, a:b] * (prev - cur)

    r_s[...] = shifted(0, D_MIX)
    k = shifted(D_MIX, 2 * D_MIX)
    v_s[...] = shifted(2 * D_MIX, 3 * D_MIX)
    lo = shifted(3 * D_MIX, 3 * D_MIX + LANES)
    lane = lax.broadcasted_iota(jnp.int32, (t, LANES), 1)
    wa = _mm(jnp.where(lane < RW_LORA_W, jnp.tanh(lo), lo), wa_ref[...])
    w_log = -_softplus(-(vec(EV_W0) + wa[:, 0:D_MIX])) - 0.5
    ld = -jnp.exp(w_log)
    a = _sigmoid(vec(EV_A0) + wa[:, D_MIX:2 * D_MIX])
    g_s[...] = _mm(_sigmoid(shifted(3 * D_MIX + LANES, RW_PAD)), gup_ref[...])
    kp = k * (1.0 + (a - 1.0) * vec(EV_KA))
    if masked:
        ld = jnp.where(valid, ld, 0.0)
        kp = jnp.where(valid, kp, 0.0)
        a = jnp.where(valid, a, 0.0)
    kk_s[...] = k * vec(EV_KK)
    kp_s[...] = kp
    a_s[...] = a
    ld_s[...] = ld
    c_s[...] = _mm_sel(incl, ld)
    last_row = lv - 1
    carry_ref[...] = z_ref[0, last_row:last_row + 1, 0:RW_PAD]

    for h0 in range(0, H_A, hg):
        heads = range(h0, h0 + hg)
        sls = [slice(h * HEAD_A, (h + 1) * HEAD_A) for h in heads]
        rs = [r_s[:, sl] for sl in sls]
        kps = [kp_s[:, sl] for sl in sls]
        vs = [v_s[:, sl] for sl in sls]
        cs = [c_s[:, sl] for sl in sls]
        kks, bs = [], []
        for sl in sls:
            kkr = kk_s[:, sl]
            nrm = jnp.sqrt(jnp.sum(kkr * kkr, axis=-1, keepdims=True))
            kks.append(kkr / jnp.maximum(nrm, 1e-12))
            bs.append(kks[-1] * a_s[:, sl])
        ecs = [jnp.exp(c) for c in cs]
        ecis = [jnp.exp(-c) for c in cs]
        lhss = [jnp.concatenate([kk * jnp.exp(c - ld_s[:, sl]), r * ec], axis=0)
                for kk, c, sl, r, ec in zip(kks, cs, sls, rs, ecs)]
        s0s = [wkv_ref[h] for h in heads]
        a1s = [_mm_nt(lhs, kp * eci) for lhs, kp, eci in zip(lhss, kps, ecis)]
        a2s = [_mm_nt(lhs, b * eci) for lhs, b, eci in zip(lhss, bs, ecis)]
        pms = [_mm_nt(lhs, s0) for lhs, s0 in zip(lhss, s0s)]
        us = [pm[0:t] + _mm(jnp.where(strict, a1[0:t], 0.0), v) for pm, a1, v in zip(pms, a1s, vs)]
        ps = [jnp.where(strict, -a2[0:t], 0.0) for a2 in a2s]
        us = [u + _mm(p, u) for u, p in zip(us, ps)]
        for _ in range(nlog - 1):
            ps = [_mm(p, p) for p in ps]
            us = [u + _mm(p, u) for u, p in zip(us, ps)]
        ys = [pm[t:2 * t] + _mm(jnp.where(incl, a1[t:2 * t], 0.0), v) - _mm(jnp.where(incl, a2[t:2 * t], 0.0), u)
              for pm, a1, a2, v, u in zip(pms, a1s, a2s, vs, us)]
        for i, h in enumerate(heads):
            c, kp, b, v, u = cs[i], kps[i], bs[i], vs[i], us[i]
            dec_end = jnp.exp(c[t - 1:t] - c)
            vu = jnp.concatenate([v, u], axis=0)
            kb = jnp.concatenate([kp * dec_end, -(b * dec_end)], axis=0)
            wkv_ref[h] = s0s[i] * ecs[i][t - 1:t] + _mm_tn(vu, kb)
        for i, sl in enumerate(sls):
            y, r, kp, v = ys[i], rs[i], kps[i], vs[i]
            mean = jnp.mean(y, axis=-1, keepdims=True)
            yc = y - mean
            var = jnp.mean(yc * yc, axis=-1, keepdims=True)
            yn = (yc * lax.rsqrt(var + RW_GN_EPS) * vec(EV_GNW, sl.start, sl.stop)
                  + vec(EV_GNB, sl.start, sl.stop))
            bonus = jnp.sum(r * kp * vec(EV_RK, sl.start, sl.stop), axis=-1, keepdims=True) * v
            y_s[:, sl] = (yn + bonus) * g_s[:, sl]

    zg = z_ref[0, :, ZM:ZM + D_MIX]
    ext_ref[SUBLANES:SUBLANES + t, :] = z_ref[0, :, ZM + D_MIX:ZM + D_MIX + MB_CONV_CH]
    conv = cb_ref[...]
    for w in range(CONV_W):
        off = SUBLANES - (CONV_W - 1) + w
        conv = conv + cw_ref[w:w + 1, :] * ext_ref[off:off + t, :]
    new_conv = ext_ref[SUBLANES + lv - (CONV_W - 1):SUBLANES + lv, :]
    ext_ref[SUBLANES - (CONV_W - 1):SUBLANES, :] = new_conv
    xbc = _silu(conv)
    xs = xbc[:, 0:D_MIX]
    dt = _softplus(z_ref[0, :, ZM + D_MIX + MB_CONV_CH:ZM + MB_PAD] + dtb_ref[...])
    logd = dt * (-jnp.exp(alog_ref[...]))
    if masked:
        logd = jnp.where(valid, logd, 0.0)
    c = _mm_sel(incl, logd)
    upper = lax.broadcasted_iota(jnp.int32, (t, t), 0) <= lax.broadcasted_iota(jnp.int32, (t, t), 1)
    c_t = _mm_tn_sel(logd, upper)
    wide = _mm_xsel(jnp.concatenate([dt, c, c[t - 1:t] - c], axis=0), exp_ref[...])
    xdt = xs * wide[0:t]
    if masked:
        xdt = jnp.where(valid, xdt, 0.0)
    ec_w = jnp.exp(wide[t:2 * t])
    dec_end_w = jnp.exp(wide[2 * t:3 * t])
    ecl_w = ec_w[t - 1:t]
    gw = (H_B // N_GROUPS) * HEAD_B
    hpg = H_B // N_GROUPS
    bgs = [xbc[:, D_MIX + g * D_STATE:D_MIX + (g + 1) * D_STATE] for g in range(N_GROUPS)]
    cgs = [xbc[:, D_MIX + (N_GROUPS + g) * D_STATE:D_MIX + (N_GROUPS + g + 1) * D_STATE]
           for g in range(N_GROUPS)]
    scores = [_mm_nt(cg, bg) for cg, bg in zip(cgs, bgs)]
    s_prevs = [ssm_ref[:, g * gw:(g + 1) * gw] for g in range(N_GROUPS)]
    inters = [_mm(cg, sp) for cg, sp in zip(cgs, s_prevs)]
    segs = [jnp.where(incl, jnp.exp(jnp.where(incl, c[:, h:h + 1] - c_t[h:h + 1, :], 0.0)), 0.0)
            for h in range(H_B)]
    intra = [_mm(scores[h // hpg] * segs[h], xdt[:, h * HEAD_B:(h + 1) * HEAD_B]) for h in range(H_B)]
    for g in range(N_GROUPS):
        gs = slice(g * gw, (g + 1) * gw)
        ssm_ref[:, gs] = s_prevs[g] * ecl_w[:, gs] + _mm_tn(bgs[g], xdt[:, gs] * dec_end_w[:, gs])
    o = jnp.concatenate(intra, axis=1) + jnp.concatenate(inters, axis=1) * ec_w
    yb = (o + vec(EV_D) * xs) * _silu(zg)
    y_s[:, D_MIX:2 * D_MIX] = _rms(yb, 1e-5) * vec(EV_NW)
    y_ref[0] = y_s[...].astype(BF16)

    @pl.when(ci == nchunks - 1)
    def _():
        shift_o[0] = carry_ref[...]
        wkv_o[0] = wkv_ref[...]
        conv_o[0] = new_conv
        for h in range(H_B):
            ssm_o[0, h] = ssm_ref[:, h * HEAD_B:(h + 1) * HEAD_B]


def _even_mixer(z, states, li, p, *, t, lv):
    nseq, length, _ = z.shape
    nchunks = length // t
    has_init = states is not None
    assert lv == t or nchunks == 1
    full = lambda *shape: pl.BlockSpec(shape, lambda b, c: (0,) * len(shape))
    in_specs = [pl.BlockSpec((1, t, EV_PAD), lambda b, c: (b, c, 0))]
    args = [z]
    if has_init:
        shift0, wkv0, conv0, ssm0 = states
        in_specs += [pl.BlockSpec((None, 1, 1, RW_COLS), lambda b, c: (li, b, 0, 0)),
                     pl.BlockSpec((None, 1, H_A, HEAD_A, HEAD_A), lambda b, c: (li, b, 0, 0, 0)),
                     pl.BlockSpec((None, 1, CONV_W - 1, MB_CONV_CH), lambda b, c: (li, b, 0, 0)),
                     pl.BlockSpec((None, 1, H_B, D_STATE, HEAD_B), lambda b, c: (li, b, 0, 0, 0))]
        args += [shift0.reshape(-1, nseq, 1, RW_COLS), wkv0, conv0, ssm0]
    in_specs += [full(1, RW_PAD), full(EV_VEC_ROWS, D_MIX), full(LANES, 2 * D_MIX),
                 full(RW_PAD - 3 * D_MIX - LANES, D_MIX), full(CONV_W, MB_CONV_CH),
                 full(1, MB_CONV_CH), full(1, LANES), full(1, LANES), full(LANES, D_MIX)]
    args += [p['mu'], p['vec'], p['wa'], p['gup'], p['conv_w'], p['conv_b'], p['dt_bias'],
             p['a_log'], p['expand']]
    wkv_shape, wkv_spec = _state_out(states and states[1], li, nseq, (H_A, HEAD_A, HEAD_A))
    ssm_shape, ssm_spec = _state_out(states and states[3], li, nseq, (H_B, D_STATE, HEAD_B))
    out_shape = (jax.ShapeDtypeStruct((nseq, length, D_MODEL), BF16),
                 jax.ShapeDtypeStruct((nseq, 1, RW_PAD), F32), wkv_shape,
                 jax.ShapeDtypeStruct((nseq, CONV_W - 1, MB_CONV_CH), F32), ssm_shape)
    out_specs = (pl.BlockSpec((1, t, D_MODEL), lambda b, c: (b, c, 0)),
                 pl.BlockSpec((1, 1, RW_PAD), lambda b, c: (b, 0, 0)), wkv_spec,
                 pl.BlockSpec((1, CONV_W - 1, MB_CONV_CH), lambda b, c: (b, 0, 0)), ssm_spec)
    wide = lambda: pltpu.VMEM((t, D_MIX), F32)
    scratch = [pltpu.VMEM((1, RW_PAD), F32), pltpu.VMEM((H_A, HEAD_A, HEAD_A), F32),
               pltpu.VMEM((t + SUBLANES, MB_CONV_CH), F32), pltpu.VMEM((D_STATE, H_B * HEAD_B), F32)]
    scratch += [wide() for _ in range(8)] + [pltpu.VMEM((t, D_MODEL), F32)]
    y, shift, wkv, conv, ssm = pl.pallas_call(
        functools.partial(_even_kernel, t=t, nchunks=nchunks, lv=lv, has_init=has_init,
                          hg=H_A),
        grid=(nseq, nchunks),
        in_specs=in_specs, out_specs=out_specs, out_shape=out_shape, scratch_shapes=scratch,
        input_output_aliases={2: 2, 4: 4} if has_init else {},
        compiler_params=pltpu.CompilerParams(dimension_semantics=("arbitrary", "arbitrary"),
                                             vmem_limit_bytes=VMEM_LIMIT),
        name="even_mixer",
    )(*args)
    return y, (shift[:, 0, :RW_COLS], wkv, conv, ssm)


def _seq_masks(g, t):
    n = g * t
    r = lax.broadcasted_iota(jnp.int32, (n, n), 0)
    c = lax.broadcasted_iota(jnp.int32, (n, n), 1)
    if g == 1:
        return r > c, r >= c, r <= c, c == n - 1
    same = (r // t) == (c // t)
    return same & (r > c), same & (r >= c), same & (r <= c), c == (r // t) * t + (t - 1)


def _even_kernel_g(*refs, t, g, nchunks, lv, has_init):
    if has_init:
        z_ref, shift0_ref, wkv0_ref, conv0_ref, ssm0_ref, *rest = refs
    else:
        z_ref, *rest = refs
    (mu_ref, vec_ref, wa_ref, gup_ref, cw_ref, cb_ref, dtb_ref, alog_ref, exp_ref,
     y_ref, shift_o, wkv_o, conv_o, ssm_o,
     carry_ref, wkv_ref, ext_ref, ssm_ref,
     r_s, kp_s, v_s, kk_s, a_s, c_s, ld_s, g_s, cl_s, y_s) = rest
    n = g * t
    ci = pl.program_id(1)
    masked = lv < t
    nlog = int(math.log2(t))
    seqs = range(g)
    hist = SUBLANES - (CONV_W - 1)
    direct = has_init and nchunks == 1
    wkv_src, wkv_dst = (wkv0_ref, wkv_o) if direct else (wkv_ref, wkv_ref)
    ssm_src, ssm_dst = (ssm0_ref, ssm_o) if direct else (ssm_ref, ssm_ref)

    @pl.when(ci == 0)
    def _():
        if has_init:
            carry_ref[:, :, 0:RW_COLS] = shift0_ref[...]
            carry_ref[:, :, RW_COLS:RW_PAD] = jnp.zeros((g, 1, RW_PAD - RW_COLS), F32)
            ext_ref[:, 0:hist, :] = jnp.zeros((g, hist, MB_CONV_CH), F32)
            ext_ref[:, hist:SUBLANES, :] = conv0_ref[...]
            if not direct:
                wkv_ref[...] = wkv0_ref[...]
                ssm_ref[...] = ssm0_ref[...]
        else:
            carry_ref[...] = jnp.zeros_like(carry_ref)
            wkv_ref[...] = jnp.zeros_like(wkv_ref)
            ext_ref[:, 0:SUBLANES, :] = jnp.zeros((g, SUBLANES, MB_CONV_CH), F32)
            ssm_ref[...] = jnp.zeros_like(ssm_ref)

    tok = lax.broadcasted_iota(jnp.int32, (n, 1), 0) % t
    valid = tok < lv
    strict, incl, upper, lastsel = _seq_masks(g, t)

    def vec(i, lo=0, hi=D_MIX):
        return vec_ref[i:i + 1, lo:hi]

    def rows_of(x, s):
        return x[s * t:(s + 1) * t]

    def per_seq(fn):
        parts = [fn(s) for s in seqs]
        return parts[0] if g == 1 else jnp.concatenate(parts, axis=0)

    def shifted(a, b):
        cur = z_ref[:, :, a:b].reshape(n, b - a)
        carry = per_seq(lambda s: jnp.broadcast_to(carry_ref[s, :, a:b], (t, b - a)))
        prev = jnp.where(tok == 0, carry, pltpu.roll(cur, 1, axis=0))
        return cur + mu_ref[:, a:b] * (prev - cur)

    r_s[...] = shifted(0, D_MIX)
    k = shifted(D_MIX, 2 * D_MIX)
    v_s[...] = shifted(2 * D_MIX, 3 * D_MIX)
    lo = shifted(3 * D_MIX, 3 * D_MIX + LANES)
    lane = lax.broadcasted_iota(jnp.int32, (n, LANES), 1)
    wa = _mm(jnp.where(lane < RW_LORA_W, jnp.tanh(lo), lo), wa_ref[...])
    w_log = -_softplus(-(vec(EV_W0) + wa[:, 0:D_MIX])) - 0.5
    ld = -jnp.exp(w_log)
    a = _sigmoid(vec(EV_A0) + wa[:, D_MIX:2 * D_MIX])
    g_s[...] = _mm(_sigmoid(shifted(3 * D_MIX + LANES, RW_PAD)), gup_ref[...])
    kp = k * (1.0 + (a - 1.0) * vec(EV_KA))
    if masked:
        ld = jnp.where(valid, ld, 0.0)
        kp = jnp.where(valid, kp, 0.0)
        a = jnp.where(valid, a, 0.0)
    kk_s[...] = k * vec(EV_KK)
    kp_s[...] = kp
    a_s[...] = a
    ld_s[...] = ld
    c = _mm_sel(incl, ld)
    c_s[...] = c
    cl_s[...] = _mm_sel(lastsel, c)
    carry_ref[...] = z_ref[:, lv - 1:lv, 0:RW_PAD]

    heads = range(H_A)
    sls = [slice(h * HEAD_A, (h + 1) * HEAD_A) for h in heads]
    rs = [r_s[:, sl] for sl in sls]
    kps = [kp_s[:, sl] for sl in sls]
    vs = [v_s[:, sl] for sl in sls]
    cs = [c_s[:, sl] for sl in sls]
    cls = [cl_s[:, sl] for sl in sls]
    kks, bs = [], []
    for sl in sls:
        kkr = kk_s[:, sl]
        nrm = jnp.sqrt(jnp.sum(kkr * kkr, axis=-1, keepdims=True))
        kks.append(kkr / jnp.maximum(nrm, 1e-12))
        bs.append(kks[-1] * a_s[:, sl])
    ecis = [jnp.exp(-c) for c in cs]
    kts = [kk * jnp.exp(c - ld_s[:, sl]) for kk, c, sl in zip(kks, cs, sls)]
    rts = [r * jnp.exp(c) for r, c in zip(rs, cs)]
    lhss = [jnp.concatenate([kt, rt], axis=0) for kt, rt in zip(kts, rts)]
    a1s = [_mm_nt(lhs, kp * eci) for lhs, kp, eci in zip(lhss, kps, ecis)]
    a2s = [_mm_nt(lhs, b * eci) for lhs, b, eci in zip(lhss, bs, ecis)]
    pms = [[_mm_nt(jnp.concatenate([rows_of(kt, s), rows_of(rt, s)], axis=0), wkv_src[s, h]) for s in seqs]
           for h, kt, rt in zip(heads, kts, rts)]
    pmk = [per_seq(lambda s: pm[s][0:t]) for pm in pms]
    pmr = [per_seq(lambda s: pm[s][t:2 * t]) for pm in pms]
    us = [pk + _mm(jnp.where(strict, a1[0:n], 0.0), v) for pk, a1, v in zip(pmk, a1s, vs)]
    ps = [jnp.where(strict, -a2[0:n], 0.0) for a2 in a2s]
    us = [u + _mm(p, u) for u, p in zip(us, ps)]
    for _ in range(nlog - 1):
        ps = [_mm(p, p) for p in ps]
        us = [u + _mm(p, u) for u, p in zip(us, ps)]
    ys = [pr + _mm(jnp.where(incl, a1[n:2 * n], 0.0), v) - _mm(jnp.where(incl, a2[n:2 * n], 0.0), u)
          for pr, a1, a2, v, u in zip(pmr, a1s, a2s, vs, us)]
    for h in heads:
        dec_end = jnp.exp(cls[h] - cs[h])
        kbar = kps[h] * dec_end
        bbar = bs[h] * dec_end
        for s in seqs:
            vu = jnp.concatenate([rows_of(vs[h], s), rows_of(us[h], s)], axis=0)
            kb = jnp.concatenate([rows_of(kbar, s), -rows_of(bbar, s)], axis=0)
            wkv_dst[s, h] = wkv_src[s, h] * jnp.exp(cls[h][s * t:s * t + 1]) + _mm_tn(vu, kb)
    for h, sl in zip(heads, sls):
        y, r, kp, v = ys[h], rs[h], kps[h], vs[h]
        mean = jnp.mean(y, axis=-1, keepdims=True)
        yc = y - mean
        var = jnp.mean(yc * yc, axis=-1, keepdims=True)
        yn = (yc * lax.rsqrt(var + RW_GN_EPS) * vec(EV_GNW, sl.start, sl.stop)
              + vec(EV_GNB, sl.start, sl.stop))
        bonus = jnp.sum(r * kp * vec(EV_RK, sl.start, sl.stop), axis=-1, keepdims=True) * v
        y_s[:, sl] = (yn + bonus) * g_s[:, sl]

    zg = z_ref[:, :, ZM:ZM + D_MIX].reshape(n, D_MIX)
    ext_ref[:, SUBLANES:SUBLANES + t, :] = z_ref[:, :, ZM + D_MIX:ZM + D_MIX + MB_CONV_CH]
    conv = cb_ref[...]
    for w in range(CONV_W):
        conv = conv + cw_ref[w:w + 1, :] * ext_ref[:, hist + w:hist + w + t, :].reshape(n, MB_CONV_CH)
    new_conv = ext_ref[:, SUBLANES + lv - (CONV_W - 1):SUBLANES + lv, :]
    ext_ref[:, hist:SUBLANES, :] = new_conv
    xbc = _silu(conv)
    xs = xbc[:, 0:D_MIX]
    dt = _softplus(z_ref[:, :, ZM + D_MIX + MB_CONV_CH:ZM + MB_PAD].reshape(n, LANES) + dtb_ref[...])
    logd = dt * (-jnp.exp(alog_ref[...]))
    if masked:
        logd = jnp.where(valid, logd, 0.0)
    c = _mm_sel(incl, logd)
    c_t = _mm_tn_sel(logd, upper)
    c_end = _mm_sel(lastsel, c)
    wide = _mm_xsel(jnp.concatenate([dt, c, c_end - c], axis=0), exp_ref[...])
    xdt = xs * wide[0:n]
    if masked:
        xdt = jnp.where(valid, xdt, 0.0)
    ec_w = jnp.exp(wide[n:2 * n])
    xdec = xdt * jnp.exp(wide[2 * n:3 * n])
    hpg = H_B // N_GROUPS
    bgs = [xbc[:, D_MIX + q * D_STATE:D_MIX + (q + 1) * D_STATE] for q in range(N_GROUPS)]
    cgs = [xbc[:, D_MIX + (N_GROUPS + q) * D_STATE:D_MIX + (N_GROUPS + q + 1) * D_STATE]
           for q in range(N_GROUPS)]
    scores = [_mm_nt(cg, bg) for cg, bg in zip(cgs, bgs)]
    segs = [jnp.where(incl, jnp.exp(jnp.where(incl, c[:, h:h + 1] - c_t[h:h + 1, :], 0.0)), 0.0)
            for h in range(H_B)]
    hsl = [slice(h * HEAD_B, (h + 1) * HEAD_B) for h in range(H_B)]
    intra = [_mm(scores[h // hpg] * segs[h], xdt[:, hsl[h]]) for h in range(H_B)]
    inter = [per_seq(lambda s: _mm_nt(rows_of(cgs[h // hpg], s), ssm_src[s, h])) for h in range(H_B)]
    for h in range(H_B):
        for s in seqs:
            last = s * t + t - 1
            ssm_dst[s, h] = (ssm_src[s, h] * jnp.exp(c[last:last + 1, h:h + 1])
                             + _mm_tn(rows_of(xdec[:, hsl[h]], s), rows_of(bgs[h // hpg], s)))
    o = jnp.concatenate(intra, axis=1) + jnp.concatenate(inter, axis=1) * ec_w
    yb = (o + vec(EV_D) * xs) * _silu(zg)
    y_s[:, D_MIX:2 * D_MIX] = _rms(yb, 1e-5) * vec(EV_NW)
    y_ref[...] = y_s[...].astype(BF16).reshape(g, t, D_MODEL)

    @pl.when(ci == nchunks - 1)
    def _():
        shift_o[...] = carry_ref[...]
        conv_o[...] = new_conv
        if not direct:
            wkv_o[...] = wkv_ref[...]
            ssm_o[...] = ssm_ref[...]


def _even_mixer_g(z, states, li, p, *, t, g, lv):
    nseq, length, _ = z.shape
    nchunks = length // t
    has_init = states is not None
    assert (lv == t or nchunks == 1) and nseq % g == 0
    full = lambda *shape: pl.BlockSpec(shape, lambda b, c: (0,) * len(shape))
    in_specs = [pl.BlockSpec((g, t, EV_PAD), lambda b, c: (b, c, 0))]
    args = [z]
    ssm_t = None
    if has_init:
        shift0, wkv0, conv0, ssm0 = states
        ssm_t = jnp.swapaxes(ssm0, 3, 4)
        in_specs += [pl.BlockSpec((None, g, 1, RW_COLS), lambda b, c: (li, b, 0, 0)),
                     pl.BlockSpec((None, g, H_A, HEAD_A, HEAD_A), lambda b, c: (li, b, 0, 0, 0)),
                     pl.BlockSpec((None, g, CONV_W - 1, MB_CONV_CH), lambda b, c: (li, b, 0, 0)),
                     pl.BlockSpec((None, g, H_B, HEAD_B, D_STATE), lambda b, c: (li, b, 0, 0, 0))]
        args += [shift0.reshape(-1, nseq, 1, RW_COLS), wkv0, conv0, ssm_t]
    in_specs += [full(1, RW_PAD), full(EV_VEC_ROWS, D_MIX), full(LANES, 2 * D_MIX),
                 full(RW_PAD - 3 * D_MIX - LANES, D_MIX), full(CONV_W, MB_CONV_CH),
                 full(1, MB_CONV_CH), full(1, LANES), full(1, LANES), full(LANES, D_MIX)]
    args += [p['mu'], p['vec'], p['wa'], p['gup'], p['conv_w'], p['conv_b'], p['dt_bias'],
             p['a_log'], p['expand']]
    wkv_shape, wkv_spec = _state_out(states and states[1], li, nseq, g, (H_A, HEAD_A, HEAD_A))
    ssm_shape, ssm_spec = _state_out(ssm_t, li, nseq, g, (H_B, HEAD_B, D_STATE))
    out_shape = (jax.ShapeDtypeStruct((nseq, length, D_MODEL), BF16),
                 jax.ShapeDtypeStruct((nseq, 1, RW_PAD), F32), wkv_shape,
                 jax.ShapeDtypeStruct((nseq, CONV_W - 1, MB_CONV_CH), F32), ssm_shape)
    out_specs = (pl.BlockSpec((g, t, D_MODEL), lambda b, c: (b, c, 0)),
                 pl.BlockSpec((g, 1, RW_PAD), lambda b, c: (b, 0, 0)), wkv_spec,
                 pl.BlockSpec((g, CONV_W - 1, MB_CONV_CH), lambda b, c: (b, 0, 0)), ssm_spec)
    n = g * t
    wide = lambda: pltpu.VMEM((n, D_MIX), F32)
    gs = 1 if (has_init and nchunks == 1) else g
    scratch = [pltpu.VMEM((g, 1, RW_PAD), F32), pltpu.VMEM((gs, H_A, HEAD_A, HEAD_A), F32),
               pltpu.VMEM((g, t + SUBLANES, MB_CONV_CH), F32), pltpu.VMEM((gs, H_B, HEAD_B, D_STATE), F32)]
    scratch += [wide() for _ in range(9)] + [pltpu.VMEM((n, D_MODEL), F32)]
    y, shift, wkv, conv, ssm = pl.pallas_call(
        functools.partial(_even_kernel_g, t=t, g=g, nchunks=nchunks, lv=lv, has_init=has_init),
        grid=(nseq // g, nchunks),
        in_specs=in_specs, out_specs=out_specs, out_shape=out_shape, scratch_shapes=scratch,
        input_output_aliases={2: 2, 4: 4} if has_init else {},
        compiler_params=pltpu.CompilerParams(dimension_semantics=("arbitrary", "arbitrary"),
                                             vmem_limit_bytes=VMEM_LIMIT),
        name="even_mixer",
    )(*args)
    return y, (shift[:, 0, :RW_COLS], wkv, conv, jnp.swapaxes(ssm, -1, -2))


def _odd_kernel(*refs, t, tsub, nchunks, lv, has_init):
    if has_init:
        z_ref, ret0_ref, gla0_ref, *rest = refs
    else:
        z_ref, *rest = refs
    (cos_ref, sin_ref, seg_ref, pw_ref, dte_ref, rnw_ref, gup_ref, gb_ref, gnw_ref,
     y_ref, ret_o, gla_o, ret_ref, gla_ref, la_s, y_s) = rest
    ci = pl.program_id(1)
    masked = lv < t

    @pl.when(ci == 0)
    def _():
        if has_init:
            ret_ref[...] = ret0_ref[0]
            for h in range(H_D):
                gla_ref[h] = jnp.transpose(gla0_ref[0, h])
        else:
            ret_ref[...] = jnp.zeros_like(ret_ref)
            gla_ref[...] = jnp.zeros_like(gla_ref)

    row = lax.broadcasted_iota(jnp.int32, (t, 1), 0)
    valid = row < lv

    cosf = cos_ref[...]
    sinf = sin_ref[...]

    def rotary(x):
        return x * cosf + pltpu.roll(x, HEAD_C // 2, axis=1) * sinf

    hc = range(H_C)
    qs = [rotary(z_ref[0, :, h * HEAD_C:(h + 1) * HEAD_C]) * HEAD_C ** -0.5 for h in hc]
    ks = [rotary(z_ref[0, :, D_MIX + h * HEAD_C:D_MIX + (h + 1) * HEAD_C]) for h in hc]
    if masked:
        ks = [jnp.where(valid, k, 0.0) for k in ks]
    vs = [z_ref[0, :, 2 * D_MIX + h * HEAD_C:2 * D_MIX + (h + 1) * HEAD_C] for h in hc]
    s0s = [ret_ref[h] for h in hc]
    sc = [_mm_nt(q, k) * seg_ref[h] for h, q, k in zip(hc, qs, ks)]
    inter = [_mm(q, s0) * pw_ref[h] for h, q, s0 in zip(hc, qs, s0s)]
    outs = [_mm(s, v) + i for s, v, i in zip(sc, vs, inter)]
    for h in hc:
        gamma = 1.0 - 2.0 ** (-5.0 - h)
        ret_ref[h] = s0s[h] * gamma ** lv + _mm_tn(ks[h] * dte_ref[h], vs[h])
    for h in hc:
        sl = slice(h * HEAD_C, (h + 1) * HEAD_C)
        rg = z_ref[0, :, 3 * D_MIX + h * HEAD_C:3 * D_MIX + (h + 1) * HEAD_C]
        y_s[:, sl] = _rms(outs[h], 1e-6) * rnw_ref[:, sl] * _silu(rg)

    gq0 = 4 * D_MIX
    gk0 = gq0 + H_D * HEAD_DK
    gv0 = gk0 + H_D * HEAD_DK
    gr0 = gv0 + D_MIX
    gd = z_ref[0, :, OD_GD:OD_GD + LANES]
    x = _mm(gd, gup_ref[...]) + gb_ref[...]
    la = -_softplus(-x) * (1.0 / GLA_NORMALIZER)
    if masked:
        la = jnp.where(valid, la, 0.0)
    la_s[...] = la
    incl = _tri(tsub, False)
    for s in range(t // tsub):
        rs = slice(s * tsub, (s + 1) * tsub)
        cum_all = _mm_sel(incl, la_s[rs, :])
        hd = range(H_D)
        cums = [cum_all[:, h * HEAD_DK:(h + 1) * HEAD_DK] for h in hd]
        qes = [z_ref[0, rs, gq0 + h * HEAD_DK:gq0 + (h + 1) * HEAD_DK] * HEAD_DK ** -0.5 * jnp.exp(cums[h])
               for h in hd]
        ks = [z_ref[0, rs, gk0 + h * HEAD_DK:gk0 + (h + 1) * HEAD_DK] for h in hd]
        if masked:
            ks = [jnp.where(valid[rs], k, 0.0) for k in ks]
        vs = [z_ref[0, rs, gv0 + h * HEAD_DV:gv0 + (h + 1) * HEAD_DV] for h in hd]
        sts = [gla_ref[h] for h in hd]
        scores = [jnp.where(incl, _mm_nt(qes[h], ks[h] * jnp.exp(-cums[h])), 0.0) for h in hd]
        outs = [_mm(scores[h], vs[h]) + _mm_nt(qes[h], sts[h]) for h in hd]
        for h in hd:
            last = cums[h][tsub - 1:tsub]
            gla_ref[h] = sts[h] * jnp.exp(last) + _mm_tn(vs[h], ks[h] * jnp.exp(last - cums[h]))
        for h in hd:
            gr = z_ref[0, rs, gr0 + h * HEAD_DV:gr0 + (h + 1) * HEAD_DV]
            ys = slice(D_MIX + h * HEAD_DV, D_MIX + (h + 1) * HEAD_DV)
            y_s[rs, ys] = _rms(outs[h], 1e-6) * gnw_ref[:, h * HEAD_DV:(h + 1) * HEAD_DV] * _silu(gr)
    y_ref[0] = y_s[...].astype(BF16)

    @pl.when(ci == nchunks - 1)
    def _():
        ret_o[0] = ret_ref[...]
        for h in range(H_D):
            gla_o[0, h] = jnp.transpose(gla_ref[h])


def _ret_tables(t, lv, pos0, nchunks):
    half = HEAD_C // 2
    inv = ROPE_BASE ** (-jnp.arange(half, dtype=F32) / half)
    pos = pos0 + jnp.arange(t * nchunks, dtype=F32)
    ang = pos[:, None] * inv[None, :]
    cos, sin = jnp.cos(ang), jnp.sin(ang)
    cosf = jnp.concatenate([cos, cos], axis=-1)
    sinf = jnp.concatenate([-sin, sin], axis=-1)
    gam = 1.0 - np.exp2(-5.0 - np.arange(H_C, dtype=np.float64))
    ti = np.arange(t)
    d = ti[:, None] - ti[None, :]
    seg = np.where(d >= 0, gam[:, None, None] ** np.maximum(d, 0)[None], 0.0)
    pw = np.broadcast_to((gam[:, None] ** (ti + 1)[None])[:, :, None], (H_C, t, LANES))
    dte = gam[:, None] ** np.maximum(lv - 1 - ti, 0)[None]
    dte = np.broadcast_to(np.where(ti < lv, dte, 0.0)[:, :, None], (H_C, t, LANES))
    return cosf, sinf, jnp.asarray(seg, F32), jnp.asarray(pw, F32), jnp.asarray(dte, F32)


def _odd_mixer(z, states, li, p, *, t, lv, pos0):
    nseq, length, _ = z.shape
    nchunks = length // t
    has_init = states is not None
    assert lv == t or nchunks == 1
    tsub = min(GLA_SUB, t)
    cosf, sinf, seg, pw, dte = _ret_tables(t, lv, pos0, nchunks)
    full = lambda *shape: pl.BlockSpec(shape, lambda b, c: (0,) * len(shape))
    in_specs = [pl.BlockSpec((1, t, OD_PAD), lambda b, c: (b, c, 0))]
    args = [z]
    if has_init:
        in_specs += [pl.BlockSpec((None, 1, H_C, HEAD_C, HEAD_C), lambda b, c: (li, b, 0, 0, 0)),
                     pl.BlockSpec((None, 1, H_D, HEAD_DK, HEAD_DV), lambda b, c: (li, b, 0, 0, 0))]
        args += list(states)
    in_specs += [pl.BlockSpec((t, HEAD_C), lambda b, c: (c, 0)),
                 pl.BlockSpec((t, HEAD_C), lambda b, c: (c, 0)),
                 full(H_C, t, t), full(H_C, t, LANES), full(H_C, t, LANES),
                 full(1, D_MIX), full(LANES, H_D * HEAD_DK), full(1, H_D * HEAD_DK), full(1, D_MIX)]
    args += [cosf, sinf, seg, pw, dte, p['ret_norm_w'], p['gate_up'], p['gate_b'], p['gla_norm_w']]
    ret_shape, ret_spec = _state_out(states and states[0], li, nseq, (H_C, HEAD_C, HEAD_C))
    gla_shape, gla_spec = _state_out(states and states[1], li, nseq, (H_D, HEAD_DK, HEAD_DV))
    out_shape = (jax.ShapeDtypeStruct((nseq, length, D_MODEL), BF16), ret_shape, gla_shape)
    out_specs = (pl.BlockSpec((1, t, D_MODEL), lambda b, c: (b, c, 0)), ret_spec, gla_spec)
    scratch = [pltpu.VMEM((H_C, HEAD_C, HEAD_C), F32), pltpu.VMEM((H_D, HEAD_DV, HEAD_DK), F32),
               pltpu.VMEM((t, H_D * HEAD_DK), F32), pltpu.VMEM((t, D_MODEL), F32)]
    y, ret, gla = pl.pallas_call(
        functools.partial(_odd_kernel, t=t, tsub=tsub, nchunks=nchunks, lv=lv, has_init=has_init),
        grid=(nseq, nchunks),
        in_specs=in_specs, out_specs=out_specs, out_shape=out_shape, scratch_shapes=scratch,
        input_output_aliases={1: 1, 2: 2} if has_init else {},
        compiler_params=pltpu.CompilerParams(dimension_semantics=("arbitrary", "arbitrary"),
                                             vmem_limit_bytes=VMEM_LIMIT),
        name="odd_mixer",
    )(*args)
    return y, (ret, gla)


def _odd_kernel_g(*refs, t, tsub, g, nchunks, lv, has_init):
    if has_init:
        z_ref, ret0_ref, gla0_ref, *rest = refs
    else:
        z_ref, *rest = refs
    (cos_ref, sin_ref, seg_ref, pw_ref, dte_ref, rnw_ref, gup_ref, gb_ref, gnw_ref,
     y_ref, ret_o, gla_o, ret_ref, gla_ref, la_s, y_s) = rest
    n = g * t
    ci = pl.program_id(1)
    masked = lv < t
    seqs = range(g)
    direct = has_init and nchunks == 1
    ret_src, ret_dst = (ret0_ref, ret_o) if direct else (ret_ref, ret_ref)

    @pl.when(ci == 0)
    def _():
        if has_init:
            if not direct:
                ret_ref[...] = ret0_ref[...]
            for s in seqs:
                for h in range(H_D):
                    gla_ref[s, h] = jnp.transpose(gla0_ref[s, h])
        else:
            ret_ref[...] = jnp.zeros_like(ret_ref)
            gla_ref[...] = jnp.zeros_like(gla_ref)

    valid = lax.broadcasted_iota(jnp.int32, (n, 1), 0) % t < lv

    def zcols(a, b):
        return z_ref[:, :, a:b].reshape(n, b - a)

    def per_seq(fn, rows):
        parts = [fn(s, slice(s * rows, (s + 1) * rows)) for s in seqs]
        return parts[0] if g == 1 else jnp.concatenate(parts, axis=0)

    cosf = cos_ref[...]
    sinf = sin_ref[...]

    def rotary(x):
        return x * cosf + pltpu.roll(x, HEAD_C // 2, axis=1) * sinf

    hc = range(H_C)
    qs = [rotary(zcols(h * HEAD_C, (h + 1) * HEAD_C)) * HEAD_C ** -0.5 for h in hc]
    ks = [rotary(zcols(D_MIX + h * HEAD_C, D_MIX + (h + 1) * HEAD_C)) for h in hc]
    if masked:
        ks = [jnp.where(valid, k, 0.0) for k in ks]
    vs = [zcols(2 * D_MIX + h * HEAD_C, 2 * D_MIX + (h + 1) * HEAD_C) for h in hc]
    sc = [_mm_nt(q, k) * seg_ref[h] for h, q, k in zip(hc, qs, ks)]
    inter = [per_seq(lambda s, rw: _mm(qs[h][rw], ret_src[s, h]), t) * pw_ref[h] for h in hc]
    outs = [_mm(s_, v) + i for s_, v, i in zip(sc, vs, inter)]
    for h in hc:
        gamma = 1.0 - 2.0 ** (-5.0 - h)
        kd = ks[h] * dte_ref[h]
        for s in seqs:
            rw = slice(s * t, (s + 1) * t)
            ret_dst[s, h] = ret_src[s, h] * gamma ** lv + _mm_tn(kd[rw], vs[h][rw])
    for h in hc:
        sl = slice(h * HEAD_C, (h + 1) * HEAD_C)
        rg = zcols(3 * D_MIX + h * HEAD_C, 3 * D_MIX + (h + 1) * HEAD_C)
        y_s[:, sl] = _rms(outs[h], 1e-6) * rnw_ref[:, sl] * _silu(rg)

    gq0 = 4 * D_MIX
    gk0 = gq0 + H_D * HEAD_DK
    gv0 = gk0 + H_D * HEAD_DK
    gr0 = gv0 + D_MIX
    x = _mm(zcols(OD_GD, OD_GD + LANES), gup_ref[...]) + gb_ref[...]
    la = -_softplus(-x) * (1.0 / GLA_NORMALIZER)
    if masked:
        la = jnp.where(valid, la, 0.0)
    la_s[...] = la
    assert g == 1 or tsub == t
    incl = _seq_masks(g, tsub)[1]
    hd = range(H_D)
    for u in range(t // tsub):
        rs = slice(0, n) if g > 1 else slice(u * tsub, (u + 1) * tsub)

        def cols(a, b, rs=rs):
            return z_ref[0, rs, a:b] if g == 1 else zcols(a, b)

        cum_all = _mm_sel(incl, la_s[rs, :])
        cums = [cum_all[:, h * HEAD_DK:(h + 1) * HEAD_DK] for h in hd]
        qes = [cols(gq0 + h * HEAD_DK, gq0 + (h + 1) * HEAD_DK) * HEAD_DK ** -0.5 * jnp.exp(cums[h])
               for h in hd]
        ks = [cols(gk0 + h * HEAD_DK, gk0 + (h + 1) * HEAD_DK) for h in hd]
        if masked:
            ks = [jnp.where(valid[rs], k, 0.0) for k in ks]
        vs = [cols(gv0 + h * HEAD_DV, gv0 + (h + 1) * HEAD_DV) for h in hd]
        scores = [jnp.where(incl, _mm_nt(qes[h], ks[h] * jnp.exp(-cums[h])), 0.0) for h in hd]
        inter = [per_seq(lambda s, rw: _mm_nt(qes[h][rw], gla_ref[s, h]), tsub) for h in hd]
        outs = [_mm(scores[h], vs[h]) + inter[h] for h in hd]
        for h in hd:
            for s in seqs:
                rw = slice(s * tsub, (s + 1) * tsub)
                cum = cums[h][rw]
                last = cum[tsub - 1:tsub]
                gla_ref[s, h] = (gla_ref[s, h] * jnp.exp(last)
                                 + _mm_tn(vs[h][rw], ks[h][rw] * jnp.exp(last - cum)))
        for h in hd:
            gr = cols(gr0 + h * HEAD_DV, gr0 + (h + 1) * HEAD_DV)
            ys = slice(D_MIX + h * HEAD_DV, D_MIX + (h + 1) * HEAD_DV)
            y_s[rs, ys] = _rms(outs[h], 1e-6) * gnw_ref[:, h * HEAD_DV:(h + 1) * HEAD_DV] * _silu(gr)
    y_ref[...] = y_s[...].reshape(g, t, D_MODEL).astype(BF16)

    @pl.when(ci == nchunks - 1)
    def _():
        if not direct:
            ret_o[...] = ret_ref[...]
        for s in seqs:
            for h in range(H_D):
                gla_o[s, h] = jnp.transpose(gla_ref[s, h])


def _ret_tables_g(t, g, lv, pos0, nchunks):
    half = HEAD_C // 2
    inv = ROPE_BASE ** (-jnp.arange(half, dtype=F32) / half)
    pos = pos0 + jnp.arange(t * nchunks, dtype=F32).reshape(nchunks, 1, t)
    pos = jnp.broadcast_to(pos, (nchunks, g, t)).reshape(-1)
    ang = pos[:, None] * inv[None, :]
    cos, sin = jnp.cos(ang), jnp.sin(ang)
    cosf = jnp.concatenate([cos, cos], axis=-1)
    sinf = jnp.concatenate([-sin, sin], axis=-1)
    gam = 1.0 - np.exp2(-5.0 - np.arange(H_C, dtype=np.float64))
    ti = np.arange(t)
    d = ti[:, None] - ti[None, :]
    seg = np.where(d >= 0, gam[:, None, None] ** np.maximum(d, 0)[None], 0.0)
    seg = np.stack([np.kron(np.eye(g), s) for s in seg])
    pw = np.tile(gam[:, None] ** (ti + 1)[None], (1, g))
    dte = np.tile(np.where(ti < lv, gam[:, None] ** np.maximum(lv - 1 - ti, 0)[None], 0.0), (1, g))
    lanes = lambda a: jnp.asarray(np.broadcast_to(a[:, :, None], a.shape + (LANES,)), F32)
    return cosf, sinf, jnp.asarray(seg, F32), lanes(pw), lanes(dte)


def _odd_mixer_g(z, states, li, p, *, t, g, lv, pos0):
    nseq, length, _ = z.shape
    nchunks = length // t
    has_init = states is not None
    assert (lv == t or nchunks == 1) and nseq % g == 0
    tsub = min(GLA_SUB, t)
    n = g * t
    cosf, sinf, seg, pw, dte = _ret_tables_g(t, g, lv, pos0, nchunks)
    full = lambda *shape: pl.BlockSpec(shape, lambda b, c: (0,) * len(shape))
    in_specs = [pl.BlockSpec((g, t, OD_PAD), lambda b, c: (b, c, 0))]
    args = [z]
    if has_init:
        in_specs += [pl.BlockSpec((None, g, H_C, HEAD_C, HEAD_C), lambda b, c: (li, b, 0, 0, 0)),
                     pl.BlockSpec((None, g, H_D, HEAD_DK, HEAD_DV), lambda b, c: (li, b, 0, 0, 0))]
        args += list(states)
    in_specs += [pl.BlockSpec((n, HEAD_C), lambda b, c: (c, 0)),
                 pl.BlockSpec((n, HEAD_C), lambda b, c: (c, 0)),
                 full(H_C, n, n), full(H_C, n, LANES), full(H_C, n, LANES),
                 full(1, D_MIX), full(LANES, H_D * HEAD_DK), full(1, H_D * HEAD_DK), full(1, D_MIX)]
    args += [cosf, sinf, seg, pw, dte, p['ret_norm_w'], p['gate_up'], p['gate_b'], p['gla_norm_w']]
    ret_shape, ret_spec = _state_out(states and states[0], li, nseq, g, (H_C, HEAD_C, HEAD_C))
    gla_shape, gla_spec = _state_out(states and states[1], li, nseq, g, (H_D, HEAD_DK, HEAD_DV))
    out_shape = (jax.ShapeDtypeStruct((nseq, length, D_MODEL), BF16), ret_shape, gla_shape)
    out_specs = (pl.BlockSpec((g, t, D_MODEL), lambda b, c: (b, c, 0)), ret_spec, gla_spec)
    gs = 1 if (has_init and nchunks == 1) else g
    scratch = [pltpu.VMEM((gs, H_C, HEAD_C, HEAD_C), F32), pltpu.VMEM((g, H_D, HEAD_DV, HEAD_DK), F32),
               pltpu.VMEM((n, H_D * HEAD_DK), F32), pltpu.VMEM((n, D_MODEL), F32)]
    y, ret, gla = pl.pallas_call(
        functools.partial(_odd_kernel_g, t=t, tsub=tsub, g=g, nchunks=nchunks, lv=lv, has_init=has_init),
        grid=(nseq // g, nchunks),
        in_specs=in_specs, out_specs=out_specs, out_shape=out_shape, scratch_shapes=scratch,
        input_output_aliases={1: 1, 2: 2} if has_init else {},
        compiler_params=pltpu.CompilerParams(dimension_semantics=("arbitrary", "arbitrary"),
                                             vmem_limit_bytes=VMEM_LIMIT),
        name="odd_mixer",
    )(*args)
    return y, (ret, gla)


def _prep_proj(W):
    w = W['ev_w_in']
    n = w.shape[0]
    ev_in = jnp.concatenate([w[:, :, :RW_COLS], jnp.zeros((n, D_MODEL, RW_PAD - RW_COLS), F32),
                             w[:, :, RW_COLS:], jnp.zeros((n, D_MODEL, MB_PAD - MB_COLS), F32)], axis=2)
    w = W['od_w_in']
    n = w.shape[0]
    gd0 = OD_COLS - D_MIX - GLA_LORA
    od_in = jnp.concatenate([w[:, :, :gd0], w[:, :, gd0 + GLA_LORA:], w[:, :, gd0:gd0 + GLA_LORA],
                             jnp.zeros((n, D_MODEL, OD_PAD - OD_COLS), F32)], axis=2)
    return dict(ev_in=ev_in.astype(BF16), od_in=od_in.astype(BF16),
                ev_out=W['ev_w_out'].astype(BF16), od_out=W['od_w_out'].astype(BF16))


def _prep_even(i, W):
    mu =jnp.pad(W['rw_mu'][i], (0, RW_PAD - RW_COLS)).reshape(1, RW_PAD)
    rep = lambda v: jnp.repeat(v, HEAD_B)
    rows = [W['rw_w0'][i], W['rw_a0'][i], W['rw_k_k'][i], W['rw_k_a'][i], W['rw_r_k'][i].reshape(-1),
            W['rw_gn_w'][i], W['rw_gn_b'][i], rep(W['mb_d'][i]), W['mb_norm_w'][i]]
    vec = jnp.concatenate([jnp.stack(rows), jnp.zeros((EV_VEC_ROWS - len(rows), D_MIX), F32)], axis=0)
    wa = jnp.zeros((LANES, 2 * D_MIX), F32)
    wa = wa.at[:RW_LORA_W, :D_MIX].set(W['rw_w_up'][i]).at[RW_LORA_W:, D_MIX:].set(W['rw_a_up'][i])
    gup = jnp.pad(W['rw_g_up'][i], ((0, RW_PAD - 3 * D_MIX - LANES - RW_LORA_G), (0, 0)))
    pad_h = lambda v: jnp.pad(v, (0, LANES - H_B)).reshape(1, LANES)
    expand = np.zeros((LANES, D_MIX), np.float32)
    for h in range(H_B):
        expand[h, h * HEAD_B:(h + 1) * HEAD_B] = 1.0
    return dict(mu=mu, vec=vec, wa=wa.astype(BF16), gup=gup.astype(BF16), conv_w=W['mb_conv_w'][i],
                conv_b=W['mb_conv_b'][i].reshape(1, MB_CONV_CH), dt_bias=pad_h(W['mb_dt_bias'][i]),
                a_log=pad_h(W['mb_a_log'][i]), expand=jnp.asarray(expand, BF16))


def _prep_odd(i, W):
    gate_up = jnp.pad(W['gla_gate_up'][i], ((0, LANES - GLA_LORA), (0, 0)))
    return dict(ret_norm_w=W['ret_norm_w'][i].reshape(1, D_MIX), gate_up=gate_up.astype(BF16),
                gate_b=W['gla_gate_b'][i].reshape(1, H_D * HEAD_DK),
                gla_norm_w=W['gla_norm_w'][i].reshape(1, D_MIX))


def _trunk(h, mod, mc0, states, P, PW, FW, final_g, *, nseq, length, t, g, lv, pos0):
    outs = ([], [], [], [], [], [])
    padded = -(-length // t) * t
    carried = None if states is None else list(states)
    stacked_f32 = isinstance(FW, tuple)
    bf16_weights = []

    def ffn(h, mc, mod_base, k, final_g=None):
        if not stacked_f32:
            return _ffn(h, mod, mc, mod_base, FW[2 * mc['l'] + k], k, final_g)
        h, w16 = _ffn(h, mod, mc, mod_base, FW, k, final_g)
        bf16_weights.append(w16)
        return h

    for l in range(DEPTH):
        i = l // 2
        p = P[l]
        mc = dict(mc0, l=l)
        even = l % 2 == 0
        h = ffn(h, mc, 0, 0)
        z = _inproj(h, mod, mc, PW['ev_in' if even else 'od_in'], i).reshape(nseq, length, -1)
        if padded != length:
            z = jnp.pad(z, ((0, 0), (0, padded - length), (0, 0)))
        if even:
            y, new = _even_mixer_g(z, carried and tuple(carried[:4]), i, p, t=t, g=g, lv=lv)
            slots = (0, 1, 2, 3)
        else:
            y, new = _odd_mixer_g(z, carried and tuple(carried[4:]), i, p, t=t, g=g, lv=lv, pos0=pos0)
            slots = (4, 5)
        for k, n in zip(slots, new):
            if carried is not None and k in (1, 3, 4, 5):
                carried[k] = n
            else:
                outs[k].append(n)
        y = y[:, :length].reshape(nseq * length, D_MODEL)
        h = _outproj(y, PW['ev_out' if even else 'od_out'], i, h, mod, mc)
        h = ffn(h, mc, 6, 1, final_g=final_g if l == DEPTH - 1 else None)
    return h, tuple(jnp.stack(lst) if lst else carried[k] for k, lst in enumerate(outs)), bf16_weights


def kernel(x_prompt, x_sample, state_rwkv_shift, state_rwkv_wkv, state_mamba_conv, state_mamba_ssm,
           state_ret, state_gla, c_prompt, c_sample, ada_w, ada_b, ffn_wg, ffn_wu, ffn_wd, ev_w_in,
           ev_w_out, rw_mu, rw_w0, rw_w_up, rw_a0, rw_a_up, rw_g_up, rw_k_k, rw_k_a, rw_r_k, rw_gn_w,
           rw_gn_b, mb_conv_w, mb_conv_b, mb_dt_bias, mb_a_log, mb_d, mb_norm_w, od_w_in, od_w_out,
           ret_norm_w, gla_gate_up, gla_gate_b, gla_norm_w, final_g):
    W = dict(ev_w_in=ev_w_in, ev_w_out=ev_w_out, rw_mu=rw_mu, rw_w0=rw_w0, rw_w_up=rw_w_up,
             rw_a0=rw_a0, rw_a_up=rw_a_up, rw_g_up=rw_g_up, rw_k_k=rw_k_k, rw_k_a=rw_k_a,
             rw_r_k=rw_r_k, rw_gn_w=rw_gn_w, rw_gn_b=rw_gn_b, mb_conv_w=mb_conv_w,
             mb_conv_b=mb_conv_b, mb_dt_bias=mb_dt_bias, mb_a_log=mb_a_log, mb_d=mb_d,
             mb_norm_w=mb_norm_w, od_w_in=od_w_in, od_w_out=od_w_out, ret_norm_w=ret_norm_w,
             gla_gate_up=gla_gate_up, gla_gate_b=gla_gate_b, gla_norm_w=gla_norm_w)
    nb, seq, _ = x_prompt.shape
    db, dseq, _ = x_sample.shape
    P = [(_prep_even if l % 2 == 0 else _prep_odd)(l // 2, W) for l in range(DEPTH)]
    PW = _prep_proj(W)

    rows = db + nb
    rows_pad = -(-rows // SUBLANES) * SUBLANES
    c_all = jnp.concatenate([c_sample, c_prompt, jnp.zeros((rows_pad - rows, D_MODEL), F32)], axis=0)
    mod = _ada(c_all, ada_w, ada_b)

    states = (state_rwkv_shift, state_rwkv_wkv, state_mamba_conv, state_mamba_ssm, state_ret, state_gla)
    y_s, st_s, w16 = _trunk(x_sample.reshape(db * dseq, D_MODEL), mod, dict(row0=0, rps=dseq), states,
                            P, PW, (ffn_wg, ffn_wu, ffn_wd), final_g, nseq=db, length=dseq,
                            t=SAMPLE_CHUNK, g=SAMPLE_GROUP, lv=dseq, pos0=float(PAST_LEN))
    y_p, st_p, _ = _trunk(x_prompt.reshape(nb * seq, D_MODEL), mod, dict(row0=db, rps=seq), None,
                          P, PW, w16, final_g, nseq=nb, length=seq, t=PROMPT_CHUNK, g=1, lv=PROMPT_CHUNK,
                          pos0=0.0)
    return (y_p.reshape(nb, seq, D_MODEL), y_s.reshape(db, dseq, D_MODEL)) + st_p + st_s
```

```python
import functools
import math

import numpy as np
import jax
import jax.numpy as jnp
from jax import lax
from jax.experimental import pallas as pl
from jax.experimental.pallas import tpu as pltpu

F32 = jnp.float32
BF16 = jnp.bfloat16

D_MODEL = 2048
DEPTH = 4
PAST_LEN = 16384
N_MOD = 9
D_FF = 5632

D_MIX = 1024
H_A, HEAD_A = 16, 64
RW_LORA_W, RW_LORA_A, RW_LORA_G = 64, 64, 160
RW_COLS = 3 * D_MIX + RW_LORA_W + RW_LORA_A + RW_LORA_G
RW_GN_EPS = 64e-5
H_B, HEAD_B, D_STATE, N_GROUPS, CONV_W = 16, 64, 128, 2, 4
MB_CONV_CH = D_MIX + 2 * N_GROUPS * D_STATE
MB_COLS = D_MIX + MB_CONV_CH + H_B
H_C, HEAD_C = 8, 128
ROPE_BASE = 10000.0
H_D, HEAD_DK, HEAD_DV, GLA_LORA = 4, 128, 256, 16
GLA_NORMALIZER = 16.0
OD_COLS = 4 * D_MIX + 2 * H_D * HEAD_DK + 2 * D_MIX + GLA_LORA

LANES = 128
SUBLANES = 8
VMEM_LIMIT = 60 * 1024 * 1024

RW_PAD = 3456
ZM = RW_PAD
MB_PAD = 2688
EV_PAD = RW_PAD + MB_PAD
OD_PAD = 7296
OD_GD = 7168

PROMPT_CHUNK = 64
GLA_SUB = 32
SAMPLE_CHUNK = 8
SAMPLE_GROUP = 8

ROW_TILE = 512
NORM_ROWS = 256
FFN_ROW_TILE = 512


def _mm(a, b):
    return jnp.dot(a.astype(BF16), b.astype(BF16), preferred_element_type=F32)


def _mm_nt(a, b):
    return lax.dot_general(a.astype(BF16), b.astype(BF16), (((1,), (1,)), ((), ())),
                           preferred_element_type=F32)


def _mm_tn(a, b):
    return lax.dot_general(a.astype(BF16), b.astype(BF16), (((0,), (0,)), ((), ())),
                           preferred_element_type=F32)


def _split3(x):
    hi = x.astype(BF16)
    r = x - hi.astype(F32)
    mid = r.astype(BF16)
    lo = (r - mid.astype(F32)).astype(BF16)
    return hi, mid, lo


def _mm_sel(sel, x):
    s = sel.astype(BF16)
    hi, mid, lo = _split3(x)
    return (jnp.dot(s, hi, preferred_element_type=F32) + jnp.dot(s, mid, preferred_element_type=F32)
            + jnp.dot(s, lo, preferred_element_type=F32))


def _mm_tn_sel(x, sel):
    s = sel.astype(BF16)
    dn = (((0,), (0,)), ((), ()))
    hi, mid, lo = _split3(x)
    return (lax.dot_general(hi, s, dn, preferred_element_type=F32)
            + lax.dot_general(mid, s, dn, preferred_element_type=F32)
            + lax.dot_general(lo, s, dn, preferred_element_type=F32))


def _mm_xsel(x, sel):
    s = sel.astype(BF16)
    hi, mid, lo = _split3(x)
    return (jnp.dot(hi, s, preferred_element_type=F32) + jnp.dot(mid, s, preferred_element_type=F32)
            + jnp.dot(lo, s, preferred_element_type=F32))


def _sigmoid(x):
    return 1.0 / (1.0 + jnp.exp(-x))


def _silu(x):
    return x * _sigmoid(x)


def _softplus(x):
    return jnp.maximum(x, 0.0) + jnp.log(1.0 + jnp.exp(-jnp.abs(x)))


def _rms(x, eps):
    return x * lax.rsqrt(jnp.mean(x * x, axis=-1, keepdims=True) + eps)


def _mod_specs(mc, tm, idxs):
    l, rps, row0 = mc['l'], mc['rps'], mc['row0']
    specs = []
    for idx in idxs:
        if rps >= tm:
            specs.append(pl.BlockSpec(
                (None, None, SUBLANES, D_MODEL),
                lambda i, j, idx=idx: (l, idx, (row0 + i * tm // rps) // SUBLANES, 0)))
        else:
            assert row0 == 0
            specs.append(pl.BlockSpec((None, None, tm // rps, D_MODEL), lambda i, j, idx=idx: (l, idx, i, 0)))
    return specs


def _mod_val(ref, mc, tm):
    rps = mc['rps']
    if rps >= tm:
        r = (mc['row0'] + pl.program_id(0) * tm // rps) % SUBLANES
        return ref[pl.ds(r, 1), :]
    nseq = tm // rps
    row = lax.broadcasted_iota(jnp.int32, (tm, nseq), 0)
    first = lax.broadcasted_iota(jnp.int32, (tm, nseq), 1) * rps
    return _mm_sel((row >= first) & (row < first + rps), ref[...])


def _ada_kernel(c_ref, w_ref, b_ref, o_ref):
    x = _silu(c_ref[...]).astype(BF16)
    o_ref[...] = jnp.dot(x, w_ref[...].astype(BF16), preferred_element_type=F32) + b_ref[...]


def _ada(c, ada_w, ada_b):
    rows = c.shape[0]
    n = ada_w.shape[-1]
    tn = 1024
    per_mod = D_MODEL // tn
    return pl.pallas_call(
        _ada_kernel,
        grid=(DEPTH, n // tn),
        in_specs=[pl.BlockSpec((rows, D_MODEL), lambda l, j: (0, 0)),
                  pl.BlockSpec((None, D_MODEL, tn), lambda l, j: (l, 0, j)),
                  pl.BlockSpec((None, 1, tn), lambda l, j: (l, 0, j))],
        out_specs=pl.BlockSpec((None, None, rows, tn), lambda l, j: (l, j // per_mod, 0, j % per_mod)),
        out_shape=jax.ShapeDtypeStruct((DEPTH, N_MOD, rows, D_MODEL), F32),
        compiler_params=pltpu.CompilerParams(dimension_semantics=("arbitrary", "arbitrary"),
                                             vmem_limit_bytes=VMEM_LIMIT),
        name="ada",
    )(c, ada_w, ada_b.reshape(DEPTH, 1, n))


def _ffn_kernel(*refs, mc, tm, nj, final, emit):
    refs = list(refs)
    h_ref, sh_ref, sc_ref, g_ref, wg_ref, wu_ref, wd_ref = refs[:7]
    fin_ref = refs[7] if final else None
    o_ref = refs[7 + final]
    xn_ref = refs[-1]
    j = pl.program_id(1)

    if emit:
        wrefs = refs[8 + final:11 + final]
        for src, dst in zip((wg_ref, wu_ref, wd_ref), wrefs):
            dst[...] = src[...].astype(BF16)
    else:
        wrefs = (wg_ref, wu_ref, wd_ref)

    def swiglu(xn):
        gate = jnp.dot(xn, wrefs[0][...], preferred_element_type=F32)
        up = jnp.dot(xn, wrefs[1][...], preferred_element_type=F32)
        act = (_silu(gate) * up).astype(BF16)
        return jnp.dot(act, wrefs[2][...], preferred_element_type=F32)

    @pl.when(j == 0)
    def _():
        scale = 1.0 + _mod_val(sc_ref, mc, tm)
        shift = _mod_val(sh_ref, mc, tm)
        per_row = scale.shape[0] > 1
        for r in range(tm // NORM_ROWS):
            rows = slice(r * NORM_ROWS, (r + 1) * NORM_ROWS)
            xn = (_rms(h_ref[rows, :], 1e-6) * (scale[rows] if per_row else scale)
                  + (shift[rows] if per_row else shift)).astype(BF16)
            xn_ref[rows, :] = xn
            o_ref[rows, :] = swiglu(xn)

    @pl.when(j > 0)
    def _():
        o_ref[...] += swiglu(xn_ref[...])

    @pl.when(j == nj - 1)
    def _():
        out = h_ref[...] + 0.5 * (1.0 + _mod_val(g_ref, mc, tm)) * o_ref[...]
        if final:
            out = _rms(out, 1e-6) * fin_ref[...]
        o_ref[...] = out


def _ffn(h, mod, mc, mod_base, w, k, final_g=None):
    m = h.shape[0]
    l = mc['l']
    emit = w[0].ndim == 4
    tm = min(FFN_ROW_TILE, m)
    tf = 256 if emit else 512
    nj = D_FF // tf
    final = final_g is not None
    in_specs = [pl.BlockSpec((tm, D_MODEL), lambda i, j: (i, 0))]
    in_specs += _mod_specs(mc, tm, (mod_base, mod_base + 1, mod_base + 2))
    col = pl.BlockSpec((D_MODEL, tf), lambda i, j: (0, j))
    row = pl.BlockSpec((tf, D_MODEL), lambda i, j: (j, 0))
    if emit:
        assert m == tm
        in_specs += [pl.BlockSpec((None, None, D_MODEL, tf), lambda i, j: (l, k, 0, j)),
                     pl.BlockSpec((None, None, D_MODEL, tf), lambda i, j: (l, k, 0, j)),
                     pl.BlockSpec((None, None, tf, D_MODEL), lambda i, j: (l, k, j, 0))]
    else:
        in_specs += [col, col, row]
    args = [h, mod, mod, mod, *w]
    if final:
        in_specs.append(pl.BlockSpec((1, D_MODEL), lambda i, j: (0, 0)))
        args.append(final_g.reshape(1, D_MODEL))
    out_specs = [pl.BlockSpec((tm, D_MODEL), lambda i, j: (i, 0))]
    out_shape = [jax.ShapeDtypeStruct((m, D_MODEL), F32)]
    if emit:
        out_specs += [col, col, row]
        out_shape += [jax.ShapeDtypeStruct((D_MODEL, D_FF), BF16), jax.ShapeDtypeStruct((D_MODEL, D_FF), BF16),
                      jax.ShapeDtypeStruct((D_FF, D_MODEL), BF16)]
    out = pl.pallas_call(
        functools.partial(_ffn_kernel, mc=mc, tm=tm, nj=nj, final=final, emit=emit),
        grid=(m // tm, nj),
        in_specs=in_specs, out_specs=out_specs, out_shape=out_shape,
        scratch_shapes=[pltpu.VMEM((tm, D_MODEL), BF16)],
        compiler_params=pltpu.CompilerParams(dimension_semantics=("arbitrary", "arbitrary"),
                                             vmem_limit_bytes=VMEM_LIMIT),
        name="ffn",
    )(*args)
    return (out[0], tuple(out[1:])) if emit else out[0]


def _inproj_kernel(h_ref, sh_ref, sc_ref, w_ref, o_ref, xn_ref, *, mc, tm):
    @pl.when(pl.program_id(1) == 0)
    def _():
        xn = _rms(h_ref[...], 1e-6) * (1.0 + _mod_val(sc_ref, mc, tm)) + _mod_val(sh_ref, mc, tm)
        xn_ref[...] = xn.astype(BF16)

    o_ref[...] = jnp.dot(xn_ref[...], w_ref[...], preferred_element_type=F32)


def _inproj(h, mod, mc, w, li):
    m = h.shape[0]
    n = w.shape[2]
    tm = min(ROW_TILE, m)
    tn = n // 3
    if n % (6 * LANES) == 0 and m % (2 * ROW_TILE) == 0:
        tm, tn = 2 * ROW_TILE, n // 6
    in_specs = [pl.BlockSpec((tm, D_MODEL), lambda i, j: (i, 0))]
    in_specs += _mod_specs(mc, tm, (3, 4))
    in_specs += [pl.BlockSpec((None, D_MODEL, tn), lambda i, j: (li, 0, j))]
    return pl.pallas_call(
        functools.partial(_inproj_kernel, mc=mc, tm=tm),
        grid=(m // tm, n // tn),
        in_specs=in_specs,
        out_specs=pl.BlockSpec((tm, tn), lambda i, j: (i, j)),
        out_shape=jax.ShapeDtypeStruct((m, n), F32),
        scratch_shapes=[pltpu.VMEM((tm, D_MODEL), BF16)],
        compiler_params=pltpu.CompilerParams(dimension_semantics=("arbitrary", "arbitrary"),
                                             vmem_limit_bytes=VMEM_LIMIT),
        name="inproj",
    )(h, mod, mod, w)


def _outproj_kernel(y_ref, w_ref, h_ref, g_ref, o_ref, *, mc, tm):
    mix = jnp.dot(y_ref[...], w_ref[...], preferred_element_type=F32)
    o_ref[...] = h_ref[...] + (1.0 + _mod_val(g_ref, mc, tm)) * mix


def _outproj(y, w, li, h, mod, mc):
    m = h.shape[0]
    tm = min(ROW_TILE, m)
    in_specs = [pl.BlockSpec((tm, D_MODEL), lambda i, j: (i, 0)),
                pl.BlockSpec((None, D_MODEL, D_MODEL), lambda i, j: (li, 0, 0)),
                pl.BlockSpec((tm, D_MODEL), lambda i, j: (i, 0))]
    in_specs += _mod_specs(mc, tm, (5,))
    return pl.pallas_call(
        functools.partial(_outproj_kernel, mc=mc, tm=tm),
        grid=(m // tm, 1),
        in_specs=in_specs,
        out_specs=pl.BlockSpec((tm, D_MODEL), lambda i, j: (i, 0)),
        out_shape=jax.ShapeDtypeStruct((m, D_MODEL), F32),
        compiler_params=pltpu.CompilerParams(dimension_semantics=("arbitrary", "arbitrary"),
                                             vmem_limit_bytes=VMEM_LIMIT),
        name="outproj",
    )(y, w, h, mod)


def _state_out(stacked, li, nseq, g, tail):
    zeros = (0,) * len(tail)
    if stacked is None:
        return (jax.ShapeDtypeStruct((nseq,) + tail, F32),
                pl.BlockSpec((g,) + tail, lambda b, c: (b,) + zeros))
    return (jax.ShapeDtypeStruct(stacked.shape, F32),
            pl.BlockSpec((None, g) + tail, lambda b, c: (li, b) + zeros))


def _seq_masks(g, t):
    n = g * t
    r = lax.broadcasted_iota(jnp.int32, (n, n), 0)
    c = lax.broadcasted_iota(jnp.int32, (n, n), 1)
    if g == 1:
        return r > c, r >= c
    same = (r // t) == (c // t)
    return same & (r > c), same & (r >= c)


def _seq_selectors(g, t):
    r = np.arange(g * t)[:, None]
    c = np.arange(g * t)[None, :]
    same = (r // t) == (c // t)
    incl = same & (r >= c)
    last = c == (r // t) * t + (t - 1)
    return jnp.asarray(np.stack([incl, last, incl.T]), BF16)


EV_W0, EV_A0, EV_KK, EV_KA, EV_RK, EV_GNW, EV_GNB, EV_D, EV_NW = range(9)
EV_VEC_ROWS = 16


def _even_kernel(*refs, t, g, nchunks, lv, has_init):
    if has_init:
        z_ref, shift0_ref, wkv0_ref, conv0_ref, ssm0_ref, *rest = refs
    else:
        z_ref, *rest = refs
    (mu_ref, vec_ref, wa_ref, gup_ref, cw_ref, cb_ref, dtb_ref, alog_ref, exp_ref, sel_ref,
     y_ref, shift_o, wkv_o, conv_o, ssm_o,
     carry_ref, wkv_ref, ext_ref, ssm_ref,
     r_s, kp_s, v_s, kk_s, a_s, c_s, ld_s, g_s, cl_s, y_s) = rest
    n = g * t
    ci = pl.program_id(1)
    masked = lv < t
    nlog = int(math.log2(t))
    seqs = range(g)
    hist = SUBLANES - (CONV_W - 1)
    direct = has_init and nchunks == 1
    wkv_src, wkv_dst = (wkv0_ref, wkv_o) if direct else (wkv_ref, wkv_ref)
    ssm_src, ssm_dst = (ssm0_ref, ssm_o) if direct else (ssm_ref, ssm_ref)

    @pl.when(ci == 0)
    def _():
        if has_init:
            carry_ref[:, :, 0:RW_COLS] = shift0_ref[...]
            carry_ref[:, :, RW_COLS:RW_PAD] = jnp.zeros((g, 1, RW_PAD - RW_COLS), F32)
            ext_ref[:, 0:hist, :] = jnp.zeros((g, hist, MB_CONV_CH), F32)
            ext_ref[:, hist:SUBLANES, :] = conv0_ref[...]
            if not direct:
                wkv_ref[...] = wkv0_ref[...]
                ssm_ref[...] = ssm0_ref[...]
        else:
            carry_ref[...] = jnp.zeros_like(carry_ref)
            wkv_ref[...] = jnp.zeros_like(wkv_ref)
            ext_ref[:, 0:SUBLANES, :] = jnp.zeros((g, SUBLANES, MB_CONV_CH), F32)
            ssm_ref[...] = jnp.zeros_like(ssm_ref)

    tok = lax.broadcasted_iota(jnp.int32, (n, 1), 0) % t
    valid = tok < lv
    strict, incl = _seq_masks(g, t)
    sel_incl, sel_last, sel_upper = sel_ref[0], sel_ref[1], sel_ref[2]

    def vec(i, lo=0, hi=D_MIX):
        return vec_ref[i:i + 1, lo:hi]

    def rows_of(x, s):
        return x[s * t:(s + 1) * t]

    def per_seq(fn):
        parts = [fn(s) for s in seqs]
        return parts[0] if g == 1 else jnp.concatenate(parts, axis=0)

    def shifted(a, b):
        cur = z_ref[:, :, a:b].reshape(n, b - a)
        carry = per_seq(lambda s: jnp.broadcast_to(carry_ref[s, :, a:b], (t, b - a)))
        prev = jnp.where(tok == 0, carry, pltpu.roll(cur, 1, axis=0))
        return cur + mu_ref[:, a:b] * (prev - cur)

    r_s[...] = shifted(0, D_MIX)
    k = shifted(D_MIX, 2 * D_MIX)
    v_s[...] = shifted(2 * D_MIX, 3 * D_MIX)
    lo = shifted(3 * D_MIX, 3 * D_MIX + LANES)
    lane = lax.broadcasted_iota(jnp.int32, (n, LANES), 1)
    wa = _mm(jnp.where(lane < RW_LORA_W, jnp.tanh(lo), lo), wa_ref[...])
    w_log = -_softplus(-(vec(EV_W0) + wa[:, 0:D_MIX])) - 0.5
    ld = -jnp.exp(w_log)
    a = _sigmoid(vec(EV_A0) + wa[:, D_MIX:2 * D_MIX])
    g_s[...] = _mm(_sigmoid(shifted(3 * D_MIX + LANES, RW_PAD)), gup_ref[...])
    kp = k * (1.0 + (a - 1.0) * vec(EV_KA))
    if masked:
        ld = jnp.where(valid, ld, 0.0)
        kp = jnp.where(valid, kp, 0.0)
        a = jnp.where(valid, a, 0.0)
    kk_s[...] = k * vec(EV_KK)
    kp_s[...] = kp
    a_s[...] = a
    ld_s[...] = ld
    c = _mm_sel(sel_incl, ld)
    c_s[...] = c
    cl_s[...] = _mm_sel(sel_last, c)
    carry_ref[...] = z_ref[:, lv - 1:lv, 0:RW_PAD]

    heads = range(H_A)
    sls = [slice(h * HEAD_A, (h + 1) * HEAD_A) for h in heads]
    rs = [r_s[:, sl] for sl in sls]
    kps = [kp_s[:, sl] for sl in sls]
    vs = [v_s[:, sl] for sl in sls]
    cs = [c_s[:, sl] for sl in sls]
    cls = [cl_s[:, sl] for sl in sls]
    kks, bs = [], []
    for sl in sls:
        kkr = kk_s[:, sl]
        nrm = jnp.sqrt(jnp.sum(kkr * kkr, axis=-1, keepdims=True))
        kks.append(kkr / jnp.maximum(nrm, 1e-12))
        bs.append(kks[-1] * a_s[:, sl])
    ecis = [jnp.exp(-c) for c in cs]
    kts = [kk * jnp.exp(c - ld_s[:, sl]) for kk, c, sl in zip(kks, cs, sls)]
    rts = [r * jnp.exp(c) for r, c in zip(rs, cs)]
    lhss = [jnp.concatenate([kt, rt], axis=0) for kt, rt in zip(kts, rts)]
    a1s = [_mm_nt(lhs, kp * eci) for lhs, kp, eci in zip(lhss, kps, ecis)]
    a2s = [_mm_nt(lhs, b * eci) for lhs, b, eci in zip(lhss, bs, ecis)]
    pms = [[_mm_nt(jnp.concatenate([rows_of(kt, s), rows_of(rt, s)], axis=0), wkv_src[s, h]) for s in seqs]
           for h, kt, rt in zip(heads, kts, rts)]
    pmk = [per_seq(lambda s: pm[s][0:t]) for pm in pms]
    pmr = [per_seq(lambda s: pm[s][t:2 * t]) for pm in pms]
    us = [pk + _mm(jnp.where(strict, a1[0:n], 0.0), v) for pk, a1, v in zip(pmk, a1s, vs)]
    ps = [jnp.where(strict, -a2[0:n], 0.0) for a2 in a2s]
    us = [u + _mm(p, u) for u, p in zip(us, ps)]
    for _ in range(nlog - 1):
        ps = [_mm(p, p) for p in ps]
        us = [u + _mm(p, u) for u, p in zip(us, ps)]
    ys = [pr + _mm(jnp.where(incl, a1[n:2 * n], 0.0), v) - _mm(jnp.where(incl, a2[n:2 * n], 0.0), u)
          for pr, a1, a2, v, u in zip(pmr, a1s, a2s, vs, us)]
    for h in heads:
        dec_end = jnp.exp(cls[h] - cs[h])
        kbar = kps[h] * dec_end
        bbar = bs[h] * dec_end
        for s in seqs:
            vu = jnp.concatenate([rows_of(vs[h], s), rows_of(us[h], s)], axis=0)
            kb = jnp.concatenate([rows_of(kbar, s), -rows_of(bbar, s)], axis=0)
            wkv_dst[s, h] = wkv_src[s, h] * jnp.exp(cls[h][s * t:s * t + 1]) + _mm_tn(vu, kb)
    for h, sl in zip(heads, sls):
        y, r, kp, v = ys[h], rs[h], kps[h], vs[h]
        mean = jnp.mean(y, axis=-1, keepdims=True)
        yc = y - mean
        var = jnp.mean(yc * yc, axis=-1, keepdims=True)
        yn = (yc * lax.rsqrt(var + RW_GN_EPS) * vec(EV_GNW, sl.start, sl.stop)
              + vec(EV_GNB, sl.start, sl.stop))
        bonus = jnp.sum(r * kp * vec(EV_RK, sl.start, sl.stop), axis=-1, keepdims=True) * v
        y_s[:, sl] = (yn + bonus) * g_s[:, sl]

    zg = z_ref[:, :, ZM:ZM + D_MIX].reshape(n, D_MIX)
    ext_ref[:, SUBLANES:SUBLANES + t, :] = z_ref[:, :, ZM + D_MIX:ZM + D_MIX + MB_CONV_CH]
    conv = cb_ref[...]
    for w in range(CONV_W):
        conv = conv + cw_ref[w:w + 1, :] * ext_ref[:, hist + w:hist + w + t, :].reshape(n, MB_CONV_CH)
    new_conv = ext_ref[:, SUBLANES + lv - (CONV_W - 1):SUBLANES + lv, :]
    ext_ref[:, hist:SUBLANES, :] = new_conv
    xbc = _silu(conv)
    xs = xbc[:, 0:D_MIX]
    dt = _softplus(z_ref[:, :, ZM + D_MIX + MB_CONV_CH:ZM + MB_PAD].reshape(n, LANES) + dtb_ref[...])
    logd = dt * (-jnp.exp(alog_ref[...]))
    if masked:
        logd = jnp.where(valid, logd, 0.0)
    c = _mm_sel(sel_incl, logd)
    c_t = _mm_tn_sel(logd, sel_upper)
    c_end = _mm_sel(sel_last, c)
    wide = _mm_xsel(jnp.concatenate([dt, c, c_end - c], axis=0), exp_ref[...])
    xdt = xs * wide[0:n]
    if masked:
        xdt = jnp.where(valid, xdt, 0.0)
    ec_w = jnp.exp(wide[n:2 * n])
    xdec = xdt * jnp.exp(wide[2 * n:3 * n])
    hpg = H_B // N_GROUPS
    bgs = [xbc[:, D_MIX + q * D_STATE:D_MIX + (q + 1) * D_STATE] for q in range(N_GROUPS)]
    cgs = [xbc[:, D_MIX + (N_GROUPS + q) * D_STATE:D_MIX + (N_GROUPS + q + 1) * D_STATE]
           for q in range(N_GROUPS)]
    scores = [_mm_nt(cg, bg) for cg, bg in zip(cgs, bgs)]
    segs = [jnp.where(incl, jnp.exp(jnp.where(incl, c[:, h:h + 1] - c_t[h:h + 1, :], 0.0)), 0.0)
            for h in range(H_B)]
    hsl = [slice(h * HEAD_B, (h + 1) * HEAD_B) for h in range(H_B)]
    intra = [_mm(scores[h // hpg] * segs[h], xdt[:, hsl[h]]) for h in range(H_B)]
    grp = lambda s, q: ssm_src[s, q * hpg:(q + 1) * hpg].reshape(hpg * HEAD_B, D_STATE)
    inter = [per_seq(lambda s: _mm_nt(rows_of(cgs[q], s), grp(s, q))) for q in range(N_GROUPS)]
    for q in range(N_GROUPS):
        for s in seqs:
            upd = _mm_tn(rows_of(xdec[:, q * hpg * HEAD_B:(q + 1) * hpg * HEAD_B], s), rows_of(bgs[q], s))
            last = s * t + t - 1
            for h in range(q * hpg, (q + 1) * hpg):
                ssm_dst[s, h] = (ssm_src[s, h] * jnp.exp(c[last:last + 1, h:h + 1])
                                 + upd[(h - q * hpg) * HEAD_B:(h - q * hpg + 1) * HEAD_B])
    o = jnp.concatenate(intra, axis=1) + jnp.concatenate(inter, axis=1) * ec_w
    yb = (o + vec(EV_D) * xs) * _silu(zg)
    y_s[:, D_MIX:2 * D_MIX] = _rms(yb, 1e-5) * vec(EV_NW)
    y_ref[...] = y_s[...].astype(BF16).reshape(g, t, D_MODEL)

    @pl.when(ci == nchunks - 1)
    def _():
        shift_o[...] = carry_ref[...]
        conv_o[...] = new_conv
        if not direct:
            wkv_o[...] = wkv_ref[...]
            ssm_o[...] = ssm_ref[...]


def _even_mixer(z, states, li, p, *, t, g, lv):
    nseq, length, _ = z.shape
    nchunks = length // t
    has_init = states is not None
    assert (lv == t or nchunks == 1) and nseq % g == 0
    full = lambda *shape: pl.BlockSpec(shape, lambda b, c: (0,) * len(shape))
    in_specs = [pl.BlockSpec((g, t, EV_PAD), lambda b, c: (b, c, 0))]
    args = [z]
    ssm_t = None
    if has_init:
        shift0, wkv0, conv0, ssm0 = states
        ssm_t = jnp.swapaxes(ssm0, 3, 4)
        in_specs += [pl.BlockSpec((None, g, 1, RW_COLS), lambda b, c: (li, b, 0, 0)),
                     pl.BlockSpec((None, g, H_A, HEAD_A, HEAD_A), lambda b, c: (li, b, 0, 0, 0)),
                     pl.BlockSpec((None, g, CONV_W - 1, MB_CONV_CH), lambda b, c: (li, b, 0, 0)),
                     pl.BlockSpec((None, g, H_B, HEAD_B, D_STATE), lambda b, c: (li, b, 0, 0, 0))]
        args += [shift0.reshape(-1, nseq, 1, RW_COLS), wkv0, conv0, ssm_t]
    in_specs += [full(1, RW_PAD), full(EV_VEC_ROWS, D_MIX), full(LANES, 2 * D_MIX),
                 full(RW_PAD - 3 * D_MIX - LANES, D_MIX), full(CONV_W, MB_CONV_CH),
                 full(1, MB_CONV_CH), full(1, LANES), full(1, LANES), full(LANES, D_MIX),
                 full(3, g * t, g * t)]
    args += [p['mu'], p['vec'], p['wa'], p['gup'], p['conv_w'], p['conv_b'], p['dt_bias'],
             p['a_log'], p['expand'], _seq_selectors(g, t)]
    wkv_shape, wkv_spec = _state_out(states and states[1], li, nseq, g, (H_A, HEAD_A, HEAD_A))
    ssm_shape, ssm_spec = _state_out(ssm_t, li, nseq, g, (H_B, HEAD_B, D_STATE))
    out_shape = (jax.ShapeDtypeStruct((nseq, length, D_MODEL), BF16),
                 jax.ShapeDtypeStruct((nseq, 1, RW_PAD), F32), wkv_shape,
                 jax.ShapeDtypeStruct((nseq, CONV_W - 1, MB_CONV_CH), F32), ssm_shape)
    out_specs = (pl.BlockSpec((g, t, D_MODEL), lambda b, c: (b, c, 0)),
                 pl.BlockSpec((g, 1, RW_PAD), lambda b, c: (b, 0, 0)), wkv_spec,
                 pl.BlockSpec((g, CONV_W - 1, MB_CONV_CH), lambda b, c: (b, 0, 0)), ssm_spec)
    n = g * t
    wide = lambda: pltpu.VMEM((n, D_MIX), F32)
    gs = 1 if (has_init and nchunks == 1) else g
    scratch = [pltpu.VMEM((g, 1, RW_PAD), F32), pltpu.VMEM((gs, H_A, HEAD_A, HEAD_A), F32),
               pltpu.VMEM((g, t + SUBLANES, MB_CONV_CH), F32), pltpu.VMEM((gs, H_B, HEAD_B, D_STATE), F32)]
    scratch += [wide() for _ in range(9)] + [pltpu.VMEM((n, D_MODEL), F32)]
    y, shift, wkv, conv, ssm = pl.pallas_call(
        functools.partial(_even_kernel, t=t, g=g, nchunks=nchunks, lv=lv, has_init=has_init),
        grid=(nseq // g, nchunks),
        in_specs=in_specs, out_specs=out_specs, out_shape=out_shape, scratch_shapes=scratch,
        input_output_aliases={2: 2, 4: 4} if has_init else {},
        compiler_params=pltpu.CompilerParams(dimension_semantics=("arbitrary", "arbitrary"),
                                             vmem_limit_bytes=VMEM_LIMIT),
        name="even_mixer",
    )(*args)
    return y, (shift[:, 0, :RW_COLS], wkv, conv, jnp.swapaxes(ssm, -1, -2))


def _odd_kernel(*refs, t, tsub, g, nchunks, lv, has_init):
    if has_init:
        z_ref, ret0_ref, gla0_ref, *rest = refs
    else:
        z_ref, *rest = refs
    (cos_ref, sin_ref, seg_ref, pw_ref, dte_ref, rnw_ref, gup_ref, gb_ref, gnw_ref, sel_ref,
     y_ref, ret_o, gla_o, ret_ref, gla_ref, la_s, y_s) = rest
    n = g * t
    ci = pl.program_id(1)
    masked = lv < t
    seqs = range(g)
    direct = has_init and nchunks == 1
    ret_src, ret_dst = (ret0_ref, ret_o) if direct else (ret_ref, ret_ref)

    @pl.when(ci == 0)
    def _():
        if has_init:
            if not direct:
                ret_ref[...] = ret0_ref[...]
            for s in seqs:
                for h in range(H_D):
                    gla_ref[s, h] = jnp.transpose(gla0_ref[s, h])
        else:
            ret_ref[...] = jnp.zeros_like(ret_ref)
            gla_ref[...] = jnp.zeros_like(gla_ref)

    valid = lax.broadcasted_iota(jnp.int32, (n, 1), 0) % t < lv

    def zcols(a, b):
        return z_ref[:, :, a:b].reshape(n, b - a)

    def per_seq(fn, rows):
        parts = [fn(s, slice(s * rows, (s + 1) * rows)) for s in seqs]
        return parts[0] if g == 1 else jnp.concatenate(parts, axis=0)

    cosf = cos_ref[...]
    sinf = sin_ref[...]

    def rotary(x):
        return x * cosf + pltpu.roll(x, HEAD_C // 2, axis=1) * sinf

    hc = range(H_C)
    qs = [rotary(zcols(h * HEAD_C, (h + 1) * HEAD_C)) * HEAD_C ** -0.5 for h in hc]
    ks = [rotary(zcols(D_MIX + h * HEAD_C, D_MIX + (h + 1) * HEAD_C)) for h in hc]
    if masked:
        ks = [jnp.where(valid, k, 0.0) for k in ks]
    vs = [zcols(2 * D_MIX + h * HEAD_C, 2 * D_MIX + (h + 1) * HEAD_C) for h in hc]
    sc = [_mm_nt(q, k) * seg_ref[h] for h, q, k in zip(hc, qs, ks)]
    inter = [per_seq(lambda s, rw: _mm(qs[h][rw], ret_src[s, h]), t) * pw_ref[h] for h in hc]
    outs = [_mm(s_, v) + i for s_, v, i in zip(sc, vs, inter)]
    for h in hc:
        gamma = 1.0 - 2.0 ** (-5.0 - h)
        kd = ks[h] * dte_ref[h]
        for s in seqs:
            rw = slice(s * t, (s + 1) * t)
            ret_dst[s, h] = ret_src[s, h] * gamma ** lv + _mm_tn(kd[rw], vs[h][rw])
    for h in hc:
        sl = slice(h * HEAD_C, (h + 1) * HEAD_C)
        rg = zcols(3 * D_MIX + h * HEAD_C, 3 * D_MIX + (h + 1) * HEAD_C)
        y_s[:, sl] = _rms(outs[h], 1e-6) * rnw_ref[:, sl] * _silu(rg)

    gq0 = 4 * D_MIX
    gk0 = gq0 + H_D * HEAD_DK
    gv0 = gk0 + H_D * HEAD_DK
    gr0 = gv0 + D_MIX
    x = _mm(zcols(OD_GD, OD_GD + LANES), gup_ref[...]) + gb_ref[...]
    la = -_softplus(-x) * (1.0 / GLA_NORMALIZER)
    if masked:
        la = jnp.where(valid, la, 0.0)
    la_s[...] = la
    assert g == 1 or tsub == t
    incl = _seq_masks(g, tsub)[1]
    hd = range(H_D)
    for u in range(t // tsub):
        rs = slice(0, n) if g > 1 else slice(u * tsub, (u + 1) * tsub)

        def cols(a, b, rs=rs):
            return z_ref[0, rs, a:b] if g == 1 else zcols(a, b)

        cum_all = _mm_sel(sel_ref[0], la_s[rs, :])
        cums = [cum_all[:, h * HEAD_DK:(h + 1) * HEAD_DK] for h in hd]
        qes = [cols(gq0 + h * HEAD_DK, gq0 + (h + 1) * HEAD_DK) * HEAD_DK ** -0.5 * jnp.exp(cums[h])
               for h in hd]
        ks = [cols(gk0 + h * HEAD_DK, gk0 + (h + 1) * HEAD_DK) for h in hd]
        if masked:
            ks = [jnp.where(valid[rs], k, 0.0) for k in ks]
        vs = [cols(gv0 + h * HEAD_DV, gv0 + (h + 1) * HEAD_DV) for h in hd]
        scores = [jnp.where(incl, _mm_nt(qes[h], ks[h] * jnp.exp(-cums[h])), 0.0) for h in hd]
        inter = [per_seq(lambda s, rw: _mm_nt(qes[h][rw], gla_ref[s, h]), tsub) for h in hd]
        outs = [_mm(scores[h], vs[h]) + inter[h] for h in hd]
        for h in hd:
            for s in seqs:
                rw = slice(s * tsub, (s + 1) * tsub)
                cum = cums[h][rw]
                last = cum[tsub - 1:tsub]
                gla_ref[s, h] = (gla_ref[s, h] * jnp.exp(last)
                                 + _mm_tn(vs[h][rw], ks[h][rw] * jnp.exp(last - cum)))
        for h in hd:
            gr = cols(gr0 + h * HEAD_DV, gr0 + (h + 1) * HEAD_DV)
            ys = slice(D_MIX + h * HEAD_DV, D_MIX + (h + 1) * HEAD_DV)
            y_s[rs, ys] = _rms(outs[h], 1e-6) * gnw_ref[:, h * HEAD_DV:(h + 1) * HEAD_DV] * _silu(gr)
    y_ref[...] = y_s[...].reshape(g, t, D_MODEL).astype(BF16)

    @pl.when(ci == nchunks - 1)
    def _():
        if not direct:
            ret_o[...] = ret_ref[...]
        for s in seqs:
            for h in range(H_D):
                gla_o[s, h] = jnp.transpose(gla_ref[s, h])


def _ret_tables(t, g, lv, pos0, nchunks):
    half = HEAD_C // 2
    inv = ROPE_BASE ** (-jnp.arange(half, dtype=F32) / half)
    pos = pos0 + jnp.arange(t * nchunks, dtype=F32).reshape(nchunks, 1, t)
    pos = jnp.broadcast_to(pos, (nchunks, g, t)).reshape(-1)
    ang = pos[:, None] * inv[None, :]
    cos, sin = jnp.cos(ang), jnp.sin(ang)
    cosf = jnp.concatenate([cos, cos], axis=-1)
    sinf = jnp.concatenate([-sin, sin], axis=-1)
    gam = 1.0 - np.exp2(-5.0 - np.arange(H_C, dtype=np.float64))
    ti = np.arange(t)
    d = ti[:, None] - ti[None, :]
    seg = np.where(d >= 0, gam[:, None, None] ** np.maximum(d, 0)[None], 0.0)
    seg = np.stack([np.kron(np.eye(g), s) for s in seg])
    pw = np.tile(gam[:, None] ** (ti + 1)[None], (1, g))
    dte = np.tile(np.where(ti < lv, gam[:, None] ** np.maximum(lv - 1 - ti, 0)[None], 0.0), (1, g))
    lanes = lambda a: jnp.asarray(np.broadcast_to(a[:, :, None], a.shape + (LANES,)), F32)
    return cosf, sinf, jnp.asarray(seg, F32), lanes(pw), lanes(dte)


def _odd_mixer(z, states, li, p, *, t, g, lv, pos0):
    nseq, length, _ = z.shape
    nchunks = length // t
    has_init = states is not None
    assert (lv == t or nchunks == 1) and nseq % g == 0
    tsub = min(GLA_SUB, t)
    n = g * t
    nsub = n if g > 1 else tsub
    cosf, sinf, seg, pw, dte = _ret_tables(t, g, lv, pos0, nchunks)
    full = lambda *shape: pl.BlockSpec(shape, lambda b, c: (0,) * len(shape))
    in_specs = [pl.BlockSpec((g, t, OD_PAD), lambda b, c: (b, c, 0))]
    args = [z]
    if has_init:
        in_specs += [pl.BlockSpec((None, g, H_C, HEAD_C, HEAD_C), lambda b, c: (li, b, 0, 0, 0)),
                     pl.BlockSpec((None, g, H_D, HEAD_DK, HEAD_DV), lambda b, c: (li, b, 0, 0, 0))]
        args += list(states)
    in_specs += [pl.BlockSpec((n, HEAD_C), lambda b, c: (c, 0)),
                 pl.BlockSpec((n, HEAD_C), lambda b, c: (c, 0)),
                 full(H_C, n, n), full(H_C, n, LANES), full(H_C, n, LANES),
                 full(1, D_MIX), full(LANES, H_D * HEAD_DK), full(1, H_D * HEAD_DK), full(1, D_MIX),
                 full(3, nsub, nsub)]
    args += [cosf, sinf, seg, pw, dte, p['ret_norm_w'], p['gate_up'], p['gate_b'], p['gla_norm_w'],
             _seq_selectors(g if g > 1 else 1, tsub)]
    ret_shape, ret_spec = _state_out(states and states[0], li, nseq, g, (H_C, HEAD_C, HEAD_C))
    gla_shape, gla_spec = _state_out(states and states[1], li, nseq, g, (H_D, HEAD_DK, HEAD_DV))
    out_shape = (jax.ShapeDtypeStruct((nseq, length, D_MODEL), BF16), ret_shape, gla_shape)
    out_specs = (pl.BlockSpec((g, t, D_MODEL), lambda b, c: (b, c, 0)), ret_spec, gla_spec)
    gs = 1 if (has_init and nchunks == 1) else g
    scratch = [pltpu.VMEM((gs, H_C, HEAD_C, HEAD_C), F32), pltpu.VMEM((g, H_D, HEAD_DV, HEAD_DK), F32),
               pltpu.VMEM((n, H_D * HEAD_DK), F32), pltpu.VMEM((n, D_MODEL), F32)]
    y, ret, gla = pl.pallas_call(
        functools.partial(_odd_kernel, t=t, tsub=tsub, g=g, nchunks=nchunks, lv=lv, has_init=has_init),
        grid=(nseq // g, nchunks),
        in_specs=in_specs, out_specs=out_specs, out_shape=out_shape, scratch_shapes=scratch,
        input_output_aliases={1: 1, 2: 2} if has_init else {},
        compiler_params=pltpu.CompilerParams(dimension_semantics=("arbitrary", "arbitrary"),
                                             vmem_limit_bytes=VMEM_LIMIT),
        name="odd_mixer",
    )(*args)
    return y, (ret, gla)


def _prep_proj(W):
    w = W['ev_w_in']
    n = w.shape[0]
    ev_in = jnp.concatenate([w[:, :, :RW_COLS], jnp.zeros((n, D_MODEL, RW_PAD - RW_COLS), F32),
                             w[:, :, RW_COLS:], jnp.zeros((n, D_MODEL, MB_PAD - MB_COLS), F32)], axis=2)
    w = W['od_w_in']
    n = w.shape[0]
    gd0 = OD_COLS - D_MIX - GLA_LORA
    od_in = jnp.concatenate([w[:, :, :gd0], w[:, :, gd0 + GLA_LORA:], w[:, :, gd0:gd0 + GLA_LORA],
                             jnp.zeros((n, D_MODEL, OD_PAD - OD_COLS), F32)], axis=2)
    return dict(ev_in=ev_in.astype(BF16), od_in=od_in.astype(BF16),
                ev_out=W['ev_w_out'].astype(BF16), od_out=W['od_w_out'].astype(BF16))


def _prep_even(i, W):
    mu = jnp.pad(W['rw_mu'][i], (0, RW_PAD - RW_COLS)).reshape(1, RW_PAD)
    rep = lambda v: jnp.repeat(v, HEAD_B)
    rows = [W['rw_w0'][i], W['rw_a0'][i], W['rw_k_k'][i], W['rw_k_a'][i], W['rw_r_k'][i].reshape(-1),
            W['rw_gn_w'][i], W['rw_gn_b'][i], rep(W['mb_d'][i]), W['mb_norm_w'][i]]
    vec = jnp.concatenate([jnp.stack(rows), jnp.zeros((EV_VEC_ROWS - len(rows), D_MIX), F32)], axis=0)
    wa = jnp.zeros((LANES, 2 * D_MIX), F32)
    wa = wa.at[:RW_LORA_W, :D_MIX].set(W['rw_w_up'][i]).at[RW_LORA_W:, D_MIX:].set(W['rw_a_up'][i])
    gup = jnp.pad(W['rw_g_up'][i], ((0, RW_PAD - 3 * D_MIX - LANES - RW_LORA_G), (0, 0)))
    pad_h = lambda v: jnp.pad(v, (0, LANES - H_B)).reshape(1, LANES)
    expand = np.zeros((LANES, D_MIX), np.float32)
    for h in range(H_B):
        expand[h, h * HEAD_B:(h + 1) * HEAD_B] = 1.0
    return dict(mu=mu, vec=vec, wa=wa.astype(BF16), gup=gup.astype(BF16), conv_w=W['mb_conv_w'][i],
                conv_b=W['mb_conv_b'][i].reshape(1, MB_CONV_CH), dt_bias=pad_h(W['mb_dt_bias'][i]),
                a_log=pad_h(W['mb_a_log'][i]), expand=jnp.asarray(expand, BF16))


def _prep_odd(i, W):
    gate_up = jnp.pad(W['gla_gate_up'][i], ((0, LANES - GLA_LORA), (0, 0)))
    return dict(ret_norm_w=W['ret_norm_w'][i].reshape(1, D_MIX), gate_up=gate_up.astype(BF16),
                gate_b=W['gla_gate_b'][i].reshape(1, H_D * HEAD_DK),
                gla_norm_w=W['gla_norm_w'][i].reshape(1, D_MIX))


def _trunk(h, mod, mc0, states, P, PW, FW, final_g, *, nseq, length, t, g, lv, pos0):
    outs = ([], [], [], [], [], [])
    padded = -(-length // t) * t
    carried = None if states is None else list(states)
    stacked_f32 = isinstance(FW, tuple)
    bf16_weights = []

    def ffn(h, mc, mod_base, k, final_g=None):
        if not stacked_f32:
            return _ffn(h, mod, mc, mod_base, FW[2 * mc['l'] + k], k, final_g)
        h, w16 = _ffn(h, mod, mc, mod_base, FW, k, final_g)
        bf16_weights.append(w16)
        return h

    for l in range(DEPTH):
        i = l // 2
        p = P[l]
        mc = dict(mc0, l=l)
        even = l % 2 == 0
        h = ffn(h, mc, 0, 0)
        z = _inproj(h, mod, mc, PW['ev_in' if even else 'od_in'], i).reshape(nseq, length, -1)
        if padded != length:
            z = jnp.pad(z, ((0, 0), (0, padded - length), (0, 0)))
        if even:
            y, new = _even_mixer(z, carried and tuple(carried[:4]), i, p, t=t, g=g, lv=lv)
            slots = (0, 1, 2, 3)
        else:
            y, new = _odd_mixer(z, carried and tuple(carried[4:]), i, p, t=t, g=g, lv=lv, pos0=pos0)
            slots = (4, 5)
        for k, n in zip(slots, new):
            if carried is not None and k in (1, 3, 4, 5):
                carried[k] = n
            else:
                outs[k].append(n)
        y = y[:, :length].reshape(nseq * length, D_MODEL)
        h = _outproj(y, PW['ev_out' if even else 'od_out'], i, h, mod, mc)
        h = ffn(h, mc, 6, 1, final_g=final_g if l == DEPTH - 1 else None)
    return h, tuple(jnp.stack(lst) if lst else carried[k] for k, lst in enumerate(outs)), bf16_weights


def kernel(x_prompt, x_sample, state_rwkv_shift, state_rwkv_wkv, state_mamba_conv, state_mamba_ssm,
           state_ret, state_gla, c_prompt, c_sample, ada_w, ada_b, ffn_wg, ffn_wu, ffn_wd, ev_w_in,
           ev_w_out, rw_mu, rw_w0, rw_w_up, rw_a0, rw_a_up, rw_g_up, rw_k_k, rw_k_a, rw_r_k, rw_gn_w,
           rw_gn_b, mb_conv_w, mb_conv_b, mb_dt_bias, mb_a_log, mb_d, mb_norm_w, od_w_in, od_w_out,
           ret_norm_w, gla_gate_up, gla_gate_b, gla_norm_w, final_g):
    W = dict(ev_w_in=ev_w_in, ev_w_out=ev_w_out, rw_mu=rw_mu, rw_w0=rw_w0, rw_w_up=rw_w_up,
             rw_a0=rw_a0, rw_a_up=rw_a_up, rw_g_up=rw_g_up, rw_k_k=rw_k_k, rw_k_a=rw_k_a,
             rw_r_k=rw_r_k, rw_gn_w=rw_gn_w, rw_gn_b=rw_gn_b, mb_conv_w=mb_conv_w,
             mb_conv_b=mb_conv_b, mb_dt_bias=mb_dt_bias, mb_a_log=mb_a_log, mb_d=mb_d,
             mb_norm_w=mb_norm_w, od_w_in=od_w_in, od_w_out=od_w_out, ret_norm_w=ret_norm_w,
             gla_gate_up=gla_gate_up, gla_gate_b=gla_gate_b, gla_norm_w=gla_norm_w)
    nb, seq, _ = x_prompt.shape
    db, dseq, _ = x_sample.shape
    P = [(_prep_even if l % 2 == 0 else _prep_odd)(l // 2, W) for l in range(DEPTH)]
    PW = _prep_proj(W)

    rows = db + nb
    rows_pad = -(-rows // SUBLANES) * SUBLANES
    c_all = jnp.concatenate([c_sample, c_prompt, jnp.zeros((rows_pad - rows, D_MODEL), F32)], axis=0)
    mod = _ada(c_all, ada_w, ada_b)

    states = (state_rwkv_shift, state_rwkv_wkv, state_mamba_conv, state_mamba_ssm, state_ret, state_gla)
    y_s, st_s, w16 = _trunk(x_sample.reshape(db * dseq, D_MODEL), mod, dict(row0=0, rps=dseq), states,
                            P, PW, (ffn_wg, ffn_wu, ffn_wd), final_g, nseq=db, length=dseq,
                            t=SAMPLE_CHUNK, g=SAMPLE_GROUP, lv=dseq, pos0=float(PAST_LEN))
    y_p, st_p, _ = _trunk(x_prompt.reshape(nb * seq, D_MODEL), mod, dict(row0=db, rps=seq), None,
                          P, PW, w16, final_g, nseq=nb, length=seq, t=PROMPT_CHUNK, g=1, lv=PROMPT_CHUNK,
                          pos0=0.0)
    return (y_p.reshape(nb, seq, D_MODEL), y_s.reshape(db, dseq, D_MODEL)) + st_p + st_s
```

```python
import functools
import math

import numpy as np
import jax
import jax.numpy as jnp
from jax import lax
from jax.experimental import pallas as pl
from jax.experimental.pallas import tpu as pltpu

F32 = jnp.float32
BF16 = jnp.bfloat16

D_MODEL = 2048
DEPTH = 4
PAST_LEN = 16384
N_MOD = 9
D_FF = 5632

D_MIX = 1024
H_A, HEAD_A = 16, 64
RW_LORA_W, RW_LORA_A, RW_LORA_G = 64, 64, 160
RW_COLS = 3 * D_MIX + RW_LORA_W + RW_LORA_A + RW_LORA_G
RW_GN_EPS = 64e-5
H_B, HEAD_B, D_STATE, N_GROUPS, CONV_W = 16, 64, 128, 2, 4
MB_CONV_CH = D_MIX + 2 * N_GROUPS * D_STATE
MB_COLS = D_MIX + MB_CONV_CH + H_B
H_C, HEAD_C = 8, 128
ROPE_BASE = 10000.0
H_D, HEAD_DK, HEAD_DV, GLA_LORA = 4, 128, 256, 16
GLA_NORMALIZER = 16.0
OD_COLS = 4 * D_MIX + 2 * H_D * HEAD_DK + 2 * D_MIX + GLA_LORA

LANES = 128
SUBLANES = 8
VMEM_LIMIT = 60 * 1024 * 1024

RW_PAD = 3456
ZM = RW_PAD
MB_PAD = 2688
EV_PAD = RW_PAD + MB_PAD
OD_PAD = 7296
OD_GD = 7168

PROMPT_CHUNK = 64
GLA_SUB = 32
SAMPLE_CHUNK = 8
SAMPLE_GROUP = 8

ROW_TILE = 512
NORM_ROWS = 256
FFN_ROW_TILE = 512


def _mm(a, b):
    return jnp.dot(a.astype(BF16), b.astype(BF16), preferred_element_type=F32)


def _mm_nt(a, b):
    return lax.dot_general(a.astype(BF16), b.astype(BF16), (((1,), (1,)), ((), ())),
                           preferred_element_type=F32)


def _mm_tn(a, b):
    return lax.dot_general(a.astype(BF16), b.astype(BF16), (((0,), (0,)), ((), ())),
                           preferred_element_type=F32)


def _split3(x):
    hi = x.astype(BF16)
    r = x - hi.astype(F32)
    mid = r.astype(BF16)
    lo = (r - mid.astype(F32)).astype(BF16)
    return hi, mid, lo


def _mm_sel(sel, x):
    s = sel.astype(BF16)
    hi, mid, lo = _split3(x)
    return (jnp.dot(s, hi, preferred_element_type=F32) + jnp.dot(s, mid, preferred_element_type=F32)
            + jnp.dot(s, lo, preferred_element_type=F32))


def _mm_tn_sel(x, sel):
    s = sel.astype(BF16)
    dn = (((0,), (0,)), ((), ()))
    hi, mid, lo = _split3(x)
    return (lax.dot_general(hi, s, dn, preferred_element_type=F32)
            + lax.dot_general(mid, s, dn, preferred_element_type=F32)
            + lax.dot_general(lo, s, dn, preferred_element_type=F32))


def _mm_xsel(x, sel):
    s = sel.astype(BF16)
    hi, mid, lo = _split3(x)
    return (jnp.dot(hi, s, preferred_element_type=F32) + jnp.dot(mid, s, preferred_element_type=F32)
            + jnp.dot(lo, s, preferred_element_type=F32))


def _sigmoid(x):
    return 1.0 / (1.0 + jnp.exp(-x))


def _silu(x):
    return x * _sigmoid(x)


def _softplus(x):
    return jnp.maximum(x, 0.0) + jnp.log(1.0 + jnp.exp(-jnp.abs(x)))


def _rms(x, eps):
    return x * lax.rsqrt(jnp.mean(x * x, axis=-1, keepdims=True) + eps)


def _mod_specs(mc, tm, idxs):
    l, rps, row0 = mc['l'], mc['rps'], mc['row0']
    specs = []
    for idx in idxs:
        if rps >= tm:
            specs.append(pl.BlockSpec(
                (None, None, SUBLANES, D_MODEL),
                lambda i, j, idx=idx: (l, idx, (row0 + i * tm // rps) // SUBLANES, 0)))
        else:
            assert row0 == 0
            specs.append(pl.BlockSpec((None, None, tm // rps, D_MODEL), lambda i, j, idx=idx: (l, idx, i, 0)))
    return specs


def _mod_val(ref, mc, tm):
    rps = mc['rps']
    if rps >= tm:
        r = (mc['row0'] + pl.program_id(0) * tm // rps) % SUBLANES
        return ref[pl.ds(r, 1), :]
    nseq = tm // rps
    row = lax.broadcasted_iota(jnp.int32, (tm, nseq), 0)
    first = lax.broadcasted_iota(jnp.int32, (tm, nseq), 1) * rps
    return _mm_sel((row >= first) & (row < first + rps), ref[...])


def _ada_kernel(c_ref, w_ref, b_ref, o_ref):
    x = _silu(c_ref[...]).astype(BF16)
    o_ref[...] = jnp.dot(x, w_ref[...].astype(BF16), preferred_element_type=F32) + b_ref[...]


def _ada(c, ada_w, ada_b):
    rows = c.shape[0]
    n = ada_w.shape[-1]
    tn = 1024
    per_mod = D_MODEL // tn
    return pl.pallas_call(
        _ada_kernel,
        grid=(DEPTH, n // tn),
        in_specs=[pl.BlockSpec((rows, D_MODEL), lambda l, j: (0, 0)),
                  pl.BlockSpec((None, D_MODEL, tn), lambda l, j: (l, 0, j)),
                  pl.BlockSpec((None, 1, tn), lambda l, j: (l, 0, j))],
        out_specs=pl.BlockSpec((None, None, rows, tn), lambda l, j: (l, j // per_mod, 0, j % per_mod)),
        out_shape=jax.ShapeDtypeStruct((DEPTH, N_MOD, rows, D_MODEL), F32),
        compiler_params=pltpu.CompilerParams(dimension_semantics=("arbitrary", "arbitrary"),
                                             vmem_limit_bytes=VMEM_LIMIT),
        name="ada",
    )(c, ada_w, ada_b.reshape(DEPTH, 1, n))


def _ffn_kernel(*refs, mc, tm, nj, final, emit):
    refs = list(refs)
    h_ref, sh_ref, sc_ref, g_ref, wg_ref, wu_ref, wd_ref = refs[:7]
    fin_ref = refs[7] if final else None
    o_ref = refs[7 + final]
    xn_ref = refs[-1]
    j = pl.program_id(1)

    if emit:
        wrefs = refs[8 + final:11 + final]
        for src, dst in zip((wg_ref, wu_ref, wd_ref), wrefs):
            dst[...] = src[...].astype(BF16)
    else:
        wrefs = (wg_ref, wu_ref, wd_ref)

    def swiglu(xn):
        gate = jnp.dot(xn, wrefs[0][...], preferred_element_type=F32)
        up = jnp.dot(xn, wrefs[1][...], preferred_element_type=F32)
        act = (_silu(gate) * up).astype(BF16)
        return jnp.dot(act, wrefs[2][...], preferred_element_type=F32)

    @pl.when(j == 0)
    def _():
        scale = 1.0 + _mod_val(sc_ref, mc, tm)
        shift = _mod_val(sh_ref, mc, tm)
        per_row = scale.shape[0] > 1
        for r in range(tm // NORM_ROWS):
            rows = slice(r * NORM_ROWS, (r + 1) * NORM_ROWS)
            xn = (_rms(h_ref[rows, :], 1e-6) * (scale[rows] if per_row else scale)
                  + (shift[rows] if per_row else shift)).astype(BF16)
            xn_ref[rows, :] = xn
            o_ref[rows, :] = swiglu(xn)

    @pl.when((j > 0) & (j < nj - 1))
    def _():
        o_ref[...] += swiglu(xn_ref[...])

    @pl.when(j == nj - 1)
    def _():
        gate = 0.5 * (1.0 + _mod_val(g_ref, mc, tm))
        per_row = gate.shape[0] > 1
        for r in range(tm // NORM_ROWS):
            rows = slice(r * NORM_ROWS, (r + 1) * NORM_ROWS)
            acc = o_ref[rows, :] + swiglu(xn_ref[rows, :])
            out = h_ref[rows, :] + (gate[rows] if per_row else gate) * acc
            if final:
                out = _rms(out, 1e-6) * fin_ref[...]
            o_ref[rows, :] = out


def _ffn(h, mod, mc, mod_base, w, k, final_g=None):
    m = h.shape[0]
    l = mc['l']
    emit = w[0].ndim == 4
    tm = min(FFN_ROW_TILE, m)
    tf = 256 if emit else 512
    nj = D_FF // tf
    final = final_g is not None
    in_specs = [pl.BlockSpec((tm, D_MODEL), lambda i, j: (i, 0))]
    in_specs += _mod_specs(mc, tm, (mod_base, mod_base + 1, mod_base + 2))
    col = pl.BlockSpec((D_MODEL, tf), lambda i, j: (0, j))
    row = pl.BlockSpec((tf, D_MODEL), lambda i, j: (j, 0))
    if emit:
        assert m == tm
        in_specs += [pl.BlockSpec((None, None, D_MODEL, tf), lambda i, j: (l, k, 0, j)),
                     pl.BlockSpec((None, None, D_MODEL, tf), lambda i, j: (l, k, 0, j)),
                     pl.BlockSpec((None, None, tf, D_MODEL), lambda i, j: (l, k, j, 0))]
    else:
        in_specs += [col, col, row]
    args = [h, mod, mod, mod, *w]
    if final:
        in_specs.append(pl.BlockSpec((1, D_MODEL), lambda i, j: (0, 0)))
        args.append(final_g.reshape(1, D_MODEL))
    out_specs = [pl.BlockSpec((tm, D_MODEL), lambda i, j: (i, 0))]
    out_shape = [jax.ShapeDtypeStruct((m, D_MODEL), F32)]
    if emit:
        out_specs += [col, col, row]
        out_shape += [jax.ShapeDtypeStruct((D_MODEL, D_FF), BF16), jax.ShapeDtypeStruct((D_MODEL, D_FF), BF16),
                      jax.ShapeDtypeStruct((D_FF, D_MODEL), BF16)]
    out = pl.pallas_call(
        functools.partial(_ffn_kernel, mc=mc, tm=tm, nj=nj, final=final, emit=emit),
        grid=(m // tm, nj),
        in_specs=in_specs, out_specs=out_specs, out_shape=out_shape,
        scratch_shapes=[pltpu.VMEM((tm, D_MODEL), BF16)],
        compiler_params=pltpu.CompilerParams(dimension_semantics=("arbitrary", "arbitrary"),
                                             vmem_limit_bytes=VMEM_LIMIT),
        name="ffn",
    )(*args)
    return (out[0], tuple(out[1:])) if emit else out[0]


def _inproj_kernel(h_ref, sh_ref, sc_ref, w_ref, o_ref, xn_ref, *, mc, tm):
    j = pl.program_id(1)

    @pl.when(j == 0)
    def _():
        scale = 1.0 + _mod_val(sc_ref, mc, tm)
        shift = _mod_val(sh_ref, mc, tm)
        per_row = scale.shape[0] > 1
        for r in range(tm // NORM_ROWS):
            rows = slice(r * NORM_ROWS, (r + 1) * NORM_ROWS)
            xn = (_rms(h_ref[rows, :], 1e-6) * (scale[rows] if per_row else scale)
                  + (shift[rows] if per_row else shift)).astype(BF16)
            xn_ref[rows, :] = xn
            o_ref[rows, :] = jnp.dot(xn, w_ref[...], preferred_element_type=F32)

    @pl.when(j > 0)
    def _():
        o_ref[...] = jnp.dot(xn_ref[...], w_ref[...], preferred_element_type=F32)


def _inproj(h, mod, mc, w, li):
    m = h.shape[0]
    n = w.shape[2]
    tm = min(ROW_TILE, m)
    tn = n // 3
    if n % (6 * LANES) == 0 and m % (2 * ROW_TILE) == 0:
        tm, tn = 2 * ROW_TILE, n // 6
    in_specs = [pl.BlockSpec((tm, D_MODEL), lambda i, j: (i, 0))]
    in_specs += _mod_specs(mc, tm, (3, 4))
    in_specs += [pl.BlockSpec((None, D_MODEL, tn), lambda i, j: (li, 0, j))]
    return pl.pallas_call(
        functools.partial(_inproj_kernel, mc=mc, tm=tm),
        grid=(m // tm, n // tn),
        in_specs=in_specs,
        out_specs=pl.BlockSpec((tm, tn), lambda i, j: (i, j)),
        out_shape=jax.ShapeDtypeStruct((m, n), F32),
        scratch_shapes=[pltpu.VMEM((tm, D_MODEL), BF16)],
        compiler_params=pltpu.CompilerParams(dimension_semantics=("arbitrary", "arbitrary"),
                                             vmem_limit_bytes=VMEM_LIMIT),
        name="inproj",
    )(h, mod, mod, w)


def _outproj_kernel(y_ref, w_ref, h_ref, g_ref, o_ref, *, mc, tm):
    mix = jnp.dot(y_ref[...], w_ref[...], preferred_element_type=F32)
    o_ref[...] = h_ref[...] + (1.0 + _mod_val(g_ref, mc, tm)) * mix


def _outproj(y, w, li, h, mod, mc):
    m = h.shape[0]
    tm = min(ROW_TILE, m)
    in_specs = [pl.BlockSpec((tm, D_MODEL), lambda i, j: (i, 0)),
                pl.BlockSpec((None, D_MODEL, D_MODEL), lambda i, j: (li, 0, 0)),
                pl.BlockSpec((tm, D_MODEL), lambda i, j: (i, 0))]
    in_specs += _mod_specs(mc, tm, (5,))
    return pl.pallas_call(
        functools.partial(_outproj_kernel, mc=mc, tm=tm),
        grid=(m // tm, 1),
        in_specs=in_specs,
        out_specs=pl.BlockSpec((tm, D_MODEL), lambda i, j: (i, 0)),
        out_shape=jax.ShapeDtypeStruct((m, D_MODEL), F32),
        compiler_params=pltpu.CompilerParams(dimension_semantics=("arbitrary", "arbitrary"),
                                             vmem_limit_bytes=VMEM_LIMIT),
        name="outproj",
    )(y, w, h, mod)


def _state_out(stacked, li, nseq, g, tail):
    zeros = (0,) * len(tail)
    if stacked is None:
        return (jax.ShapeDtypeStruct((nseq,) + tail, F32),
                pl.BlockSpec((g,) + tail, lambda b, c: (b,) + zeros))
    return (jax.ShapeDtypeStruct(stacked.shape, F32),
            pl.BlockSpec((None, g) + tail, lambda b, c: (li, b) + zeros))


def _seq_masks(g, t):
    n = g * t
    r = lax.broadcasted_iota(jnp.int32, (n, n), 0)
    c = lax.broadcasted_iota(jnp.int32, (n, n), 1)
    if g == 1:
        return r > c, r >= c
    same = (r // t) == (c // t)
    return same & (r > c), same & (r >= c)


def _seq_selectors(g, t):
    r = np.arange(g * t)[:, None]
    c = np.arange(g * t)[None, :]
    same = (r // t) == (c // t)
    incl = same & (r >= c)
    last = c == (r // t) * t + (t - 1)
    return jnp.asarray(np.stack([incl, last, incl.T]), BF16)


EV_W0, EV_A0, EV_KK, EV_KA, EV_RK, EV_GNW, EV_GNB, EV_D, EV_NW = range(9)
EV_VEC_ROWS = 16


def _even_kernel(*refs, t, g, nchunks, lv, has_init):
    if has_init:
        z_ref, shift0_ref, wkv0_ref, conv0_ref, ssm0_ref, *rest = refs
    else:
        z_ref, *rest = refs
    (mu_ref, vec_ref, wa_ref, gup_ref, cw_ref, cb_ref, dtb_ref, alog_ref, exp_ref, sel_ref,
     y_ref, shift_o, wkv_o, conv_o, ssm_o,
     carry_ref, wkv_ref, ext_ref, ssm_ref,
     r_s, kp_s, v_s, kk_s, a_s, c_s, ld_s, g_s, cl_s, y_s) = rest
    n = g * t
    ci = pl.program_id(1)
    masked = lv < t
    nlog = int(math.log2(t))
    seqs = range(g)
    hist = SUBLANES - (CONV_W - 1)
    direct = has_init and nchunks == 1
    wkv_src, wkv_dst = (wkv0_ref, wkv_o) if direct else (wkv_ref, wkv_ref)
    ssm_src, ssm_dst = (ssm0_ref, ssm_o) if direct else (ssm_ref, ssm_ref)

    @pl.when(ci == 0)
    def _():
        if has_init:
            carry_ref[:, :, 0:RW_COLS] = shift0_ref[...]
            carry_ref[:, :, RW_COLS:RW_PAD] = jnp.zeros((g, 1, RW_PAD - RW_COLS), F32)
            ext_ref[:, 0:hist, :] = jnp.zeros((g, hist, MB_CONV_CH), F32)
            ext_ref[:, hist:SUBLANES, :] = conv0_ref[...]
            if not direct:
                wkv_ref[...] = wkv0_ref[...]
                ssm_ref[...] = ssm0_ref[...]
        else:
            carry_ref[...] = jnp.zeros_like(carry_ref)
            wkv_ref[...] = jnp.zeros_like(wkv_ref)
            ext_ref[:, 0:SUBLANES, :] = jnp.zeros((g, SUBLANES, MB_CONV_CH), F32)
            ssm_ref[...] = jnp.zeros_like(ssm_ref)

    tok = lax.broadcasted_iota(jnp.int32, (n, 1), 0) % t
    valid = tok < lv
    strict, incl = _seq_masks(g, t)
    sel_incl, sel_last, sel_upper = sel_ref[0], sel_ref[1], sel_ref[2]

    def vec(i, lo=0, hi=D_MIX):
        return vec_ref[i:i + 1, lo:hi]

    def rows_of(x, s):
        return x[s * t:(s + 1) * t]

    def per_seq(fn):
        parts = [fn(s) for s in seqs]
        return parts[0] if g == 1 else jnp.concatenate(parts, axis=0)

    def shifted(a, b):
        cur = z_ref[:, :, a:b].reshape(n, b - a)
        carry = per_seq(lambda s: jnp.broadcast_to(carry_ref[s, :, a:b], (t, b - a)))
        prev = jnp.where(tok == 0, carry, pltpu.roll(cur, 1, axis=0))
        return cur + mu_ref[:, a:b] * (prev - cur)

    r_s[...] = shifted(0, D_MIX)
    k = shifted(D_MIX, 2 * D_MIX)
    v_s[...] = shifted(2 * D_MIX, 3 * D_MIX)
    lo = shifted(3 * D_MIX, 3 * D_MIX + LANES)
    lane = lax.broadcasted_iota(jnp.int32, (n, LANES), 1)
    wa = _mm(jnp.where(lane < RW_LORA_W, jnp.tanh(lo), lo), wa_ref[...])
    w_log = -_softplus(-(vec(EV_W0) + wa[:, 0:D_MIX])) - 0.5
    ld = -jnp.exp(w_log)
    a = _sigmoid(vec(EV_A0) + wa[:, D_MIX:2 * D_MIX])
    g_s[...] = _mm(_sigmoid(shifted(3 * D_MIX + LANES, RW_PAD)), gup_ref[...])
    kp = k * (1.0 + (a - 1.0) * vec(EV_KA))
    if masked:
        ld = jnp.where(valid, ld, 0.0)
        kp = jnp.where(valid, kp, 0.0)
        a = jnp.where(valid, a, 0.0)
    kk_s[...] = k * vec(EV_KK)
    kp_s[...] = kp
    a_s[...] = a
    ld_s[...] = ld
    c = _mm_sel(sel_incl, ld)
    c_s[...] = c
    cl_s[...] = _mm_sel(sel_last, c)
    carry_ref[...] = z_ref[:, lv - 1:lv, 0:RW_PAD]

    heads = range(H_A)
    sls = [slice(h * HEAD_A, (h + 1) * HEAD_A) for h in heads]
    rs = [r_s[:, sl] for sl in sls]
    kps = [kp_s[:, sl] for sl in sls]
    vs = [v_s[:, sl] for sl in sls]
    cs = [c_s[:, sl] for sl in sls]
    cls = [cl_s[:, sl] for sl in sls]
    kks, bs = [], []
    for sl in sls:
        kkr = kk_s[:, sl]
        nrm = jnp.sqrt(jnp.sum(kkr * kkr, axis=-1, keepdims=True))
        kks.append(kkr / jnp.maximum(nrm, 1e-12))
        bs.append(kks[-1] * a_s[:, sl])
    ecis = [jnp.exp(-c) for c in cs]
    kts = [kk * jnp.exp(c - ld_s[:, sl]) for kk, c, sl in zip(kks, cs, sls)]
    rts = [r * jnp.exp(c) for r, c in zip(rs, cs)]
    lhss = [jnp.concatenate([kt, rt], axis=0) for kt, rt in zip(kts, rts)]
    a1s = [_mm_nt(lhs, kp * eci) for lhs, kp, eci in zip(lhss, kps, ecis)]
    a2s = [_mm_nt(lhs, b * eci) for lhs, b, eci in zip(lhss, bs, ecis)]
    pms = [[_mm_nt(jnp.concatenate([rows_of(kt, s), rows_of(rt, s)], axis=0), wkv_src[s, h]) for s in seqs]
           for h, kt, rt in zip(heads, kts, rts)]
    pmk = [per_seq(lambda s: pm[s][0:t]) for pm in pms]
    pmr = [per_seq(lambda s: pm[s][t:2 * t]) for pm in pms]
    us = [pk + _mm(jnp.where(strict, a1[0:n], 0.0), v) for pk, a1, v in zip(pmk, a1s, vs)]
    ps = [jnp.where(strict, -a2[0:n], 0.0) for a2 in a2s]
    us = [u + _mm(p, u) for u, p in zip(us, ps)]
    for _ in range(nlog - 1):
        ps = [_mm(p, p) for p in ps]
        us = [u + _mm(p, u) for u, p in zip(us, ps)]
    ys = [pr + _mm(jnp.where(incl, a1[n:2 * n], 0.0), v) - _mm(jnp.where(incl, a2[n:2 * n], 0.0), u)
          for pr, a1, a2, v, u in zip(pmr, a1s, a2s, vs, us)]
    for h in heads:
        dec_end = jnp.exp(cls[h] - cs[h])
        kbar = kps[h] * dec_end
        bbar = bs[h] * dec_end
        for s in seqs:
            vu = jnp.concatenate([rows_of(vs[h], s), rows_of(us[h], s)], axis=0)
            kb = jnp.concatenate([rows_of(kbar, s), -rows_of(bbar, s)], axis=0)
            wkv_dst[s, h] = wkv_src[s, h] * jnp.exp(cls[h][s * t:s * t + 1]) + _mm_tn(vu, kb)
    for h, sl in zip(heads, sls):
        y, r, kp, v = ys[h], rs[h], kps[h], vs[h]
        mean = jnp.mean(y, axis=-1, keepdims=True)
        yc = y - mean
        var = jnp.mean(yc * yc, axis=-1, keepdims=True)
        yn = (yc * lax.rsqrt(var + RW_GN_EPS) * vec(EV_GNW, sl.start, sl.stop)
              + vec(EV_GNB, sl.start, sl.stop))
        bonus = jnp.sum(r * kp * vec(EV_RK, sl.start, sl.stop), axis=-1, keepdims=True) * v
        y_s[:, sl] = (yn + bonus) * g_s[:, sl]

    zg = z_ref[:, :, ZM:ZM + D_MIX].reshape(n, D_MIX)
    ext_ref[:, SUBLANES:SUBLANES + t, :] = z_ref[:, :, ZM + D_MIX:ZM + D_MIX + MB_CONV_CH]
    conv = cb_ref[...]
    for w in range(CONV_W):
        conv = conv + cw_ref[w:w + 1, :] * ext_ref[:, hist + w:hist + w + t, :].reshape(n, MB_CONV_CH)
    new_conv = ext_ref[:, SUBLANES + lv - (CONV_W - 1):SUBLANES + lv, :]
    ext_ref[:, hist:SUBLANES, :] = new_conv
    xbc = _silu(conv)
    xs = xbc[:, 0:D_MIX]
    dt = _softplus(z_ref[:, :, ZM + D_MIX + MB_CONV_CH:ZM + MB_PAD].reshape(n, LANES) + dtb_ref[...])
    logd = dt * (-jnp.exp(alog_ref[...]))
    if masked:
        logd = jnp.where(valid, logd, 0.0)
    c = _mm_sel(sel_incl, logd)
    c_t = _mm_tn_sel(logd, sel_upper)
    c_end = _mm_sel(sel_last, c)
    wide = _mm_xsel(jnp.concatenate([dt, c, c_end - c], axis=0), exp_ref[...])
    xdt = xs * wide[0:n]
    if masked:
        xdt = jnp.where(valid, xdt, 0.0)
    ec_w = jnp.exp(wide[n:2 * n])
    xdec = xdt * jnp.exp(wide[2 * n:3 * n])
    hpg = H_B // N_GROUPS
    bgs = [xbc[:, D_MIX + q * D_STATE:D_MIX + (q + 1) * D_STATE] for q in range(N_GROUPS)]
    cgs = [xbc[:, D_MIX + (N_GROUPS + q) * D_STATE:D_MIX + (N_GROUPS + q + 1) * D_STATE]
           for q in range(N_GROUPS)]
    scores = [_mm_nt(cg, bg) for cg, bg in zip(cgs, bgs)]
    segs = [jnp.where(incl, jnp.exp(jnp.where(incl, c[:, h:h + 1] - c_t[h:h + 1, :], 0.0)), 0.0)
            for h in range(H_B)]
    hsl = [slice(h * HEAD_B, (h + 1) * HEAD_B) for h in range(H_B)]
    intra = [_mm(scores[h // hpg] * segs[h], xdt[:, hsl[h]]) for h in range(H_B)]
    grp = lambda s, q: ssm_src[s, q * hpg:(q + 1) * hpg].reshape(hpg * HEAD_B, D_STATE)
    inter = [per_seq(lambda s: _mm_nt(rows_of(cgs[q], s), grp(s, q))) for q in range(N_GROUPS)]
    for q in range(N_GROUPS):
        for s in seqs:
            upd = _mm_tn(rows_of(xdec[:, q * hpg * HEAD_B:(q + 1) * hpg * HEAD_B], s), rows_of(bgs[q], s))
            last = s * t + t - 1
            for h in range(q * hpg, (q + 1) * hpg):
                ssm_dst[s, h] = (ssm_src[s, h] * jnp.exp(c[last:last + 1, h:h + 1])
                                 + upd[(h - q * hpg) * HEAD_B:(h - q * hpg + 1) * HEAD_B])
    o = jnp.concatenate(intra, axis=1) + jnp.concatenate(inter, axis=1) * ec_w
    yb = (o + vec(EV_D) * xs) * _silu(zg)
    y_s[:, D_MIX:2 * D_MIX] = _rms(yb, 1e-5) * vec(EV_NW)
    y_ref[...] = y_s[...].astype(BF16).reshape(g, t, D_MODEL)

    @pl.when(ci == nchunks - 1)
    def _():
        shift_o[...] = carry_ref[...]
        conv_o[...] = new_conv
        if not direct:
            wkv_o[...] = wkv_ref[...]
            ssm_o[...] = ssm_ref[...]


def _even_mixer(z, states, li, p, *, t, g, lv):
    nseq, length, _ = z.shape
    nchunks = length // t
    has_init = states is not None
    assert (lv == t or nchunks == 1) and nseq % g == 0
    full = lambda *shape: pl.BlockSpec(shape, lambda b, c: (0,) * len(shape))
    in_specs = [pl.BlockSpec((g, t, EV_PAD), lambda b, c: (b, c, 0))]
    args = [z]
    ssm_t = None
    if has_init:
        shift0, wkv0, conv0, ssm0 = states
        ssm_t = jnp.swapaxes(ssm0, 3, 4)
        in_specs += [pl.BlockSpec((None, g, 1, RW_COLS), lambda b, c: (li, b, 0, 0)),
                     pl.BlockSpec((None, g, H_A, HEAD_A, HEAD_A), lambda b, c: (li, b, 0, 0, 0)),
                     pl.BlockSpec((None, g, CONV_W - 1, MB_CONV_CH), lambda b, c: (li, b, 0, 0)),
                     pl.BlockSpec((None, g, H_B, HEAD_B, D_STATE), lambda b, c: (li, b, 0, 0, 0))]
        args += [shift0.reshape(-1, nseq, 1, RW_COLS), wkv0, conv0, ssm_t]
    in_specs += [full(1, RW_PAD), full(EV_VEC_ROWS, D_MIX), full(LANES, 2 * D_MIX),
                 full(RW_PAD - 3 * D_MIX - LANES, D_MIX), full(CONV_W, MB_CONV_CH),
                 full(1, MB_CONV_CH), full(1, LANES), full(1, LANES), full(LANES, D_MIX),
                 full(3, g * t, g * t)]
    args += [p['mu'], p['vec'], p['wa'], p['gup'], p['conv_w'], p['conv_b'], p['dt_bias'],
             p['a_log'], p['expand'], _seq_selectors(g, t)]
    wkv_shape, wkv_spec = _state_out(states and states[1], li, nseq, g, (H_A, HEAD_A, HEAD_A))
    ssm_shape, ssm_spec = _state_out(ssm_t, li, nseq, g, (H_B, HEAD_B, D_STATE))
    out_shape = (jax.ShapeDtypeStruct((nseq, length, D_MODEL), BF16),
                 jax.ShapeDtypeStruct((nseq, 1, RW_PAD), F32), wkv_shape,
                 jax.ShapeDtypeStruct((nseq, CONV_W - 1, MB_CONV_CH), F32), ssm_shape)
    out_specs = (pl.BlockSpec((g, t, D_MODEL), lambda b, c: (b, c, 0)),
                 pl.BlockSpec((g, 1, RW_PAD), lambda b, c: (b, 0, 0)), wkv_spec,
                 pl.BlockSpec((g, CONV_W - 1, MB_CONV_CH), lambda b, c: (b, 0, 0)), ssm_spec)
    n = g * t
    wide = lambda: pltpu.VMEM((n, D_MIX), F32)
    gs = 1 if (has_init and nchunks == 1) else g
    scratch = [pltpu.VMEM((g, 1, RW_PAD), F32), pltpu.VMEM((gs, H_A, HEAD_A, HEAD_A), F32),
               pltpu.VMEM((g, t + SUBLANES, MB_CONV_CH), F32), pltpu.VMEM((gs, H_B, HEAD_B, D_STATE), F32)]
    scratch += [wide() for _ in range(9)] + [pltpu.VMEM((n, D_MODEL), F32)]
    y, shift, wkv, conv, ssm = pl.pallas_call(
        functools.partial(_even_kernel, t=t, g=g, nchunks=nchunks, lv=lv, has_init=has_init),
        grid=(nseq // g, nchunks),
        in_specs=in_specs, out_specs=out_specs, out_shape=out_shape, scratch_shapes=scratch,
        input_output_aliases={2: 2, 4: 4} if has_init else {},
        compiler_params=pltpu.CompilerParams(dimension_semantics=("arbitrary", "arbitrary"),
                                             vmem_limit_bytes=VMEM_LIMIT),
        name="even_mixer",
    )(*args)
    return y, (shift[:, 0, :RW_COLS], wkv, conv, jnp.swapaxes(ssm, -1, -2))


def _odd_kernel(*refs, t, tsub, g, nchunks, lv, has_init):
    if has_init:
        z_ref, ret0_ref, gla0_ref, *rest = refs
    else:
        z_ref, *rest = refs
    (cos_ref, sin_ref, seg_ref, pw_ref, dte_ref, rnw_ref, gup_ref, gb_ref, gnw_ref, sel_ref,
     y_ref, ret_o, gla_o, ret_ref, gla_ref, la_s, y_s) = rest
    n = g * t
    ci = pl.program_id(1)
    masked = lv < t
    seqs = range(g)
    direct = has_init and nchunks == 1
    ret_src, ret_dst = (ret0_ref, ret_o) if direct else (ret_ref, ret_ref)

    @pl.when(ci == 0)
    def _():
        if has_init:
            if not direct:
                ret_ref[...] = ret0_ref[...]
            for s in seqs:
                for h in range(H_D):
                    gla_ref[s, h] = jnp.transpose(gla0_ref[s, h])
        else:
            ret_ref[...] = jnp.zeros_like(ret_ref)
            gla_ref[...] = jnp.zeros_like(gla_ref)

    valid = lax.broadcasted_iota(jnp.int32, (n, 1), 0) % t < lv

    def zcols(a, b):
        return z_ref[:, :, a:b].reshape(n, b - a)

    def per_seq(fn, rows):
        parts = [fn(s, slice(s * rows, (s + 1) * rows)) for s in seqs]
        return parts[0] if g == 1 else jnp.concatenate(parts, axis=0)

    cosf = cos_ref[...]
    sinf = sin_ref[...]

    def rotary(x):
        return x * cosf + pltpu.roll(x, HEAD_C // 2, axis=1) * sinf

    hc = range(H_C)
    qs = [rotary(zcols(h * HEAD_C, (h + 1) * HEAD_C)) * HEAD_C ** -0.5 for h in hc]
    ks = [rotary(zcols(D_MIX + h * HEAD_C, D_MIX + (h + 1) * HEAD_C)) for h in hc]
    if masked:
        ks = [jnp.where(valid, k, 0.0) for k in ks]
    vs = [zcols(2 * D_MIX + h * HEAD_C, 2 * D_MIX + (h + 1) * HEAD_C) for h in hc]
    sc = [_mm_nt(q, k) * seg_ref[h] for h, q, k in zip(hc, qs, ks)]
    inter = [per_seq(lambda s, rw: _mm(qs[h][rw], ret_src[s, h]), t) * pw_ref[h] for h in hc]
    outs = [_mm(s_, v) + i for s_, v, i in zip(sc, vs, inter)]
    for h in hc:
        gamma = 1.0 - 2.0 ** (-5.0 - h)
        kd = ks[h] * dte_ref[h]
        for s in seqs:
            rw = slice(s * t, (s + 1) * t)
            ret_dst[s, h] = ret_src[s, h] * gamma ** lv + _mm_tn(kd[rw], vs[h][rw])
    for h in hc:
        sl = slice(h * HEAD_C, (h + 1) * HEAD_C)
        rg = zcols(3 * D_MIX + h * HEAD_C, 3 * D_MIX + (h + 1) * HEAD_C)
        y_s[:, sl] = _rms(outs[h], 1e-6) * rnw_ref[:, sl] * _silu(rg)

    gq0 = 4 * D_MIX
    gk0 = gq0 + H_D * HEAD_DK
    gv0 = gk0 + H_D * HEAD_DK
    gr0 = gv0 + D_MIX
    x = _mm(zcols(OD_GD, OD_GD + LANES), gup_ref[...]) + gb_ref[...]
    la = -_softplus(-x) * (1.0 / GLA_NORMALIZER)
    if masked:
        la = jnp.where(valid, la, 0.0)
    la_s[...] = la
    assert g == 1 or tsub == t
    incl = _seq_masks(g, tsub)[1]
    hd = range(H_D)
    for u in range(t // tsub):
        rs = slice(0, n) if g > 1 else slice(u * tsub, (u + 1) * tsub)

        def cols(a, b, rs=rs):
            return z_ref[0, rs, a:b] if g == 1 else zcols(a, b)

        cum_all = _mm_sel(sel_ref[0], la_s[rs, :])
        cums = [cum_all[:, h * HEAD_DK:(h + 1) * HEAD_DK] for h in hd]
        qes = [cols(gq0 + h * HEAD_DK, gq0 + (h + 1) * HEAD_DK) * HEAD_DK ** -0.5 * jnp.exp(cums[h])
               for h in hd]
        ks = [cols(gk0 + h * HEAD_DK, gk0 + (h + 1) * HEAD_DK) for h in hd]
        if masked:
            ks = [jnp.where(valid[rs], k, 0.0) for k in ks]
        vs = [cols(gv0 + h * HEAD_DV, gv0 + (h + 1) * HEAD_DV) for h in hd]
        scores = [jnp.where(incl, _mm_nt(qes[h], ks[h] * jnp.exp(-cums[h])), 0.0) for h in hd]
        inter = [per_seq(lambda s, rw: _mm_nt(qes[h][rw], gla_ref[s, h]), tsub) for h in hd]
        outs = [_mm(scores[h], vs[h]) + inter[h] for h in hd]
        for h in hd:
            for s in seqs:
                rw = slice(s * tsub, (s + 1) * tsub)
                cum = cums[h][rw]
                last = cum[tsub - 1:tsub]
                gla_ref[s, h] = (gla_ref[s, h] * jnp.exp(last)
                                 + _mm_tn(vs[h][rw], ks[h][rw] * jnp.exp(last - cum)))
        for h in hd:
            gr = cols(gr0 + h * HEAD_DV, gr0 + (h + 1) * HEAD_DV)
            ys = slice(D_MIX + h * HEAD_DV, D_MIX + (h + 1) * HEAD_DV)
            y_s[rs, ys] = _rms(outs[h], 1e-6) * gnw_ref[:, h * HEAD_DV:(h + 1) * HEAD_DV] * _silu(gr)
    y_ref[...] = y_s[...].reshape(g, t, D_MODEL).astype(BF16)

    @pl.when(ci == nchunks - 1)
    def _():
        if not direct:
            ret_o[...] = ret_ref[...]
        for s in seqs:
            for h in range(H_D):
                gla_o[s, h] = jnp.transpose(gla_ref[s, h])


def _ret_tables(t, g, lv, pos0, nchunks):
    half = HEAD_C // 2
    inv = ROPE_BASE ** (-jnp.arange(half, dtype=F32) / half)
    pos = pos0 + jnp.arange(t * nchunks, dtype=F32).reshape(nchunks, 1, t)
    pos = jnp.broadcast_to(pos, (nchunks, g, t)).reshape(-1)
    ang = pos[:, None] * inv[None, :]
    cos, sin = jnp.cos(ang), jnp.sin(ang)
    cosf = jnp.concatenate([cos, cos], axis=-1)
    sinf = jnp.concatenate([-sin, sin], axis=-1)
    gam = 1.0 - np.exp2(-5.0 - np.arange(H_C, dtype=np.float64))
    ti = np.arange(t)
    d = ti[:, None] - ti[None, :]
    seg = np.where(d >= 0, gam[:, None, None] ** np.maximum(d, 0)[None], 0.0)
    seg = np.stack([np.kron(np.eye(g), s) for s in seg])
    pw = np.tile(gam[:, None] ** (ti + 1)[None], (1, g))
    dte = np.tile(np.where(ti < lv, gam[:, None] ** np.maximum(lv - 1 - ti, 0)[None], 0.0), (1, g))
    lanes = lambda a: jnp.asarray(np.broadcast_to(a[:, :, None], a.shape + (LANES,)), F32)
    return cosf, sinf, jnp.asarray(seg, F32), lanes(pw), lanes(dte)


def _odd_mixer(z, states, li, p, *, t, g, lv, pos0):
    nseq, length, _ = z.shape
    nchunks = length // t
    has_init = states is not None
    assert (lv == t or nchunks == 1) and nseq % g == 0
    tsub = min(GLA_SUB, t)
    n = g * t
    nsub = n if g > 1 else tsub
    cosf, sinf, seg, pw, dte = _ret_tables(t, g, lv, pos0, nchunks)
    full = lambda *shape: pl.BlockSpec(shape, lambda b, c: (0,) * len(shape))
    in_specs = [pl.BlockSpec((g, t, OD_PAD), lambda b, c: (b, c, 0))]
    args = [z]
    if has_init:
        in_specs += [pl.BlockSpec((None, g, H_C, HEAD_C, HEAD_C), lambda b, c: (li, b, 0, 0, 0)),
                     pl.BlockSpec((None, g, H_D, HEAD_DK, HEAD_DV), lambda b, c: (li, b, 0, 0, 0))]
        args += list(states)
    in_specs += [pl.BlockSpec((n, HEAD_C), lambda b, c: (c, 0)),
                 pl.BlockSpec((n, HEAD_C), lambda b, c: (c, 0)),
                 full(H_C, n, n), full(H_C, n, LANES), full(H_C, n, LANES),
                 full(1, D_MIX), full(LANES, H_D * HEAD_DK), full(1, H_D * HEAD_DK), full(1, D_MIX),
                 full(3, nsub, nsub)]
    args += [cosf, sinf, seg, pw, dte, p['ret_norm_w'], p['gate_up'], p['gate_b'], p['gla_norm_w'],
             _seq_selectors(g if g > 1 else 1, tsub)]
    ret_shape, ret_spec = _state_out(states and states[0], li, nseq, g, (H_C, HEAD_C, HEAD_C))
    gla_shape, gla_spec = _state_out(states and states[1], li, nseq, g, (H_D, HEAD_DK, HEAD_DV))
    out_shape = (jax.ShapeDtypeStruct((nseq, length, D_MODEL), BF16), ret_shape, gla_shape)
    out_specs = (pl.BlockSpec((g, t, D_MODEL), lambda b, c: (b, c, 0)), ret_spec, gla_spec)
    gs = 1 if (has_init and nchunks == 1) else g
    scratch = [pltpu.VMEM((gs, H_C, HEAD_C, HEAD_C), F32), pltpu.VMEM((g, H_D, HEAD_DV, HEAD_DK), F32),
               pltpu.VMEM((n, H_D * HEAD_DK), F32), pltpu.VMEM((n, D_MODEL), F32)]
    y, ret, gla = pl.pallas_call(
        functools.partial(_odd_kernel, t=t, tsub=tsub, g=g, nchunks=nchunks, lv=lv, has_init=has_init),
        grid=(nseq // g, nchunks),
        in_specs=in_specs, out_specs=out_specs, out_shape=out_shape, scratch_shapes=scratch,
        input_output_aliases={1: 1, 2: 2} if has_init else {},
        compiler_params=pltpu.CompilerParams(dimension_semantics=("arbitrary", "arbitrary"),
                                             vmem_limit_bytes=VMEM_LIMIT),
        name="odd_mixer",
    )(*args)
    return y, (ret, gla)


def _prep_proj(W):
    w = W['ev_w_in']
    n = w.shape[0]
    ev_in = jnp.concatenate([w[:, :, :RW_COLS], jnp.zeros((n, D_MODEL, RW_PAD - RW_COLS), F32),
                             w[:, :, RW_COLS:], jnp.zeros((n, D_MODEL, MB_PAD - MB_COLS), F32)], axis=2)
    w = W['od_w_in']
    n = w.shape[0]
    gd0 = OD_COLS - D_MIX - GLA_LORA
    od_in = jnp.concatenate([w[:, :, :gd0], w[:, :, gd0 + GLA_LORA:], w[:, :, gd0:gd0 + GLA_LORA],
                             jnp.zeros((n, D_MODEL, OD_PAD - OD_COLS), F32)], axis=2)
    return dict(ev_in=ev_in.astype(BF16), od_in=od_in.astype(BF16),
                ev_out=W['ev_w_out'].astype(BF16), od_out=W['od_w_out'].astype(BF16))


def _prep_even(i, W):
    mu = jnp.pad(W['rw_mu'][i], (0, RW_PAD - RW_COLS)).reshape(1, RW_PAD)
    rep = lambda v: jnp.repeat(v, HEAD_B)
    rows = [W['rw_w0'][i], W['rw_a0'][i], W['rw_k_k'][i], W['rw_k_a'][i], W['rw_r_k'][i].reshape(-1),
            W['rw_gn_w'][i], W['rw_gn_b'][i], rep(W['mb_d'][i]), W['mb_norm_w'][i]]
    vec = jnp.concatenate([jnp.stack(rows), jnp.zeros((EV_VEC_ROWS - len(rows), D_MIX), F32)], axis=0)
    wa = jnp.zeros((LANES, 2 * D_MIX), F32)
    wa = wa.at[:RW_LORA_W, :D_MIX].set(W['rw_w_up'][i]).at[RW_LORA_W:, D_MIX:].set(W['rw_a_up'][i])
    gup = jnp.pad(W['rw_g_up'][i], ((0, RW_PAD - 3 * D_MIX - LANES - RW_LORA_G), (0, 0)))
    pad_h = lambda v: jnp.pad(v, (0, LANES - H_B)).reshape(1, LANES)
    expand = np.zeros((LANES, D_MIX), np.float32)
    for h in range(H_B):
        expand[h, h * HEAD_B:(h + 1) * HEAD_B] = 1.0
    return dict(mu=mu, vec=vec, wa=wa.astype(BF16), gup=gup.astype(BF16), conv_w=W['mb_conv_w'][i],
                conv_b=W['mb_conv_b'][i].reshape(1, MB_CONV_CH), dt_bias=pad_h(W['mb_dt_bias'][i]),
                a_log=pad_h(W['mb_a_log'][i]), expand=jnp.asarray(expand, BF16))


def _prep_odd(i, W):
    gate_up = jnp.pad(W['gla_gate_up'][i], ((0, LANES - GLA_LORA), (0, 0)))
    return dict(ret_norm_w=W['ret_norm_w'][i].reshape(1, D_MIX), gate_up=gate_up.astype(BF16),
                gate_b=W['gla_gate_b'][i].reshape(1, H_D * HEAD_DK),
                gla_norm_w=W['gla_norm_w'][i].reshape(1, D_MIX))


def _trunk(h, mod, mc0, states, P, PW, FW, final_g, *, nseq, length, t, g, lv, pos0):
    outs = ([], [], [], [], [], [])
    padded = -(-length // t) * t
    carried = None if states is None else list(states)
    stacked_f32 = isinstance(FW, tuple)
    bf16_weights = []

    def ffn(h, mc, mod_base, k, final_g=None):
        if not stacked_f32:
            return _ffn(h, mod, mc, mod_base, FW[2 * mc['l'] + k], k, final_g)
        h, w16 = _ffn(h, mod, mc, mod_base, FW, k, final_g)
        bf16_weights.append(w16)
        return h

    for l in range(DEPTH):
        i = l // 2
        p = P[l]
        mc = dict(mc0, l=l)
        even = l % 2 == 0
        h = ffn(h, mc, 0, 0)
        z = _inproj(h, mod, mc, PW['ev_in' if even else 'od_in'], i).reshape(nseq, length, -1)
        if padded != length:
            z = jnp.pad(z, ((0, 0), (0, padded - length), (0, 0)))
        if even:
            y, new = _even_mixer(z, carried and tuple(carried[:4]), i, p, t=t, g=g, lv=lv)
            slots = (0, 1, 2, 3)
        else:
            y, new = _odd_mixer(z, carried and tuple(carried[4:]), i, p, t=t, g=g, lv=lv, pos0=pos0)
            slots = (4, 5)
        for k, n in zip(slots, new):
            if carried is not None and k in (1, 3, 4, 5):
                carried[k] = n
            else:
                outs[k].append(n)
        y = y[:, :length].reshape(nseq * length, D_MODEL)
        h = _outproj(y, PW['ev_out' if even else 'od_out'], i, h, mod, mc)
        h = ffn(h, mc, 6, 1, final_g=final_g if l == DEPTH - 1 else None)
    return h, tuple(jnp.stack(lst) if lst else carried[k] for k, lst in enumerate(outs)), bf16_weights


def kernel(x_prompt, x_sample, state_rwkv_shift, state_rwkv_wkv, state_mamba_conv, state_mamba_ssm,
           state_ret, state_gla, c_prompt, c_sample, ada_w, ada_b, ffn_wg, ffn_wu, ffn_wd, ev_w_in,
           ev_w_out, rw_mu, rw_w0, rw_w_up, rw_a0, rw_a_up, rw_g_up, rw_k_k, rw_k_a, rw_r_k, rw_gn_w,
           rw_gn_b, mb_conv_w, mb_conv_b, mb_dt_bias, mb_a_log, mb_d, mb_norm_w, od_w_in, od_w_out,
           ret_norm_w, gla_gate_up, gla_gate_b, gla_norm_w, final_g):
    W = dict(ev_w_in=ev_w_in, ev_w_out=ev_w_out, rw_mu=rw_mu, rw_w0=rw_w0, rw_w_up=rw_w_up,
             rw_a0=rw_a0, rw_a_up=rw_a_up, rw_g_up=rw_g_up, rw_k_k=rw_k_k, rw_k_a=rw_k_a,
             rw_r_k=rw_r_k, rw_gn_w=rw_gn_w, rw_gn_b=rw_gn_b, mb_conv_w=mb_conv_w,
             mb_conv_b=mb_conv_b, mb_dt_bias=mb_dt_bias, mb_a_log=mb_a_log, mb_d=mb_d,
             mb_norm_w=mb_norm_w, od_w_in=od_w_in, od_w_out=od_w_out, ret_norm_w=ret_norm_w,
             gla_gate_up=gla_gate_up, gla_gate_b=gla_gate_b, gla_norm_w=gla_norm_w)
    nb, seq, _ = x_prompt.shape
    db, dseq, _ = x_sample.shape
    P = [(_prep_even if l % 2 == 0 else _prep_odd)(l // 2, W) for l in range(DEPTH)]
    PW = _prep_proj(W)

    rows = db + nb
    rows_pad = -(-rows // SUBLANES) * SUBLANES
    c_all = jnp.concatenate([c_sample, c_prompt, jnp.zeros((rows_pad - rows, D_MODEL), F32)], axis=0)
    mod = _ada(c_all, ada_w, ada_b)

    states = (state_rwkv_shift, state_rwkv_wkv, state_mamba_conv, state_mamba_ssm, state_ret, state_gla)
    y_s, st_s, w16 = _trunk(x_sample.reshape(db * dseq, D_MODEL), mod, dict(row0=0, rps=dseq), states,
                            P, PW, (ffn_wg, ffn_wu, ffn_wd), final_g, nseq=db, length=dseq,
                            t=SAMPLE_CHUNK, g=SAMPLE_GROUP, lv=dseq, pos0=float(PAST_LEN))
    y_p, st_p, _ = _trunk(x_prompt.reshape(nb * seq, D_MODEL), mod, dict(row0=db, rps=seq), None,
                          P, PW, w16, final_g, nseq=nb, length=seq, t=PROMPT_CHUNK, g=1, lv=PROMPT_CHUNK,
                          pos0=0.0)
    return (y_p.reshape(nb, seq, D_MODEL), y_s.reshape(db, dseq, D_MODEL)) + st_p + st_s
```

```python
import functools
import math

import numpy as np
import jax
import jax.numpy as jnp
from jax import lax
from jax.experimental import pallas as pl
from jax.experimental.pallas import tpu as pltpu

F32 = jnp.float32
BF16 = jnp.bfloat16

D_MODEL = 2048
DEPTH = 4
PAST_LEN = 16384
N_MOD = 9
D_FF = 5632

D_MIX = 1024
H_A, HEAD_A = 16, 64
RW_LORA_W, RW_LORA_A, RW_LORA_G = 64, 64, 160
RW_COLS = 3 * D_MIX + RW_LORA_W + RW_LORA_A + RW_LORA_G
RW_GN_EPS = 64e-5
H_B, HEAD_B, D_STATE, N_GROUPS, CONV_W = 16, 64, 128, 2, 4
MB_CONV_CH = D_MIX + 2 * N_GROUPS * D_STATE
MB_COLS = D_MIX + MB_CONV_CH + H_B
H_C, HEAD_C = 8, 128
ROPE_BASE = 10000.0
H_D, HEAD_DK, HEAD_DV, GLA_LORA = 4, 128, 256, 16
GLA_NORMALIZER = 16.0
OD_COLS = 4 * D_MIX + 2 * H_D * HEAD_DK + 2 * D_MIX + GLA_LORA

LANES = 128
SUBLANES = 8
VMEM_LIMIT = 60 * 1024 * 1024

RW_PAD = 3456
ZM = RW_PAD
MB_PAD = 2688
EV_PAD = RW_PAD + MB_PAD
OD_PAD = 7296
OD_GD = 7168

PROMPT_CHUNK = 64
GLA_SUB = 32
SAMPLE_CHUNK = 8
SAMPLE_GROUP = 8

ROW_TILE = 512
NORM_ROWS = 256
FFN_ROW_TILE = 512


def _mm(a, b):
    return jnp.dot(a.astype(BF16), b.astype(BF16), preferred_element_type=F32)


def _mm_nt(a, b):
    return lax.dot_general(a.astype(BF16), b.astype(BF16), (((1,), (1,)), ((), ())),
                           preferred_element_type=F32)


def _mm_tn(a, b):
    return lax.dot_general(a.astype(BF16), b.astype(BF16), (((0,), (0,)), ((), ())),
                           preferred_element_type=F32)


def _split3(x):
    hi = x.astype(BF16)
    r = x - hi.astype(F32)
    mid = r.astype(BF16)
    lo = (r - mid.astype(F32)).astype(BF16)
    return hi, mid, lo


def _mm_sel(sel, x):
    s = sel.astype(BF16)
    hi, mid, lo = _split3(x)
    return (jnp.dot(s, hi, preferred_element_type=F32) + jnp.dot(s, mid, preferred_element_type=F32)
            + jnp.dot(s, lo, preferred_element_type=F32))


def _mm_tn_sel(x, sel):
    s = sel.astype(BF16)
    dn = (((0,), (0,)), ((), ()))
    hi, mid, lo = _split3(x)
    return (lax.dot_general(hi, s, dn, preferred_element_type=F32)
            + lax.dot_general(mid, s, dn, preferred_element_type=F32)
            + lax.dot_general(lo, s, dn, preferred_element_type=F32))


def _mm_xsel(x, sel):
    s = sel.astype(BF16)
    hi, mid, lo = _split3(x)
    return (jnp.dot(hi, s, preferred_element_type=F32) + jnp.dot(mid, s, preferred_element_type=F32)
            + jnp.dot(lo, s, preferred_element_type=F32))


def _sigmoid(x):
    return 1.0 / (1.0 + jnp.exp(-x))


def _silu(x):
    return x * _sigmoid(x)


def _softplus(x):
    return jnp.maximum(x, 0.0) + jnp.log(1.0 + jnp.exp(-jnp.abs(x)))


def _rms(x, eps):
    return x * lax.rsqrt(jnp.mean(x * x, axis=-1, keepdims=True) + eps)


def _mod_specs(mc, tm, idxs):
    l, rps, row0 = mc['l'], mc['rps'], mc['row0']
    specs = []
    for idx in idxs:
        if rps >= tm:
            specs.append(pl.BlockSpec(
                (None, None, SUBLANES, D_MODEL),
                lambda i, j, idx=idx: (l, idx, (row0 + i * tm // rps) // SUBLANES, 0)))
        else:
            assert row0 == 0
            specs.append(pl.BlockSpec((None, None, tm // rps, D_MODEL), lambda i, j, idx=idx: (l, idx, i, 0)))
    return specs


def _mod_val(ref, mc, tm):
    rps = mc['rps']
    if rps >= tm:
        r = (mc['row0'] + pl.program_id(0) * tm // rps) % SUBLANES
        return ref[pl.ds(r, 1), :]
    nseq = tm // rps
    row = lax.broadcasted_iota(jnp.int32, (tm, nseq), 0)
    first = lax.broadcasted_iota(jnp.int32, (tm, nseq), 1) * rps
    return _mm_sel((row >= first) & (row < first + rps), ref[...])


def _ada_kernel(c_ref, w_ref, b_ref, o_ref):
    x = _silu(c_ref[...]).astype(BF16)
    o_ref[...] = jnp.dot(x, w_ref[...].astype(BF16), preferred_element_type=F32) + b_ref[...]


def _ada(c, ada_w, ada_b):
    rows = c.shape[0]
    n = ada_w.shape[-1]
    tn = 1024
    per_mod = D_MODEL // tn
    return pl.pallas_call(
        _ada_kernel,
        grid=(DEPTH, n // tn),
        in_specs=[pl.BlockSpec((rows, D_MODEL), lambda l, j: (0, 0)),
                  pl.BlockSpec((None, D_MODEL, tn), lambda l, j: (l, 0, j)),
                  pl.BlockSpec((None, 1, tn), lambda l, j: (l, 0, j))],
        out_specs=pl.BlockSpec((None, None, rows, tn), lambda l, j: (l, j // per_mod, 0, j % per_mod)),
        out_shape=jax.ShapeDtypeStruct((DEPTH, N_MOD, rows, D_MODEL), F32),
        compiler_params=pltpu.CompilerParams(dimension_semantics=("arbitrary", "arbitrary"),
                                             vmem_limit_bytes=VMEM_LIMIT),
        name="ada",
    )(c, ada_w, ada_b.reshape(DEPTH, 1, n))


def _ffn_kernel(*refs, mc, tm, nj, final, emit):
    refs = list(refs)
    h_ref, sh_ref, sc_ref, g_ref, wg_ref, wu_ref, wd_ref = refs[:7]
    fin_ref = refs[7] if final else None
    o_ref = refs[7 + final]
    xn_ref = refs[-1]
    j = pl.program_id(1)

    if emit:
        wrefs = refs[8 + final:11 + final]
        for src, dst in zip((wg_ref, wu_ref, wd_ref), wrefs):
            dst[...] = src[...].astype(BF16)
    else:
        wrefs = (wg_ref, wu_ref, wd_ref)

    def swiglu(xn):
        gate = jnp.dot(xn, wrefs[0][...], preferred_element_type=F32)
        up = jnp.dot(xn, wrefs[1][...], preferred_element_type=F32)
        act = (_silu(gate) * up).astype(BF16)
        return jnp.dot(act, wrefs[2][...], preferred_element_type=F32)

    @pl.when(j == 0)
    def _():
        scale = 1.0 + _mod_val(sc_ref, mc, tm)
        shift = _mod_val(sh_ref, mc, tm)
        per_row = scale.shape[0] > 1
        for r in range(tm // NORM_ROWS):
            rows = slice(r * NORM_ROWS, (r + 1) * NORM_ROWS)
            xn = (_rms(h_ref[rows, :], 1e-6) * (scale[rows] if per_row else scale)
                  + (shift[rows] if per_row else shift)).astype(BF16)
            xn_ref[rows, :] = xn
            o_ref[rows, :] = swiglu(xn)

    @pl.when((j > 0) & (j < nj - 1))
    def _():
        o_ref[...] += swiglu(xn_ref[...])

    @pl.when(j == nj - 1)
    def _():
        gate = 0.5 * (1.0 + _mod_val(g_ref, mc, tm))
        per_row = gate.shape[0] > 1
        for r in range(tm // NORM_ROWS):
            rows = slice(r * NORM_ROWS, (r + 1) * NORM_ROWS)
            acc = o_ref[rows, :] + swiglu(xn_ref[rows, :])
            out = h_ref[rows, :] + (gate[rows] if per_row else gate) * acc
            if final:
                out = _rms(out, 1e-6) * fin_ref[...]
            o_ref[rows, :] = out


def _ffn(h, mod, mc, mod_base, w, k, final_g=None):
    m = h.shape[0]
    l = mc['l']
    emit = w[0].ndim == 4
    tm = min(FFN_ROW_TILE, m)
    tf = 256 if emit else 512
    nj = D_FF // tf
    final = final_g is not None
    in_specs = [pl.BlockSpec((tm, D_MODEL), lambda i, j: (i, 0))]
    in_specs += _mod_specs(mc, tm, (mod_base, mod_base + 1, mod_base + 2))
    col = pl.BlockSpec((D_MODEL, tf), lambda i, j: (0, j))
    row = pl.BlockSpec((tf, D_MODEL), lambda i, j: (j, 0))
    if emit:
        assert m == tm
        in_specs += [pl.BlockSpec((None, None, D_MODEL, tf), lambda i, j: (l, k, 0, j)),
                     pl.BlockSpec((None, None, D_MODEL, tf), lambda i, j: (l, k, 0, j)),
                     pl.BlockSpec((None, None, tf, D_MODEL), lambda i, j: (l, k, j, 0))]
    else:
        in_specs += [col, col, row]
    args = [h, mod, mod, mod, *w]
    if final:
        in_specs.append(pl.BlockSpec((1, D_MODEL), lambda i, j: (0, 0)))
        args.append(final_g.reshape(1, D_MODEL))
    out_specs = [pl.BlockSpec((tm, D_MODEL), lambda i, j: (i, 0))]
    out_shape = [jax.ShapeDtypeStruct((m, D_MODEL), F32)]
    if emit:
        out_specs += [col, col, row]
        out_shape += [jax.ShapeDtypeStruct((D_MODEL, D_FF), BF16), jax.ShapeDtypeStruct((D_MODEL, D_FF), BF16),
                      jax.ShapeDtypeStruct((D_FF, D_MODEL), BF16)]
    out = pl.pallas_call(
        functools.partial(_ffn_kernel, mc=mc, tm=tm, nj=nj, final=final, emit=emit),
        grid=(m // tm, nj),
        in_specs=in_specs, out_specs=out_specs, out_shape=out_shape,
        scratch_shapes=[pltpu.VMEM((tm, D_MODEL), BF16)],
        compiler_params=pltpu.CompilerParams(dimension_semantics=("arbitrary", "arbitrary"),
                                             vmem_limit_bytes=VMEM_LIMIT),
        name="ffn",
    )(*args)
    return (out[0], tuple(out[1:])) if emit else out[0]


def _inproj_kernel(h_ref, sh_ref, sc_ref, w_ref, o_ref, xn_ref, *, mc, tm):
    j = pl.program_id(1)

    @pl.when(j == 0)
    def _():
        scale = 1.0 + _mod_val(sc_ref, mc, tm)
        shift = _mod_val(sh_ref, mc, tm)
        per_row = scale.shape[0] > 1
        for r in range(tm // NORM_ROWS):
            rows = slice(r * NORM_ROWS, (r + 1) * NORM_ROWS)
            xn = (_rms(h_ref[rows, :], 1e-6) * (scale[rows] if per_row else scale)
                  + (shift[rows] if per_row else shift)).astype(BF16)
            xn_ref[rows, :] = xn
            o_ref[rows, :] = jnp.dot(xn, w_ref[...], preferred_element_type=F32)

    @pl.when(j > 0)
    def _():
        o_ref[...] = jnp.dot(xn_ref[...], w_ref[...], preferred_element_type=F32)


def _inproj(h, mod, mc, w, li):
    m = h.shape[0]
    n = w.shape[2]
    tm = min(ROW_TILE, m)
    tn = n // 3
    if n % (6 * LANES) == 0 and m % (2 * ROW_TILE) == 0:
        tm, tn = 2 * ROW_TILE, n // 6
    in_specs = [pl.BlockSpec((tm, D_MODEL), lambda i, j: (i, 0))]
    in_specs += _mod_specs(mc, tm, (3, 4))
    in_specs += [pl.BlockSpec((None, D_MODEL, tn), lambda i, j: (li, 0, j))]
    return pl.pallas_call(
        functools.partial(_inproj_kernel, mc=mc, tm=tm),
        grid=(m // tm, n // tn),
        in_specs=in_specs,
        out_specs=pl.BlockSpec((tm, tn), lambda i, j: (i, j)),
        out_shape=jax.ShapeDtypeStruct((m, n), F32),
        scratch_shapes=[pltpu.VMEM((tm, D_MODEL), BF16)],
        compiler_params=pltpu.CompilerParams(dimension_semantics=("arbitrary", "arbitrary"),
                                             vmem_limit_bytes=VMEM_LIMIT),
        name="inproj",
    )(h, mod, mod, w)


def _outproj_kernel(y_ref, w_ref, h_ref, g_ref, o_ref, *, mc, tm):
    mix = jnp.dot(y_ref[...], w_ref[...], preferred_element_type=F32)
    o_ref[...] = h_ref[...] + (1.0 + _mod_val(g_ref, mc, tm)) * mix


def _outproj(y, w, li, h, mod, mc):
    m = h.shape[0]
    tm = min(ROW_TILE, m)
    in_specs = [pl.BlockSpec((tm, D_MODEL), lambda i, j: (i, 0)),
                pl.BlockSpec((None, D_MODEL, D_MODEL), lambda i, j: (li, 0, 0)),
                pl.BlockSpec((tm, D_MODEL), lambda i, j: (i, 0))]
    in_specs += _mod_specs(mc, tm, (5,))
    return pl.pallas_call(
        functools.partial(_outproj_kernel, mc=mc, tm=tm),
        grid=(m // tm, 1),
        in_specs=in_specs,
        out_specs=pl.BlockSpec((tm, D_MODEL), lambda i, j: (i, 0)),
        out_shape=jax.ShapeDtypeStruct((m, D_MODEL), F32),
        compiler_params=pltpu.CompilerParams(dimension_semantics=("arbitrary", "arbitrary"),
                                             vmem_limit_bytes=VMEM_LIMIT),
        name="outproj",
    )(y, w, h, mod)


def _state_spec(li, nl, g, tail, whole):
    zeros = (0,) * len(tail)
    if whole:
        return pl.BlockSpec((nl, g) + tail, lambda b, c: (0, b) + zeros)
    return pl.BlockSpec((None, g) + tail, lambda b, c: (li, b) + zeros)


def _state_out(stacked, li, nseq, g, tail, whole):
    if stacked is None:
        zeros = (0,) * len(tail)
        return (jax.ShapeDtypeStruct((nseq,) + tail, F32),
                pl.BlockSpec((g,) + tail, lambda b, c: (b,) + zeros))
    return jax.ShapeDtypeStruct(stacked.shape, F32), _state_spec(li, stacked.shape[0], g, tail, whole)


def _layer_views(li, nl, pairs):
    for src, dst in pairs:
        for lj in range(nl):
            if lj != li:
                dst[lj] = src[lj]
    return [r.at[li] for pair in pairs for r in pair]


def _seq_masks(g, t):
    n = g * t
    r = lax.broadcasted_iota(jnp.int32, (n, n), 0)
    c = lax.broadcasted_iota(jnp.int32, (n, n), 1)
    if g == 1:
        return r > c, r >= c
    same = (r // t) == (c // t)
    return same & (r > c), same & (r >= c)


def _seq_selectors(g, t):
    r = np.arange(g * t)[:, None]
    c = np.arange(g * t)[None, :]
    same = (r // t) == (c // t)
    incl = same & (r >= c)
    last = c == (r // t) * t + (t - 1)
    return jnp.asarray(np.stack([incl, last, incl.T]), BF16)


EV_W0, EV_A0, EV_KK, EV_KA, EV_RK, EV_GNW, EV_GNB, EV_D, EV_NW = range(9)
EV_VEC_ROWS = 16


def _even_kernel(*refs, t, g, nchunks, lv, has_init, ncarry, whole):
    if has_init:
        z_ref, shift0_ref, wkv0_ref, conv0_ref, ssm0_ref, *rest = refs
    else:
        z_ref, *rest = refs
    mu_ref, vec_ref, wa_ref, gup_ref, cw_ref, cb_ref, dtb_ref, alog_ref, exp_ref, sel_ref, *rest = rest
    (y_ref, shift_o, wkv_o, conv_o, ssm_o,
     carry_ref, wkv_ref, ext_ref, ssm_ref,
     r_s, kp_s, v_s, kk_s, a_s, c_s, ld_s, g_s, cl_s, y_s) = rest[ncarry:]
    n = g * t
    ci = pl.program_id(1)
    masked = lv < t
    nlog = int(math.log2(t))
    seqs = range(g)
    hist = SUBLANES - (CONV_W - 1)
    direct = has_init and nchunks == 1
    if whole:
        assert direct
        wkv0_ref, wkv_o, ssm0_ref, ssm_o = _layer_views(*whole, [(wkv0_ref, wkv_o), (ssm0_ref, ssm_o)])
    wkv_src, wkv_dst = (wkv0_ref, wkv_o) if direct else (wkv_ref, wkv_ref)
    ssm_src, ssm_dst = (ssm0_ref, ssm_o) if direct else (ssm_ref, ssm_ref)

    @pl.when(ci == 0)
    def _():
        if has_init:
            carry_ref[:, :, 0:RW_COLS] = shift0_ref[...]
            carry_ref[:, :, RW_COLS:RW_PAD] = jnp.zeros((g, 1, RW_PAD - RW_COLS), F32)
            ext_ref[:, 0:hist, :] = jnp.zeros((g, hist, MB_CONV_CH), F32)
            ext_ref[:, hist:SUBLANES, :] = conv0_ref[...]
            if not direct:
                wkv_ref[...] = wkv0_ref[...]
                ssm_ref[...] = ssm0_ref[...]
        else:
            carry_ref[...] = jnp.zeros_like(carry_ref)
            wkv_ref[...] = jnp.zeros_like(wkv_ref)
            ext_ref[:, 0:SUBLANES, :] = jnp.zeros((g, SUBLANES, MB_CONV_CH), F32)
            ssm_ref[...] = jnp.zeros_like(ssm_ref)

    tok = lax.broadcasted_iota(jnp.int32, (n, 1), 0) % t
    valid = tok < lv
    strict, incl = _seq_masks(g, t)
    sel_incl, sel_last, sel_upper = sel_ref[0], sel_ref[1], sel_ref[2]

    def vec(i, lo=0, hi=D_MIX):
        return vec_ref[i:i + 1, lo:hi]

    def rows_of(x, s):
        return x[s * t:(s + 1) * t]

    def per_seq(fn):
        parts = [fn(s) for s in seqs]
        return parts[0] if g == 1 else jnp.concatenate(parts, axis=0)

    def shifted(a, b):
        cur = z_ref[:, :, a:b].reshape(n, b - a)
        carry = per_seq(lambda s: jnp.broadcast_to(carry_ref[s, :, a:b], (t, b - a)))
        prev = jnp.where(tok == 0, carry, pltpu.roll(cur, 1, axis=0))
        return cur + mu_ref[:, a:b] * (prev - cur)

    r_s[...] = shifted(0, D_MIX)
    k = shifted(D_MIX, 2 * D_MIX)
    v_s[...] = shifted(2 * D_MIX, 3 * D_MIX)
    lo = shifted(3 * D_MIX, 3 * D_MIX + LANES)
    lane = lax.broadcasted_iota(jnp.int32, (n, LANES), 1)
    wa = _mm(jnp.where(lane < RW_LORA_W, jnp.tanh(lo), lo), wa_ref[...])
    w_log = -_softplus(-(vec(EV_W0) + wa[:, 0:D_MIX])) - 0.5
    ld = -jnp.exp(w_log)
    a = _sigmoid(vec(EV_A0) + wa[:, D_MIX:2 * D_MIX])
    g_s[...] = _mm(_sigmoid(shifted(3 * D_MIX + LANES, RW_PAD)), gup_ref[...])
    kp = k * (1.0 + (a - 1.0) * vec(EV_KA))
    if masked:
        ld = jnp.where(valid, ld, 0.0)
        kp = jnp.where(valid, kp, 0.0)
        a = jnp.where(valid, a, 0.0)
    kk_s[...] = k * vec(EV_KK)
    kp_s[...] = kp
    a_s[...] = a
    ld_s[...] = ld
    c = _mm_sel(sel_incl, ld)
    c_s[...] = c
    cl_s[...] = _mm_sel(sel_last, c)
    carry_ref[...] = z_ref[:, lv - 1:lv, 0:RW_PAD]

    heads = range(H_A)
    sls = [slice(h * HEAD_A, (h + 1) * HEAD_A) for h in heads]
    rs = [r_s[:, sl] for sl in sls]
    kps = [kp_s[:, sl] for sl in sls]
    vs = [v_s[:, sl] for sl in sls]
    cs = [c_s[:, sl] for sl in sls]
    cls = [cl_s[:, sl] for sl in sls]
    kks, bs = [], []
    for sl in sls:
        kkr = kk_s[:, sl]
        nrm = jnp.sqrt(jnp.sum(kkr * kkr, axis=-1, keepdims=True))
        kks.append(kkr / jnp.maximum(nrm, 1e-12))
        bs.append(kks[-1] * a_s[:, sl])
    ecis = [jnp.exp(-c) for c in cs]
    kts = [kk * jnp.exp(c - ld_s[:, sl]) for kk, c, sl in zip(kks, cs, sls)]
    rts = [r * jnp.exp(c) for r, c in zip(rs, cs)]
    lhss = [jnp.concatenate([kt, rt], axis=0) for kt, rt in zip(kts, rts)]
    a1s = [_mm_nt(lhs, kp * eci) for lhs, kp, eci in zip(lhss, kps, ecis)]
    a2s = [_mm_nt(lhs, b * eci) for lhs, b, eci in zip(lhss, bs, ecis)]
    pms = [[_mm_nt(jnp.concatenate([rows_of(kt, s), rows_of(rt, s)], axis=0), wkv_src[s, h]) for s in seqs]
           for h, kt, rt in zip(heads, kts, rts)]
    pmk = [per_seq(lambda s: pm[s][0:t]) for pm in pms]
    pmr = [per_seq(lambda s: pm[s][t:2 * t]) for pm in pms]
    us = [pk + _mm(jnp.where(strict, a1[0:n], 0.0), v) for pk, a1, v in zip(pmk, a1s, vs)]
    ps = [jnp.where(strict, -a2[0:n], 0.0) for a2 in a2s]
    us = [u + _mm(p, u) for u, p in zip(us, ps)]
    for _ in range(nlog - 1):
        ps = [_mm(p, p) for p in ps]
        us = [u + _mm(p, u) for u, p in zip(us, ps)]
    ys = [pr + _mm(jnp.where(incl, a1[n:2 * n], 0.0), v) - _mm(jnp.where(incl, a2[n:2 * n], 0.0), u)
          for pr, a1, a2, v, u in zip(pmr, a1s, a2s, vs, us)]
    for h in heads:
        dec_end = jnp.exp(cls[h] - cs[h])
        kbar = kps[h] * dec_end
        bbar = bs[h] * dec_end
        for s in seqs:
            vu = jnp.concatenate([rows_of(vs[h], s), rows_of(us[h], s)], axis=0)
            kb = jnp.concatenate([rows_of(kbar, s), -rows_of(bbar, s)], axis=0)
            wkv_dst[s, h] = wkv_src[s, h] * jnp.exp(cls[h][s * t:s * t + 1]) + _mm_tn(vu, kb)
    for h, sl in zip(heads, sls):
        y, r, kp, v = ys[h], rs[h], kps[h], vs[h]
        mean = jnp.mean(y, axis=-1, keepdims=True)
        yc = y - mean
        var = jnp.mean(yc * yc, axis=-1, keepdims=True)
        yn = (yc * lax.rsqrt(var + RW_GN_EPS) * vec(EV_GNW, sl.start, sl.stop)
              + vec(EV_GNB, sl.start, sl.stop))
        bonus = jnp.sum(r * kp * vec(EV_RK, sl.start, sl.stop), axis=-1, keepdims=True) * v
        y_s[:, sl] = (yn + bonus) * g_s[:, sl]

    zg = z_ref[:, :, ZM:ZM + D_MIX].reshape(n, D_MIX)
    ext_ref[:, SUBLANES:SUBLANES + t, :] = z_ref[:, :, ZM + D_MIX:ZM + D_MIX + MB_CONV_CH]
    conv = cb_ref[...]
    for w in range(CONV_W):
        conv = conv + cw_ref[w:w + 1, :] * ext_ref[:, hist + w:hist + w + t, :].reshape(n, MB_CONV_CH)
    new_conv = ext_ref[:, SUBLANES + lv - (CONV_W - 1):SUBLANES + lv, :]
    ext_ref[:, hist:SUBLANES, :] = new_conv
    xbc = _silu(conv)
    xs = xbc[:, 0:D_MIX]
    dt = _softplus(z_ref[:, :, ZM + D_MIX + MB_CONV_CH:ZM + MB_PAD].reshape(n, LANES) + dtb_ref[...])
    logd = dt * (-jnp.exp(alog_ref[...]))
    if masked:
        logd = jnp.where(valid, logd, 0.0)
    c = _mm_sel(sel_incl, logd)
    c_t = _mm_tn_sel(logd, sel_upper)
    c_end = _mm_sel(sel_last, c)
    wide = _mm_xsel(jnp.concatenate([dt, c, c_end - c], axis=0), exp_ref[...])
    xdt = xs * wide[0:n]
    if masked:
        xdt = jnp.where(valid, xdt, 0.0)
    ec_w = jnp.exp(wide[n:2 * n])
    xdec = xdt * jnp.exp(wide[2 * n:3 * n])
    hpg = H_B // N_GROUPS
    bgs = [xbc[:, D_MIX + q * D_STATE:D_MIX + (q + 1) * D_STATE] for q in range(N_GROUPS)]
    cgs = [xbc[:, D_MIX + (N_GROUPS + q) * D_STATE:D_MIX + (N_GROUPS + q + 1) * D_STATE]
           for q in range(N_GROUPS)]
    scores = [_mm_nt(cg, bg) for cg, bg in zip(cgs, bgs)]
    segs = [jnp.where(incl, jnp.exp(jnp.where(incl, c[:, h:h + 1] - c_t[h:h + 1, :], 0.0)), 0.0)
            for h in range(H_B)]
    hsl = [slice(h * HEAD_B, (h + 1) * HEAD_B) for h in range(H_B)]
    intra = [_mm(scores[h // hpg] * segs[h], xdt[:, hsl[h]]) for h in range(H_B)]
    grp = lambda s, q: ssm_src[s, q * hpg:(q + 1) * hpg].reshape(hpg * HEAD_B, D_STATE)
    inter = [per_seq(lambda s: _mm_nt(rows_of(cgs[q], s), grp(s, q))) for q in range(N_GROUPS)]
    for q in range(N_GROUPS):
        for s in seqs:
            upd = _mm_tn(rows_of(xdec[:, q * hpg * HEAD_B:(q + 1) * hpg * HEAD_B], s), rows_of(bgs[q], s))
            last = s * t + t - 1
            for h in range(q * hpg, (q + 1) * hpg):
                ssm_dst[s, h] = (ssm_src[s, h] * jnp.exp(c[last:last + 1, h:h + 1])
                                 + upd[(h - q * hpg) * HEAD_B:(h - q * hpg + 1) * HEAD_B])
    o = jnp.concatenate(intra, axis=1) + jnp.concatenate(inter, axis=1) * ec_w
    yb = (o + vec(EV_D) * xs) * _silu(zg)
    y_s[:, D_MIX:2 * D_MIX] = _rms(yb, 1e-5) * vec(EV_NW)
    y_ref[...] = y_s[...].astype(BF16).reshape(g, t, D_MODEL)

    @pl.when(ci == nchunks - 1)
    def _():
        shift_o[...] = carry_ref[...]
        conv_o[...] = new_conv
        if not direct:
            wkv_o[...] = wkv_ref[...]
            ssm_o[...] = ssm_ref[...]


def _carry_specs(carriers, first_out):
    return ([pl.BlockSpec(memory_space=pl.ANY)] * len(carriers), list(carriers),
            lambda nargs: {nargs + i: first_out[i] for i in range(len(carriers))})


def _even_mixer(z, states, carriers, li, p, *, t, g, lv):
    nseq, length, _ = z.shape
    nchunks = length // t
    has_init = states is not None
    assert (lv == t or nchunks == 1) and nseq % g == 0
    full = lambda *shape: pl.BlockSpec(shape, lambda b, c: (0,) * len(shape))
    in_specs = [pl.BlockSpec((g, t, EV_PAD), lambda b, c: (b, c, 0))]
    args = [z]
    ssm_t = None
    whole = has_init and not carriers
    nl = states[1].shape[0] if has_init else 0
    if has_init:
        shift0, wkv0, conv0, ssm0 = states
        ssm_t = jnp.swapaxes(ssm0, 3, 4)
        in_specs += [pl.BlockSpec((None, g, 1, RW_COLS), lambda b, c: (li, b, 0, 0)),
                     _state_spec(li, nl, g, (H_A, HEAD_A, HEAD_A), whole),
                     pl.BlockSpec((None, g, CONV_W - 1, MB_CONV_CH), lambda b, c: (li, b, 0, 0)),
                     _state_spec(li, nl, g, (H_B, HEAD_B, D_STATE), whole)]
        args += [shift0.reshape(-1, nseq, 1, RW_COLS), wkv0, conv0, ssm_t]
    in_specs += [full(1, RW_PAD), full(EV_VEC_ROWS, D_MIX), full(LANES, 2 * D_MIX),
                 full(RW_PAD - 3 * D_MIX - LANES, D_MIX), full(CONV_W, MB_CONV_CH),
                 full(1, MB_CONV_CH), full(1, LANES), full(1, LANES), full(LANES, D_MIX),
                 full(3, g * t, g * t)]
    args += [p['mu'], p['vec'], p['wa'], p['gup'], p['conv_w'], p['conv_b'], p['dt_bias'],
             p['a_log'], p['expand'], _seq_selectors(g, t)]
    carry_specs, carry_args, carry_alias = _carry_specs(carriers, (2, 4))
    aliases = carry_alias(len(args))
    in_specs += carry_specs
    args += carry_args
    wkv_shape, wkv_spec = _state_out(states and states[1], li, nseq, g, (H_A, HEAD_A, HEAD_A), whole)
    ssm_shape, ssm_spec = _state_out(ssm_t, li, nseq, g, (H_B, HEAD_B, D_STATE), whole)
    out_shape = (jax.ShapeDtypeStruct((nseq, length, D_MODEL), BF16),
                 jax.ShapeDtypeStruct((nseq, 1, RW_PAD), F32), wkv_shape,
                 jax.ShapeDtypeStruct((nseq, CONV_W - 1, MB_CONV_CH), F32), ssm_shape)
    out_specs = (pl.BlockSpec((g, t, D_MODEL), lambda b, c: (b, c, 0)),
                 pl.BlockSpec((g, 1, RW_PAD), lambda b, c: (b, 0, 0)), wkv_spec,
                 pl.BlockSpec((g, CONV_W - 1, MB_CONV_CH), lambda b, c: (b, 0, 0)), ssm_spec)
    n = g * t
    wide = lambda: pltpu.VMEM((n, D_MIX), F32)
    gs = 1 if (has_init and nchunks == 1) else g
    scratch = [pltpu.VMEM((g, 1, RW_PAD), F32), pltpu.VMEM((gs, H_A, HEAD_A, HEAD_A), F32),
               pltpu.VMEM((g, t + SUBLANES, MB_CONV_CH), F32), pltpu.VMEM((gs, H_B, HEAD_B, D_STATE), F32)]
    scratch += [wide() for _ in range(9)] + [pltpu.VMEM((n, D_MODEL), F32)]
    y, shift, wkv, conv, ssm = pl.pallas_call(
        functools.partial(_even_kernel, t=t, g=g, nchunks=nchunks, lv=lv, has_init=has_init,
                          ncarry=len(carriers), whole=(li, nl) if whole else None),
        grid=(nseq // g, nchunks),
        in_specs=in_specs, out_specs=out_specs, out_shape=out_shape, scratch_shapes=scratch,
        input_output_aliases=aliases,
        compiler_params=pltpu.CompilerParams(dimension_semantics=("arbitrary", "arbitrary"),
                                             vmem_limit_bytes=VMEM_LIMIT),
        name="even_mixer",
    )(*args)
    return y, (shift[:, 0, :RW_COLS], wkv, conv, ssm)


def _odd_kernel(*refs, t, tsub, g, nchunks, lv, has_init, ncarry, whole):
    if has_init:
        z_ref, ret0_ref, gla0_ref, *rest = refs
    else:
        z_ref, *rest = refs
    cos_ref, sin_ref, seg_ref, pw_ref, dte_ref, rnw_ref, gup_ref, gb_ref, gnw_ref, sel_ref, *rest = rest
    y_ref, ret_o, gla_o, ret_ref, gla_ref, la_s, y_s = rest[ncarry:]
    n = g * t
    ci = pl.program_id(1)
    masked = lv < t
    seqs = range(g)
    direct = has_init and nchunks == 1
    if whole:
        assert direct
        ret0_ref, ret_o, gla0_ref, gla_o = _layer_views(*whole, [(ret0_ref, ret_o), (gla0_ref, gla_o)])
    ret_src, ret_dst = (ret0_ref, ret_o) if direct else (ret_ref, ret_ref)

    @pl.when(ci == 0)
    def _():
        if has_init:
            if not direct:
                ret_ref[...] = ret0_ref[...]
            for s in seqs:
                for h in range(H_D):
                    gla_ref[s, h] = jnp.transpose(gla0_ref[s, h])
        else:
            ret_ref[...] = jnp.zeros_like(ret_ref)
            gla_ref[...] = jnp.zeros_like(gla_ref)

    valid = lax.broadcasted_iota(jnp.int32, (n, 1), 0) % t < lv

    def zcols(a, b):
        return z_ref[:, :, a:b].reshape(n, b - a)

    def per_seq(fn, rows):
        parts = [fn(s, slice(s * rows, (s + 1) * rows)) for s in seqs]
        return parts[0] if g == 1 else jnp.concatenate(parts, axis=0)

    cosf = cos_ref[...]
    sinf = sin_ref[...]

    def rotary(x):
        return x * cosf + pltpu.roll(x, HEAD_C // 2, axis=1) * sinf

    hc = range(H_C)
    qs = [rotary(zcols(h * HEAD_C, (h + 1) * HEAD_C)) * HEAD_C ** -0.5 for h in hc]
    ks = [rotary(zcols(D_MIX + h * HEAD_C, D_MIX + (h + 1) * HEAD_C)) for h in hc]
    if masked:
        ks = [jnp.where(valid, k, 0.0) for k in ks]
    vs = [zcols(2 * D_MIX + h * HEAD_C, 2 * D_MIX + (h + 1) * HEAD_C) for h in hc]
    sc = [_mm_nt(q, k) * seg_ref[h] for h, q, k in zip(hc, qs, ks)]
    inter = [per_seq(lambda s, rw: _mm(qs[h][rw], ret_src[s, h]), t) * pw_ref[h] for h in hc]
    outs = [_mm(s_, v) + i for s_, v, i in zip(sc, vs, inter)]
    for h in hc:
        gamma = 1.0 - 2.0 ** (-5.0 - h)
        kd = ks[h] * dte_ref[h]
        for s in seqs:
            rw = slice(s * t, (s + 1) * t)
            ret_dst[s, h] = ret_src[s, h] * gamma ** lv + _mm_tn(kd[rw], vs[h][rw])
    for h in hc:
        sl = slice(h * HEAD_C, (h + 1) * HEAD_C)
        rg = zcols(3 * D_MIX + h * HEAD_C, 3 * D_MIX + (h + 1) * HEAD_C)
        y_s[:, sl] = _rms(outs[h], 1e-6) * rnw_ref[:, sl] * _silu(rg)

    gq0 = 4 * D_MIX
    gk0 = gq0 + H_D * HEAD_DK
    gv0 = gk0 + H_D * HEAD_DK
    gr0 = gv0 + D_MIX
    x = _mm(zcols(OD_GD, OD_GD + LANES), gup_ref[...]) + gb_ref[...]
    la = -_softplus(-x) * (1.0 / GLA_NORMALIZER)
    if masked:
        la = jnp.where(valid, la, 0.0)
    la_s[...] = la
    assert g == 1 or tsub == t
    incl = _seq_masks(g, tsub)[1]
    hd = range(H_D)
    for u in range(t // tsub):
        rs = slice(0, n) if g > 1 else slice(u * tsub, (u + 1) * tsub)

        def cols(a, b, rs=rs):
            return z_ref[0, rs, a:b] if g == 1 else zcols(a, b)

        cum_all = _mm_sel(sel_ref[0], la_s[rs, :])
        cums = [cum_all[:, h * HEAD_DK:(h + 1) * HEAD_DK] for h in hd]
        qes = [cols(gq0 + h * HEAD_DK, gq0 + (h + 1) * HEAD_DK) * HEAD_DK ** -0.5 * jnp.exp(cums[h])
               for h in hd]
        ks = [cols(gk0 + h * HEAD_DK, gk0 + (h + 1) * HEAD_DK) for h in hd]
        if masked:
            ks = [jnp.where(valid[rs], k, 0.0) for k in ks]
        vs = [cols(gv0 + h * HEAD_DV, gv0 + (h + 1) * HEAD_DV) for h in hd]
        scores = [jnp.where(incl, _mm_nt(qes[h], ks[h] * jnp.exp(-cums[h])), 0.0) for h in hd]
        inter = [per_seq(lambda s, rw: _mm_nt(qes[h][rw], gla_ref[s, h]), tsub) for h in hd]
        outs = [_mm(scores[h], vs[h]) + inter[h] for h in hd]
        for h in hd:
            for s in seqs:
                rw = slice(s * tsub, (s + 1) * tsub)
                cum = cums[h][rw]
                last = cum[tsub - 1:tsub]
                gla_ref[s, h] = (gla_ref[s, h] * jnp.exp(last)
                                 + _mm_tn(vs[h][rw], ks[h][rw] * jnp.exp(last - cum)))
        for h in hd:
            gr = cols(gr0 + h * HEAD_DV, gr0 + (h + 1) * HEAD_DV)
            ys = slice(D_MIX + h * HEAD_DV, D_MIX + (h + 1) * HEAD_DV)
            y_s[rs, ys] = _rms(outs[h], 1e-6) * gnw_ref[:, h * HEAD_DV:(h + 1) * HEAD_DV] * _silu(gr)
    y_ref[...] = y_s[...].reshape(g, t, D_MODEL).astype(BF16)

    @pl.when(ci == nchunks - 1)
    def _():
        if not direct:
            ret_o[...] = ret_ref[...]
        for s in seqs:
            for h in range(H_D):
                gla_o[s, h] = jnp.transpose(gla_ref[s, h])


def _ret_tables(t, g, lv, pos0, nchunks):
    half = HEAD_C // 2
    inv = ROPE_BASE ** (-jnp.arange(half, dtype=F32) / half)
    pos = pos0 + jnp.arange(t * nchunks, dtype=F32).reshape(nchunks, 1, t)
    pos = jnp.broadcast_to(pos, (nchunks, g, t)).reshape(-1)
    ang = pos[:, None] * inv[None, :]
    cos, sin = jnp.cos(ang), jnp.sin(ang)
    cosf = jnp.concatenate([cos, cos], axis=-1)
    sinf = jnp.concatenate([-sin, sin], axis=-1)
    gam = 1.0 - np.exp2(-5.0 - np.arange(H_C, dtype=np.float64))
    ti = np.arange(t)
    d = ti[:, None] - ti[None, :]
    seg = np.where(d >= 0, gam[:, None, None] ** np.maximum(d, 0)[None], 0.0)
    seg = np.stack([np.kron(np.eye(g), s) for s in seg])
    pw = np.tile(gam[:, None] ** (ti + 1)[None], (1, g))
    dte = np.tile(np.where(ti < lv, gam[:, None] ** np.maximum(lv - 1 - ti, 0)[None], 0.0), (1, g))
    lanes = lambda a: jnp.asarray(np.broadcast_to(a[:, :, None], a.shape + (LANES,)), F32)
    return cosf, sinf, jnp.asarray(seg, F32), lanes(pw), lanes(dte)


def _odd_mixer(z, states, carriers, li, p, *, t, g, lv, pos0):
    nseq, length, _ = z.shape
    nchunks = length // t
    has_init = states is not None
    assert (lv == t or nchunks == 1) and nseq % g == 0
    tsub = min(GLA_SUB, t)
    n = g * t
    nsub = n if g > 1 else tsub
    cosf, sinf, seg, pw, dte = _ret_tables(t, g, lv, pos0, nchunks)
    full = lambda *shape: pl.BlockSpec(shape, lambda b, c: (0,) * len(shape))
    in_specs = [pl.BlockSpec((g, t, OD_PAD), lambda b, c: (b, c, 0))]
    args = [z]
    whole = has_init and not carriers
    nl = states[0].shape[0] if has_init else 0
    if has_init:
        in_specs += [_state_spec(li, nl, g, (H_C, HEAD_C, HEAD_C), whole),
                     _state_spec(li, nl, g, (H_D, HEAD_DK, HEAD_DV), whole)]
        args += list(states)
    in_specs += [pl.BlockSpec((n, HEAD_C), lambda b, c: (c, 0)),
                 pl.BlockSpec((n, HEAD_C), lambda b, c: (c, 0)),
                 full(H_C, n, n), full(H_C, n, LANES), full(H_C, n, LANES),
                 full(1, D_MIX), full(LANES, H_D * HEAD_DK), full(1, H_D * HEAD_DK), full(1, D_MIX),
                 full(3, nsub, nsub)]
    args += [cosf, sinf, seg, pw, dte, p['ret_norm_w'], p['gate_up'], p['gate_b'], p['gla_norm_w'],
             _seq_selectors(g if g > 1 else 1, tsub)]
    carry_specs, carry_args, carry_alias = _carry_specs(carriers, (1, 2))
    aliases = carry_alias(len(args))
    in_specs += carry_specs
    args += carry_args
    ret_shape, ret_spec = _state_out(states and states[0], li, nseq, g, (H_C, HEAD_C, HEAD_C), whole)
    gla_shape, gla_spec = _state_out(states and states[1], li, nseq, g, (H_D, HEAD_DK, HEAD_DV), whole)
    out_shape = (jax.ShapeDtypeStruct((nseq, length, D_MODEL), BF16), ret_shape, gla_shape)
    out_specs = (pl.BlockSpec((g, t, D_MODEL), lambda b, c: (b, c, 0)), ret_spec, gla_spec)
    gs = 1 if (has_init and nchunks == 1) else g
    scratch = [pltpu.VMEM((gs, H_C, HEAD_C, HEAD_C), F32), pltpu.VMEM((g, H_D, HEAD_DV, HEAD_DK), F32),
               pltpu.VMEM((n, H_D * HEAD_DK), F32), pltpu.VMEM((n, D_MODEL), F32)]
    y, ret, gla = pl.pallas_call(
        functools.partial(_odd_kernel, t=t, tsub=tsub, g=g, nchunks=nchunks, lv=lv, has_init=has_init,
                          ncarry=len(carriers), whole=(li, nl) if whole else None),
        grid=(nseq // g, nchunks),
        in_specs=in_specs, out_specs=out_specs, out_shape=out_shape, scratch_shapes=scratch,
        input_output_aliases=aliases,
        compiler_params=pltpu.CompilerParams(dimension_semantics=("arbitrary", "arbitrary"),
                                             vmem_limit_bytes=VMEM_LIMIT),
        name="odd_mixer",
    )(*args)
    return y, (ret, gla)


def _prep_proj(W):
    w = W['ev_w_in']
    n = w.shape[0]
    ev_in = jnp.concatenate([w[:, :, :RW_COLS], jnp.zeros((n, D_MODEL, RW_PAD - RW_COLS), F32),
                             w[:, :, RW_COLS:], jnp.zeros((n, D_MODEL, MB_PAD - MB_COLS), F32)], axis=2)
    w = W['od_w_in']
    n = w.shape[0]
    gd0 = OD_COLS - D_MIX - GLA_LORA
    od_in = jnp.concatenate([w[:, :, :gd0], w[:, :, gd0 + GLA_LORA:], w[:, :, gd0:gd0 + GLA_LORA],
                             jnp.zeros((n, D_MODEL, OD_PAD - OD_COLS), F32)], axis=2)
    return dict(ev_in=ev_in.astype(BF16), od_in=od_in.astype(BF16),
                ev_out=W['ev_w_out'].astype(BF16), od_out=W['od_w_out'].astype(BF16))


def _prep_even(i, W):
    mu = jnp.pad(W['rw_mu'][i], (0, RW_PAD - RW_COLS)).reshape(1, RW_PAD)
    rep = lambda v: jnp.repeat(v, HEAD_B)
    rows = [W['rw_w0'][i], W['rw_a0'][i], W['rw_k_k'][i], W['rw_k_a'][i], W['rw_r_k'][i].reshape(-1),
            W['rw_gn_w'][i], W['rw_gn_b'][i], rep(W['mb_d'][i]), W['mb_norm_w'][i]]
    vec = jnp.concatenate([jnp.stack(rows), jnp.zeros((EV_VEC_ROWS - len(rows), D_MIX), F32)], axis=0)
    wa = jnp.zeros((LANES, 2 * D_MIX), F32)
    wa = wa.at[:RW_LORA_W, :D_MIX].set(W['rw_w_up'][i]).at[RW_LORA_W:, D_MIX:].set(W['rw_a_up'][i])
    gup = jnp.pad(W['rw_g_up'][i], ((0, RW_PAD - 3 * D_MIX - LANES - RW_LORA_G), (0, 0)))
    pad_h = lambda v: jnp.pad(v, (0, LANES - H_B)).reshape(1, LANES)
    expand = np.zeros((LANES, D_MIX), np.float32)
    for h in range(H_B):
        expand[h, h * HEAD_B:(h + 1) * HEAD_B] = 1.0
    return dict(mu=mu, vec=vec, wa=wa.astype(BF16), gup=gup.astype(BF16), conv_w=W['mb_conv_w'][i],
                conv_b=W['mb_conv_b'][i].reshape(1, MB_CONV_CH), dt_bias=pad_h(W['mb_dt_bias'][i]),
                a_log=pad_h(W['mb_a_log'][i]), expand=jnp.asarray(expand, BF16))


def _prep_odd(i, W):
    gate_up = jnp.pad(W['gla_gate_up'][i], ((0, LANES - GLA_LORA), (0, 0)))
    return dict(ret_norm_w=W['ret_norm_w'][i].reshape(1, D_MIX), gate_up=gate_up.astype(BF16),
                gate_b=W['gla_gate_b'][i].reshape(1, H_D * HEAD_DK),
                gla_norm_w=W['gla_norm_w'][i].reshape(1, D_MIX))


def _trunk(h, mod, mc0, states, P, PW, FW, final_g, *, nseq, length, t, g, lv, pos0):
    outs = ([], [], [], [], [], [])
    padded = -(-length // t) * t
    carried = states is not None
    carriers = {True: (), False: ()}
    stacked_f32 = isinstance(FW, tuple)
    bf16_weights = []

    def ffn(h, mc, mod_base, k, final_g=None):
        if not stacked_f32:
            return _ffn(h, mod, mc, mod_base, FW[2 * mc['l'] + k], k, final_g)
        h, w16 = _ffn(h, mod, mc, mod_base, FW, k, final_g)
        bf16_weights.append(w16)
        return h

    for l in range(DEPTH):
        i = l // 2
        p = P[l]
        mc = dict(mc0, l=l)
        even = l % 2 == 0
        h = ffn(h, mc, 0, 0)
        z = _inproj(h, mod, mc, PW['ev_in' if even else 'od_in'], i).reshape(nseq, length, -1)
        if padded != length:
            z = jnp.pad(z, ((0, 0), (0, padded - length), (0, 0)))
        gl = max(1, g // 2) if carried and not carriers[even] else g
        if even:
            y, new = _even_mixer(z, states and states[:4], carriers[even], i, p, t=t, g=gl, lv=lv)
            slots, large = (0, 1, 2, 3), (1, 3)
        else:
            y, new = _odd_mixer(z, states and states[4:], carriers[even], i, p, t=t, g=gl, lv=lv, pos0=pos0)
            slots, large = (4, 5), (4, 5)
        if carried:
            carriers[even] = tuple(n for k, n in zip(slots, new) if k in large)
        for k, n in zip(slots, new):
            if not (carried and k in large):
                outs[k].append(n)
        y = y[:, :length].reshape(nseq * length, D_MODEL)
        h = _outproj(y, PW['ev_out' if even else 'od_out'], i, h, mod, mc)
        h = ffn(h, mc, 6, 1, final_g=final_g if l == DEPTH - 1 else None)
    res = [jnp.stack(lst) if lst else None for lst in outs]
    if carried:
        res[1], res[3] = carriers[True]
        res[4], res[5] = carriers[False]
    res[3] = jnp.swapaxes(res[3], -1, -2)
    return h, tuple(res), bf16_weights


def kernel(x_prompt, x_sample, state_rwkv_shift, state_rwkv_wkv, state_mamba_conv, state_mamba_ssm,
           state_ret, state_gla, c_prompt, c_sample, ada_w, ada_b, ffn_wg, ffn_wu, ffn_wd, ev_w_in,
           ev_w_out, rw_mu, rw_w0, rw_w_up, rw_a0, rw_a_up, rw_g_up, rw_k_k, rw_k_a, rw_r_k, rw_gn_w,
           rw_gn_b, mb_conv_w, mb_conv_b, mb_dt_bias, mb_a_log, mb_d, mb_norm_w, od_w_in, od_w_out,
           ret_norm_w, gla_gate_up, gla_gate_b, gla_norm_w, final_g):
    W = dict(ev_w_in=ev_w_in, ev_w_out=ev_w_out, rw_mu=rw_mu, rw_w0=rw_w0, rw_w_up=rw_w_up,
             rw_a0=rw_a0, rw_a_up=rw_a_up, rw_g_up=rw_g_up, rw_k_k=rw_k_k, rw_k_a=rw_k_a,
             rw_r_k=rw_r_k, rw_gn_w=rw_gn_w, rw_gn_b=rw_gn_b, mb_conv_w=mb_conv_w,
             mb_conv_b=mb_conv_b, mb_dt_bias=mb_dt_bias, mb_a_log=mb_a_log, mb_d=mb_d,
             mb_norm_w=mb_norm_w, od_w_in=od_w_in, od_w_out=od_w_out, ret_norm_w=ret_norm_w,
             gla_gate_up=gla_gate_up, gla_gate_b=gla_gate_b, gla_norm_w=gla_norm_w)
    nb, seq, _ = x_prompt.shape
    db, dseq, _ = x_sample.shape
    P = [(_prep_even if l % 2 == 0 else _prep_odd)(l // 2, W) for l in range(DEPTH)]
    PW = _prep_proj(W)

    rows = db + nb
    rows_pad = -(-rows // SUBLANES) * SUBLANES
    c_all = jnp.concatenate([c_sample, c_prompt, jnp.zeros((rows_pad - rows, D_MODEL), F32)], axis=0)
    mod = _ada(c_all, ada_w, ada_b)

    states = (state_rwkv_shift, state_rwkv_wkv, state_mamba_conv, state_mamba_ssm, state_ret, state_gla)
    y_s, st_s, w16 = _trunk(x_sample.reshape(db * dseq, D_MODEL), mod, dict(row0=0, rps=dseq), states,
                            P, PW, (ffn_wg, ffn_wu, ffn_wd), final_g, nseq=db, length=dseq,
                            t=SAMPLE_CHUNK, g=SAMPLE_GROUP, lv=dseq, pos0=float(PAST_LEN))
    y_p, st_p, _ = _trunk(x_prompt.reshape(nb * seq, D_MODEL), mod, dict(row0=db, rps=seq), None,
                          P, PW, w16, final_g, nseq=nb, length=seq, t=PROMPT_CHUNK, g=1, lv=PROMPT_CHUNK,
                          pos0=0.0)
    return (y_p.reshape(nb, seq, D_MODEL), y_s.reshape(db, dseq, D_MODEL)) + st_p + st_s
```

```python
import functools
import math

import numpy as np
import jax
import jax.numpy as jnp
from jax import lax
from jax.experimental import pallas as pl
from jax.experimental.pallas import tpu as pltpu

F32 = jnp.float32
BF16 = jnp.bfloat16

D_MODEL = 2048
DEPTH = 4
PAST_LEN = 16384
N_MOD = 9
D_FF = 5632

D_MIX = 1024
H_A, HEAD_A = 16, 64
RW_LORA_W, RW_LORA_A, RW_LORA_G = 64, 64, 160
RW_COLS = 3 * D_MIX + RW_LORA_W + RW_LORA_A + RW_LORA_G
RW_GN_EPS = 64e-5
H_B, HEAD_B, D_STATE, N_GROUPS, CONV_W = 16, 64, 128, 2, 4
MB_CONV_CH = D_MIX + 2 * N_GROUPS * D_STATE
MB_COLS = D_MIX + MB_CONV_CH + H_B
H_C, HEAD_C = 8, 128
ROPE_BASE = 10000.0
H_D, HEAD_DK, HEAD_DV, GLA_LORA = 4, 128, 256, 16
GLA_NORMALIZER = 16.0
OD_COLS = 4 * D_MIX + 2 * H_D * HEAD_DK + 2 * D_MIX + GLA_LORA

LANES = 128
SUBLANES = 8
VMEM_LIMIT = 60 * 1024 * 1024

RW_PAD = 3456
ZM = RW_PAD
MB_PAD = 2688
EV_PAD = RW_PAD + MB_PAD
OD_PAD = 7296
OD_GD = 7168

PROMPT_CHUNK = 64
PROMPT_GROUP = 2
GLA_SUB = 32
SAMPLE_CHUNK = 8
SAMPLE_GROUP = 8

ROW_TILE = 512
NORM_ROWS = 256
FFN_ROW_TILE = 512


def _mm(a, b):
    return jnp.dot(a.astype(BF16), b.astype(BF16), preferred_element_type=F32)


def _mm_nt(a, b):
    return lax.dot_general(a.astype(BF16), b.astype(BF16), (((1,), (1,)), ((), ())),
                           preferred_element_type=F32)


def _mm_tn(a, b):
    return lax.dot_general(a.astype(BF16), b.astype(BF16), (((0,), (0,)), ((), ())),
                           preferred_element_type=F32)


def _split3(x):
    hi = x.astype(BF16)
    r = x - hi.astype(F32)
    mid = r.astype(BF16)
    lo = (r - mid.astype(F32)).astype(BF16)
    return hi, mid, lo


def _mm_sel(sel, x):
    s = sel.astype(BF16)
    hi, mid, lo = _split3(x)
    return (jnp.dot(s, hi, preferred_element_type=F32) + jnp.dot(s, mid, preferred_element_type=F32)
            + jnp.dot(s, lo, preferred_element_type=F32))


def _mm_tn_sel(x, sel):
    s = sel.astype(BF16)
    dn = (((0,), (0,)), ((), ()))
    hi, mid, lo = _split3(x)
    return (lax.dot_general(hi, s, dn, preferred_element_type=F32)
            + lax.dot_general(mid, s, dn, preferred_element_type=F32)
            + lax.dot_general(lo, s, dn, preferred_element_type=F32))


def _mm_xsel(x, sel):
    s = sel.astype(BF16)
    hi, mid, lo = _split3(x)
    return (jnp.dot(hi, s, preferred_element_type=F32) + jnp.dot(mid, s, preferred_element_type=F32)
            + jnp.dot(lo, s, preferred_element_type=F32))


def _sigmoid(x):
    return 1.0 / (1.0 + jnp.exp(-x))


def _silu(x):
    return x * _sigmoid(x)


def _softplus(x):
    return jnp.maximum(x, 0.0) + jnp.log(1.0 + jnp.exp(-jnp.abs(x)))


def _rms(x, eps):
    return x * lax.rsqrt(jnp.mean(x * x, axis=-1, keepdims=True) + eps)


def _mod_specs(mc, tm, idxs):
    l, rps, row0 = mc['l'], mc['rps'], mc['row0']
    specs = []
    for idx in idxs:
        if rps >= tm:
            specs.append(pl.BlockSpec(
                (None, None, SUBLANES, D_MODEL),
                lambda i, j, idx=idx: (l, idx, (row0 + i * tm // rps) // SUBLANES, 0)))
        else:
            assert row0 == 0
            specs.append(pl.BlockSpec((None, None, tm // rps, D_MODEL), lambda i, j, idx=idx: (l, idx, i, 0)))
    return specs


def _mod_val(ref, mc, tm):
    rps = mc['rps']
    if rps >= tm:
        r = (mc['row0'] + pl.program_id(0) * tm // rps) % SUBLANES
        return ref[pl.ds(r, 1), :]
    nseq = tm // rps
    row = lax.broadcasted_iota(jnp.int32, (tm, nseq), 0)
    first = lax.broadcasted_iota(jnp.int32, (tm, nseq), 1) * rps
    return _mm_sel((row >= first) & (row < first + rps), ref[...])


def _ada_kernel(c_ref, w_ref, b_ref, o_ref):
    x = _silu(c_ref[...]).astype(BF16)
    o_ref[...] = jnp.dot(x, w_ref[...].astype(BF16), preferred_element_type=F32) + b_ref[...]


def _ada(c, ada_w, ada_b):
    rows = c.shape[0]
    n = ada_w.shape[-1]
    tn = 1024
    per_mod = D_MODEL // tn
    return pl.pallas_call(
        _ada_kernel,
        grid=(DEPTH, n // tn),
        in_specs=[pl.BlockSpec((rows, D_MODEL), lambda l, j: (0, 0)),
                  pl.BlockSpec((None, D_MODEL, tn), lambda l, j: (l, 0, j)),
                  pl.BlockSpec((None, 1, tn), lambda l, j: (l, 0, j))],
        out_specs=pl.BlockSpec((None, None, rows, tn), lambda l, j: (l, j // per_mod, 0, j % per_mod)),
        out_shape=jax.ShapeDtypeStruct((DEPTH, N_MOD, rows, D_MODEL), F32),
        compiler_params=pltpu.CompilerParams(dimension_semantics=("arbitrary", "arbitrary"),
                                             vmem_limit_bytes=VMEM_LIMIT),
        name="ada",
    )(c, ada_w, ada_b.reshape(DEPTH, 1, n))


def _ffn_kernel(*refs, mc, tm, nj, final, emit):
    refs = list(refs)
    h_ref, sh_ref, sc_ref, g_ref, wg_ref, wu_ref, wd_ref = refs[:7]
    fin_ref = refs[7] if final else None
    o_ref = refs[7 + final]
    xn_ref = refs[-1]
    j = pl.program_id(1)

    if emit:
        wrefs = refs[8 + final:11 + final]
        for src, dst in zip((wg_ref, wu_ref, wd_ref), wrefs):
            dst[...] = src[...].astype(BF16)
    else:
        wrefs = (wg_ref, wu_ref, wd_ref)

    def swiglu(xn):
        gate = jnp.dot(xn, wrefs[0][...], preferred_element_type=F32)
        up = jnp.dot(xn, wrefs[1][...], preferred_element_type=F32)
        act = (_silu(gate) * up).astype(BF16)
        return jnp.dot(act, wrefs[2][...], preferred_element_type=F32)

    @pl.when(j == 0)
    def _():
        scale = 1.0 + _mod_val(sc_ref, mc, tm)
        shift = _mod_val(sh_ref, mc, tm)
        per_row = scale.shape[0] > 1
        for r in range(tm // NORM_ROWS):
            rows = slice(r * NORM_ROWS, (r + 1) * NORM_ROWS)
            xn = (_rms(h_ref[rows, :], 1e-6) * (scale[rows] if per_row else scale)
                  + (shift[rows] if per_row else shift)).astype(BF16)
            xn_ref[rows, :] = xn
            o_ref[rows, :] = swiglu(xn)

    @pl.when((j > 0) & (j < nj - 1))
    def _():
        o_ref[...] += swiglu(xn_ref[...])

    @pl.when(j == nj - 1)
    def _():
        gate = 0.5 * (1.0 + _mod_val(g_ref, mc, tm))
        per_row = gate.shape[0] > 1
        for r in range(tm // NORM_ROWS):
            rows = slice(r * NORM_ROWS, (r + 1) * NORM_ROWS)
            acc = o_ref[rows, :] + swiglu(xn_ref[rows, :])
            out = h_ref[rows, :] + (gate[rows] if per_row else gate) * acc
            if final:
                out = _rms(out, 1e-6) * fin_ref[...]
            o_ref[rows, :] = out


def _ffn(h, mod, mc, mod_base, w, k, final_g=None):
    m = h.shape[0]
    l = mc['l']
    emit = w[0].ndim == 4
    tm = min(FFN_ROW_TILE, m)
    tf = 256 if emit else 512
    nj = D_FF // tf
    final = final_g is not None
    in_specs = [pl.BlockSpec((tm, D_MODEL), lambda i, j: (i, 0))]
    in_specs += _mod_specs(mc, tm, (mod_base, mod_base + 1, mod_base + 2))
    col = pl.BlockSpec((D_MODEL, tf), lambda i, j: (0, j))
    row = pl.BlockSpec((tf, D_MODEL), lambda i, j: (j, 0))
    if emit:
        assert m == tm
        in_specs += [pl.BlockSpec((None, None, D_MODEL, tf), lambda i, j: (l, k, 0, j)),
                     pl.BlockSpec((None, None, D_MODEL, tf), lambda i, j: (l, k, 0, j)),
                     pl.BlockSpec((None, None, tf, D_MODEL), lambda i, j: (l, k, j, 0))]
    else:
        in_specs += [col, col, row]
    args = [h, mod, mod, mod, *w]
    if final:
        in_specs.append(pl.BlockSpec((1, D_MODEL), lambda i, j: (0, 0)))
        args.append(final_g.reshape(1, D_MODEL))
    out_specs = [pl.BlockSpec((tm, D_MODEL), lambda i, j: (i, 0))]
    out_shape = [jax.ShapeDtypeStruct((m, D_MODEL), F32)]
    if emit:
        out_specs += [col, col, row]
        out_shape += [jax.ShapeDtypeStruct((D_MODEL, D_FF), BF16), jax.ShapeDtypeStruct((D_MODEL, D_FF), BF16),
                      jax.ShapeDtypeStruct((D_FF, D_MODEL), BF16)]
    out = pl.pallas_call(
        functools.partial(_ffn_kernel, mc=mc, tm=tm, nj=nj, final=final, emit=emit),
        grid=(m // tm, nj),
        in_specs=in_specs, out_specs=out_specs, out_shape=out_shape,
        scratch_shapes=[pltpu.VMEM((tm, D_MODEL), BF16)],
        compiler_params=pltpu.CompilerParams(dimension_semantics=("arbitrary", "arbitrary"),
                                             vmem_limit_bytes=VMEM_LIMIT),
        name="ffn",
    )(*args)
    return (out[0], tuple(out[1:])) if emit else out[0]


def _inproj_kernel(h_ref, sh_ref, sc_ref, w_ref, o_ref, xn_ref, *, mc, tm):
    j = pl.program_id(1)

    @pl.when(j == 0)
    def _():
        scale = 1.0 + _mod_val(sc_ref, mc, tm)
        shift = _mod_val(sh_ref, mc, tm)
        per_row = scale.shape[0] > 1
        for r in range(tm // NORM_ROWS):
            rows = slice(r * NORM_ROWS, (r + 1) * NORM_ROWS)
            xn = (_rms(h_ref[rows, :], 1e-6) * (scale[rows] if per_row else scale)
                  + (shift[rows] if per_row else shift)).astype(BF16)
            xn_ref[rows, :] = xn
            o_ref[rows, :] = jnp.dot(xn, w_ref[...], preferred_element_type=F32)

    @pl.when(j > 0)
    def _():
        o_ref[...] = jnp.dot(xn_ref[...], w_ref[...], preferred_element_type=F32)


def _inproj(h, mod, mc, w, li):
    m = h.shape[0]
    n = w.shape[2]
    tm = min(ROW_TILE, m)
    tn = n // 3
    if n % (6 * LANES) == 0 and m % (2 * ROW_TILE) == 0:
        tm, tn = 2 * ROW_TILE, n // 6
    in_specs = [pl.BlockSpec((tm, D_MODEL), lambda i, j: (i, 0))]
    in_specs += _mod_specs(mc, tm, (3, 4))
    in_specs += [pl.BlockSpec((None, D_MODEL, tn), lambda i, j: (li, 0, j))]
    return pl.pallas_call(
        functools.partial(_inproj_kernel, mc=mc, tm=tm),
        grid=(m // tm, n // tn),
        in_specs=in_specs,
        out_specs=pl.BlockSpec((tm, tn), lambda i, j: (i, j)),
        out_shape=jax.ShapeDtypeStruct((m, n), F32),
        scratch_shapes=[pltpu.VMEM((tm, D_MODEL), BF16)],
        compiler_params=pltpu.CompilerParams(dimension_semantics=("arbitrary", "arbitrary"),
                                             vmem_limit_bytes=VMEM_LIMIT),
        name="inproj",
    )(h, mod, mod, w)


def _outproj_kernel(y_ref, w_ref, h_ref, g_ref, o_ref, *, mc, tm):
    mix = jnp.dot(y_ref[...], w_ref[...], preferred_element_type=F32)
    o_ref[...] = h_ref[...] + (1.0 + _mod_val(g_ref, mc, tm)) * mix


def _outproj(y, w, li, h, mod, mc):
    m = h.shape[0]
    tm = min(ROW_TILE, m)
    in_specs = [pl.BlockSpec((tm, D_MODEL), lambda i, j: (i, 0)),
                pl.BlockSpec((None, D_MODEL, D_MODEL), lambda i, j: (li, 0, 0)),
                pl.BlockSpec((tm, D_MODEL), lambda i, j: (i, 0))]
    in_specs += _mod_specs(mc, tm, (5,))
    return pl.pallas_call(
        functools.partial(_outproj_kernel, mc=mc, tm=tm),
        grid=(m // tm, 1),
        in_specs=in_specs,
        out_specs=pl.BlockSpec((tm, D_MODEL), lambda i, j: (i, 0)),
        out_shape=jax.ShapeDtypeStruct((m, D_MODEL), F32),
        compiler_params=pltpu.CompilerParams(dimension_semantics=("arbitrary", "arbitrary"),
                                             vmem_limit_bytes=VMEM_LIMIT),
        name="outproj",
    )(y, w, h, mod)


def _state_spec(li, nl, g, tail, whole):
    zeros = (0,) * len(tail)
    if whole:
        return pl.BlockSpec((nl, g) + tail, lambda b, c: (0, b) + zeros)
    return pl.BlockSpec((None, g) + tail, lambda b, c: (li, b) + zeros)


def _state_out(stacked, li, nseq, g, tail, whole):
    if stacked is None:
        zeros = (0,) * len(tail)
        return (jax.ShapeDtypeStruct((nseq,) + tail, F32),
                pl.BlockSpec((g,) + tail, lambda b, c: (b,) + zeros))
    return jax.ShapeDtypeStruct(stacked.shape, F32), _state_spec(li, stacked.shape[0], g, tail, whole)


def _layer_views(li, nl, pairs):
    for src, dst in pairs:
        for lj in range(nl):
            if lj != li:
                dst[lj] = src[lj]
    return [r.at[li] for pair in pairs for r in pair]


def _seq_masks(g, t):
    n = g * t
    r = lax.broadcasted_iota(jnp.int32, (n, n), 0)
    c = lax.broadcasted_iota(jnp.int32, (n, n), 1)
    if g == 1:
        return r > c, r >= c
    same = (r // t) == (c // t)
    return same & (r > c), same & (r >= c)


def _seq_selectors(g, t):
    r = np.arange(g * t)[:, None]
    c = np.arange(g * t)[None, :]
    same = (r // t) == (c // t)
    incl = same & (r >= c)
    last = c == (r // t) * t + (t - 1)
    return jnp.asarray(np.stack([incl, last, incl.T]), BF16)


EV_W0, EV_A0, EV_KK, EV_KA, EV_RK, EV_GNW, EV_GNB, EV_D, EV_NW = range(9)
EV_VEC_ROWS = 16


def _even_kernel(*refs, t, g, nchunks, lv, has_init, ncarry, whole):
    if has_init:
        z_ref, shift0_ref, wkv0_ref, conv0_ref, ssm0_ref, *rest = refs
    else:
        z_ref, *rest = refs
    mu_ref, vec_ref, wa_ref, gup_ref, cw_ref, cb_ref, dtb_ref, alog_ref, exp_ref, sel_ref, *rest = rest
    (y_ref, shift_o, wkv_o, conv_o, ssm_o,
     carry_ref, wkv_ref, ext_ref, ssm_ref,
     r_s, kp_s, v_s, kk_s, a_s, c_s, ld_s, g_s, cl_s, y_s) = rest[ncarry:]
    n = g * t
    ci = pl.program_id(1)
    masked = lv < t
    nlog = int(math.log2(t))
    seqs = range(g)
    hist = SUBLANES - (CONV_W - 1)
    direct = has_init and nchunks == 1
    if whole:
        assert direct
        wkv0_ref, wkv_o, ssm0_ref, ssm_o = _layer_views(*whole, [(wkv0_ref, wkv_o), (ssm0_ref, ssm_o)])
    wkv_src, wkv_dst = (wkv0_ref, wkv_o) if direct else (wkv_ref, wkv_ref)
    ssm_src, ssm_dst = (ssm0_ref, ssm_o) if direct else (ssm_ref, ssm_ref)

    @pl.when(ci == 0)
    def _():
        if has_init:
            carry_ref[:, :, 0:RW_COLS] = shift0_ref[...]
            carry_ref[:, :, RW_COLS:RW_PAD] = jnp.zeros((g, 1, RW_PAD - RW_COLS), F32)
            ext_ref[:, 0:hist, :] = jnp.zeros((g, hist, MB_CONV_CH), F32)
            ext_ref[:, hist:SUBLANES, :] = conv0_ref[...]
            if not direct:
                wkv_ref[...] = wkv0_ref[...]
                ssm_ref[...] = ssm0_ref[...]
        else:
            carry_ref[...] = jnp.zeros_like(carry_ref)
            wkv_ref[...] = jnp.zeros_like(wkv_ref)
            ext_ref[:, 0:SUBLANES, :] = jnp.zeros((g, SUBLANES, MB_CONV_CH), F32)
            ssm_ref[...] = jnp.zeros_like(ssm_ref)

    tok = lax.broadcasted_iota(jnp.int32, (n, 1), 0) % t
    valid = tok < lv
    strict, incl = _seq_masks(g, t)
    sel_incl, sel_last, sel_upper = sel_ref[0], sel_ref[1], sel_ref[2]

    def vec(i, lo=0, hi=D_MIX):
        return vec_ref[i:i + 1, lo:hi]

    def rows_of(x, s):
        return x[s * t:(s + 1) * t]

    def per_seq(fn):
        parts = [fn(s) for s in seqs]
        return parts[0] if g == 1 else jnp.concatenate(parts, axis=0)

    def shifted(a, b):
        cur = z_ref[:, :, a:b].reshape(n, b - a)
        carry = per_seq(lambda s: jnp.broadcast_to(carry_ref[s, :, a:b], (t, b - a)))
        prev = jnp.where(tok == 0, carry, pltpu.roll(cur, 1, axis=0))
        return cur + mu_ref[:, a:b] * (prev - cur)

    r_s[...] = shifted(0, D_MIX)
    k = shifted(D_MIX, 2 * D_MIX)
    v_s[...] = shifted(2 * D_MIX, 3 * D_MIX)
    lo = shifted(3 * D_MIX, 3 * D_MIX + LANES)
    lane = lax.broadcasted_iota(jnp.int32, (n, LANES), 1)
    wa = _mm(jnp.where(lane < RW_LORA_W, jnp.tanh(lo), lo), wa_ref[...])
    w_log = -_softplus(-(vec(EV_W0) + wa[:, 0:D_MIX])) - 0.5
    ld = -jnp.exp(w_log)
    a = _sigmoid(vec(EV_A0) + wa[:, D_MIX:2 * D_MIX])
    g_s[...] = _mm(_sigmoid(shifted(3 * D_MIX + LANES, RW_PAD)), gup_ref[...])
    kp = k * (1.0 + (a - 1.0) * vec(EV_KA))
    if masked:
        ld = jnp.where(valid, ld, 0.0)
        kp = jnp.where(valid, kp, 0.0)
        a = jnp.where(valid, a, 0.0)
    kk_s[...] = k * vec(EV_KK)
    kp_s[...] = kp
    a_s[...] = a
    ld_s[...] = ld
    c = _mm_sel(sel_incl, ld)
    c_s[...] = c
    cl_s[...] = _mm_sel(sel_last, c)
    carry_ref[...] = z_ref[:, lv - 1:lv, 0:RW_PAD]

    heads = range(H_A)
    sls = [slice(h * HEAD_A, (h + 1) * HEAD_A) for h in heads]
    rs = [r_s[:, sl] for sl in sls]
    kps = [kp_s[:, sl] for sl in sls]
    vs = [v_s[:, sl] for sl in sls]
    cs = [c_s[:, sl] for sl in sls]
    cls = [cl_s[:, sl] for sl in sls]
    kks, bs = [], []
    for sl in sls:
        kkr = kk_s[:, sl]
        nrm = jnp.sqrt(jnp.sum(kkr * kkr, axis=-1, keepdims=True))
        kks.append(kkr / jnp.maximum(nrm, 1e-12))
        bs.append(kks[-1] * a_s[:, sl])
    ecis = [jnp.exp(-c) for c in cs]
    kts = [kk * jnp.exp(c - ld_s[:, sl]) for kk, c, sl in zip(kks, cs, sls)]
    rts = [r * jnp.exp(c) for r, c in zip(rs, cs)]
    lhss = [jnp.concatenate([kt, rt], axis=0) for kt, rt in zip(kts, rts)]
    a1s = [_mm_nt(lhs, kp * eci) for lhs, kp, eci in zip(lhss, kps, ecis)]
    a2s = [_mm_nt(lhs, b * eci) for lhs, b, eci in zip(lhss, bs, ecis)]
    pms = [[_mm_nt(jnp.concatenate([rows_of(kt, s), rows_of(rt, s)], axis=0), wkv_src[s, h]) for s in seqs]
           for h, kt, rt in zip(heads, kts, rts)]
    pmk = [per_seq(lambda s: pm[s][0:t]) for pm in pms]
    pmr = [per_seq(lambda s: pm[s][t:2 * t]) for pm in pms]
    us = [pk + _mm(jnp.where(strict, a1[0:n], 0.0), v) for pk, a1, v in zip(pmk, a1s, vs)]
    ps = [jnp.where(strict, -a2[0:n], 0.0) for a2 in a2s]
    us = [u + _mm(p, u) for u, p in zip(us, ps)]
    for _ in range(nlog - 1):
        ps = [_mm(p, p) for p in ps]
        us = [u + _mm(p, u) for u, p in zip(us, ps)]
    ys = [pr + _mm(jnp.where(incl, a1[n:2 * n], 0.0), v) - _mm(jnp.where(incl, a2[n:2 * n], 0.0), u)
          for pr, a1, a2, v, u in zip(pmr, a1s, a2s, vs, us)]
    for h in heads:
        dec_end = jnp.exp(cls[h] - cs[h])
        kbar = kps[h] * dec_end
        bbar = bs[h] * dec_end
        for s in seqs:
            vu = jnp.concatenate([rows_of(vs[h], s), rows_of(us[h], s)], axis=0)
            kb = jnp.concatenate([rows_of(kbar, s), -rows_of(bbar, s)], axis=0)
            wkv_dst[s, h] = wkv_src[s, h] * jnp.exp(cls[h][s * t:s * t + 1]) + _mm_tn(vu, kb)
    for h, sl in zip(heads, sls):
        y, r, kp, v = ys[h], rs[h], kps[h], vs[h]
        mean = jnp.mean(y, axis=-1, keepdims=True)
        yc = y - mean
        var = jnp.mean(yc * yc, axis=-1, keepdims=True)
        yn = (yc * lax.rsqrt(var + RW_GN_EPS) * vec(EV_GNW, sl.start, sl.stop)
              + vec(EV_GNB, sl.start, sl.stop))
        bonus = jnp.sum(r * kp * vec(EV_RK, sl.start, sl.stop), axis=-1, keepdims=True) * v
        y_s[:, sl] = (yn + bonus) * g_s[:, sl]

    zg = z_ref[:, :, ZM:ZM + D_MIX].reshape(n, D_MIX)
    ext_ref[:, SUBLANES:SUBLANES + t, :] = z_ref[:, :, ZM + D_MIX:ZM + D_MIX + MB_CONV_CH]
    conv = cb_ref[...]
    for w in range(CONV_W):
        conv = conv + cw_ref[w:w + 1, :] * ext_ref[:, hist + w:hist + w + t, :].reshape(n, MB_CONV_CH)
    new_conv = ext_ref[:, SUBLANES + lv - (CONV_W - 1):SUBLANES + lv, :]
    ext_ref[:, hist:SUBLANES, :] = new_conv
    xbc = _silu(conv)
    xs = xbc[:, 0:D_MIX]
    dt = _softplus(z_ref[:, :, ZM + D_MIX + MB_CONV_CH:ZM + MB_PAD].reshape(n, LANES) + dtb_ref[...])
    logd = dt * (-jnp.exp(alog_ref[...]))
    if masked:
        logd = jnp.where(valid, logd, 0.0)
    c = _mm_sel(sel_incl, logd)
    c_t = _mm_tn_sel(logd, sel_upper)
    c_end = _mm_sel(sel_last, c)
    wide = _mm_xsel(jnp.concatenate([dt, c, c_end - c], axis=0), exp_ref[...])
    xdt = xs * wide[0:n]
    if masked:
        xdt = jnp.where(valid, xdt, 0.0)
    ec_w = jnp.exp(wide[n:2 * n])
    xdec = xdt * jnp.exp(wide[2 * n:3 * n])
    hpg = H_B // N_GROUPS
    bgs = [xbc[:, D_MIX + q * D_STATE:D_MIX + (q + 1) * D_STATE] for q in range(N_GROUPS)]
    cgs = [xbc[:, D_MIX + (N_GROUPS + q) * D_STATE:D_MIX + (N_GROUPS + q + 1) * D_STATE]
           for q in range(N_GROUPS)]
    scores = [_mm_nt(cg, bg) for cg, bg in zip(cgs, bgs)]
    segs = [jnp.where(incl, jnp.exp(jnp.where(incl, c[:, h:h + 1] - c_t[h:h + 1, :], 0.0)), 0.0)
            for h in range(H_B)]
    hsl = [slice(h * HEAD_B, (h + 1) * HEAD_B) for h in range(H_B)]
    intra = [_mm(scores[h // hpg] * segs[h], xdt[:, hsl[h]]) for h in range(H_B)]
    grp = lambda s, q: ssm_src[s, q * hpg:(q + 1) * hpg].reshape(hpg * HEAD_B, D_STATE)
    inter = [per_seq(lambda s: _mm_nt(rows_of(cgs[q], s), grp(s, q))) for q in range(N_GROUPS)]
    for q in range(N_GROUPS):
        for s in seqs:
            upd = _mm_tn(rows_of(xdec[:, q * hpg * HEAD_B:(q + 1) * hpg * HEAD_B], s), rows_of(bgs[q], s))
            last = s * t + t - 1
            for h in range(q * hpg, (q + 1) * hpg):
                ssm_dst[s, h] = (ssm_src[s, h] * jnp.exp(c[last:last + 1, h:h + 1])
                                 + upd[(h - q * hpg) * HEAD_B:(h - q * hpg + 1) * HEAD_B])
    o = jnp.concatenate(intra, axis=1) + jnp.concatenate(inter, axis=1) * ec_w
    yb = (o + vec(EV_D) * xs) * _silu(zg)
    y_s[:, D_MIX:2 * D_MIX] = _rms(yb, 1e-5) * vec(EV_NW)
    y_ref[...] = y_s[...].astype(BF16).reshape(g, t, D_MODEL)

    @pl.when(ci == nchunks - 1)
    def _():
        shift_o[...] = carry_ref[...]
        conv_o[...] = new_conv
        if not direct:
            wkv_o[...] = wkv_ref[...]
            ssm_o[...] = ssm_ref[...]


def _carry_specs(carriers, first_out):
    return ([pl.BlockSpec(memory_space=pl.ANY)] * len(carriers), list(carriers),
            lambda nargs: {nargs + i: first_out[i] for i in range(len(carriers))})


def _even_mixer(z, states, carriers, li, p, *, t, g, lv):
    nseq, length, _ = z.shape
    nchunks = length // t
    has_init = states is not None
    assert (lv == t or nchunks == 1) and nseq % g == 0
    full = lambda *shape: pl.BlockSpec(shape, lambda b, c: (0,) * len(shape))
    in_specs = [pl.BlockSpec((g, t, EV_PAD), lambda b, c: (b, c, 0))]
    args = [z]
    ssm_t = None
    whole = has_init and not carriers
    nl = states[1].shape[0] if has_init else 0
    if has_init:
        shift0, wkv0, conv0, ssm0 = states
        ssm_t = jnp.swapaxes(ssm0, 3, 4)
        in_specs += [pl.BlockSpec((None, g, 1, RW_COLS), lambda b, c: (li, b, 0, 0)),
                     _state_spec(li, nl, g, (H_A, HEAD_A, HEAD_A), whole),
                     pl.BlockSpec((None, g, CONV_W - 1, MB_CONV_CH), lambda b, c: (li, b, 0, 0)),
                     _state_spec(li, nl, g, (H_B, HEAD_B, D_STATE), whole)]
        args += [shift0.reshape(-1, nseq, 1, RW_COLS), wkv0, conv0, ssm_t]
    in_specs += [full(1, RW_PAD), full(EV_VEC_ROWS, D_MIX), full(LANES, 2 * D_MIX),
                 full(RW_PAD - 3 * D_MIX - LANES, D_MIX), full(CONV_W, MB_CONV_CH),
                 full(1, MB_CONV_CH), full(1, LANES), full(1, LANES), full(LANES, D_MIX),
                 full(3, g * t, g * t)]
    args += [p['mu'], p['vec'], p['wa'], p['gup'], p['conv_w'], p['conv_b'], p['dt_bias'],
             p['a_log'], p['expand'], _seq_selectors(g, t)]
    carry_specs, carry_args, carry_alias = _carry_specs(carriers, (2, 4))
    aliases = carry_alias(len(args))
    in_specs += carry_specs
    args += carry_args
    wkv_shape, wkv_spec = _state_out(states and states[1], li, nseq, g, (H_A, HEAD_A, HEAD_A), whole)
    ssm_shape, ssm_spec = _state_out(ssm_t, li, nseq, g, (H_B, HEAD_B, D_STATE), whole)
    out_shape = (jax.ShapeDtypeStruct((nseq, length, D_MODEL), BF16),
                 jax.ShapeDtypeStruct((nseq, 1, RW_PAD), F32), wkv_shape,
                 jax.ShapeDtypeStruct((nseq, CONV_W - 1, MB_CONV_CH), F32), ssm_shape)
    out_specs = (pl.BlockSpec((g, t, D_MODEL), lambda b, c: (b, c, 0)),
                 pl.BlockSpec((g, 1, RW_PAD), lambda b, c: (b, 0, 0)), wkv_spec,
                 pl.BlockSpec((g, CONV_W - 1, MB_CONV_CH), lambda b, c: (b, 0, 0)), ssm_spec)
    n = g * t
    wide = lambda: pltpu.VMEM((n, D_MIX), F32)
    gs = 1 if (has_init and nchunks == 1) else g
    scratch = [pltpu.VMEM((g, 1, RW_PAD), F32), pltpu.VMEM((gs, H_A, HEAD_A, HEAD_A), F32),
               pltpu.VMEM((g, t + SUBLANES, MB_CONV_CH), F32), pltpu.VMEM((gs, H_B, HEAD_B, D_STATE), F32)]
    scratch += [wide() for _ in range(9)] + [pltpu.VMEM((n, D_MODEL), F32)]
    y, shift, wkv, conv, ssm = pl.pallas_call(
        functools.partial(_even_kernel, t=t, g=g, nchunks=nchunks, lv=lv, has_init=has_init,
                          ncarry=len(carriers), whole=(li, nl) if whole else None),
        grid=(nseq // g, nchunks),
        in_specs=in_specs, out_specs=out_specs, out_shape=out_shape, scratch_shapes=scratch,
        input_output_aliases=aliases,
        compiler_params=pltpu.CompilerParams(dimension_semantics=("arbitrary", "arbitrary"),
                                             vmem_limit_bytes=VMEM_LIMIT),
        name="even_mixer",
    )(*args)
    return y, (shift[:, 0, :RW_COLS], wkv, conv, ssm)


def _odd_kernel(*refs, t, tsub, g, nchunks, lv, has_init, ncarry, whole):
    if has_init:
        z_ref, ret0_ref, gla0_ref, *rest = refs
    else:
        z_ref, *rest = refs
    cos_ref, sin_ref, seg_ref, pw_ref, dte_ref, rnw_ref, gup_ref, gb_ref, gnw_ref, sel_ref, *rest = rest
    y_ref, ret_o, gla_o, ret_ref, gla_ref, la_s, y_s = rest[ncarry:]
    n = g * t
    ci = pl.program_id(1)
    masked = lv < t
    seqs = range(g)
    direct = has_init and nchunks == 1
    if whole:
        assert direct
        ret0_ref, ret_o, gla0_ref, gla_o = _layer_views(*whole, [(ret0_ref, ret_o), (gla0_ref, gla_o)])
    ret_src, ret_dst = (ret0_ref, ret_o) if direct else (ret_ref, ret_ref)

    @pl.when(ci == 0)
    def _():
        if has_init:
            if not direct:
                ret_ref[...] = ret0_ref[...]
            for s in seqs:
                for h in range(H_D):
                    gla_ref[s, h] = jnp.transpose(gla0_ref[s, h])
        else:
            ret_ref[...] = jnp.zeros_like(ret_ref)
            gla_ref[...] = jnp.zeros_like(gla_ref)

    valid = lax.broadcasted_iota(jnp.int32, (n, 1), 0) % t < lv

    def zcols(a, b):
        return z_ref[:, :, a:b].reshape(n, b - a)

    def per_seq(fn, rows):
        parts = [fn(s, slice(s * rows, (s + 1) * rows)) for s in seqs]
        return parts[0] if g == 1 else jnp.concatenate(parts, axis=0)

    cosf = cos_ref[...]
    sinf = sin_ref[...]

    def rotary(x):
        return x * cosf + pltpu.roll(x, HEAD_C // 2, axis=1) * sinf

    hc = range(H_C)
    qs = [rotary(zcols(h * HEAD_C, (h + 1) * HEAD_C)) * HEAD_C ** -0.5 for h in hc]
    ks = [rotary(zcols(D_MIX + h * HEAD_C, D_MIX + (h + 1) * HEAD_C)) for h in hc]
    if masked:
        ks = [jnp.where(valid, k, 0.0) for k in ks]
    vs = [zcols(2 * D_MIX + h * HEAD_C, 2 * D_MIX + (h + 1) * HEAD_C) for h in hc]
    sc = [_mm_nt(q, k) * seg_ref[h] for h, q, k in zip(hc, qs, ks)]
    inter = [per_seq(lambda s, rw: _mm(qs[h][rw], ret_src[s, h]), t) * pw_ref[h] for h in hc]
    outs = [_mm(s_, v) + i for s_, v, i in zip(sc, vs, inter)]
    for h in hc:
        gamma = 1.0 - 2.0 ** (-5.0 - h)
        kd = ks[h] * dte_ref[h]
        for s in seqs:
            rw = slice(s * t, (s + 1) * t)
            ret_dst[s, h] = ret_src[s, h] * gamma ** lv + _mm_tn(kd[rw], vs[h][rw])
    for h in hc:
        sl = slice(h * HEAD_C, (h + 1) * HEAD_C)
        rg = zcols(3 * D_MIX + h * HEAD_C, 3 * D_MIX + (h + 1) * HEAD_C)
        y_s[:, sl] = _rms(outs[h], 1e-6) * rnw_ref[:, sl] * _silu(rg)

    gq0 = 4 * D_MIX
    gk0 = gq0 + H_D * HEAD_DK
    gv0 = gk0 + H_D * HEAD_DK
    gr0 = gv0 + D_MIX
    x = _mm(zcols(OD_GD, OD_GD + LANES), gup_ref[...]) + gb_ref[...]
    la = -_softplus(-x) * (1.0 / GLA_NORMALIZER)
    if masked:
        la = jnp.where(valid, la, 0.0)
    la_s[...] = la
    assert g == 1 or tsub == t
    incl = _seq_masks(g, tsub)[1]
    hd = range(H_D)
    for u in range(t // tsub):
        rs = slice(0, n) if g > 1 else slice(u * tsub, (u + 1) * tsub)

        def cols(a, b, rs=rs):
            return z_ref[0, rs, a:b] if g == 1 else zcols(a, b)

        cum_all = _mm_sel(sel_ref[0], la_s[rs, :])
        cums = [cum_all[:, h * HEAD_DK:(h + 1) * HEAD_DK] for h in hd]
        qes = [cols(gq0 + h * HEAD_DK, gq0 + (h + 1) * HEAD_DK) * HEAD_DK ** -0.5 * jnp.exp(cums[h])
               for h in hd]
        ks = [cols(gk0 + h * HEAD_DK, gk0 + (h + 1) * HEAD_DK) for h in hd]
        if masked:
            ks = [jnp.where(valid[rs], k, 0.0) for k in ks]
        vs = [cols(gv0 + h * HEAD_DV, gv0 + (h + 1) * HEAD_DV) for h in hd]
        scores = [jnp.where(incl, _mm_nt(qes[h], ks[h] * jnp.exp(-cums[h])), 0.0) for h in hd]
        inter = [per_seq(lambda s, rw: _mm_nt(qes[h][rw], gla_ref[s, h]), tsub) for h in hd]
        outs = [_mm(scores[h], vs[h]) + inter[h] for h in hd]
        for h in hd:
            for s in seqs:
                rw = slice(s * tsub, (s + 1) * tsub)
                cum = cums[h][rw]
                last = cum[tsub - 1:tsub]
                gla_ref[s, h] = (gla_ref[s, h] * jnp.exp(last)
                                 + _mm_tn(vs[h][rw], ks[h][rw] * jnp.exp(last - cum)))
        for h in hd:
            gr = cols(gr0 + h * HEAD_DV, gr0 + (h + 1) * HEAD_DV)
            ys = slice(D_MIX + h * HEAD_DV, D_MIX + (h + 1) * HEAD_DV)
            y_s[rs, ys] = _rms(outs[h], 1e-6) * gnw_ref[:, h * HEAD_DV:(h + 1) * HEAD_DV] * _silu(gr)
    y_ref[...] = y_s[...].reshape(g, t, D_MODEL).astype(BF16)

    @pl.when(ci == nchunks - 1)
    def _():
        if not direct:
            ret_o[...] = ret_ref[...]
        for s in seqs:
            for h in range(H_D):
                gla_o[s, h] = jnp.transpose(gla_ref[s, h])


def _ret_tables(t, g, lv, pos0, nchunks):
    half = HEAD_C // 2
    inv = ROPE_BASE ** (-jnp.arange(half, dtype=F32) / half)
    pos = pos0 + jnp.arange(t * nchunks, dtype=F32).reshape(nchunks, 1, t)
    pos = jnp.broadcast_to(pos, (nchunks, g, t)).reshape(-1)
    ang = pos[:, None] * inv[None, :]
    cos, sin = jnp.cos(ang), jnp.sin(ang)
    cosf = jnp.concatenate([cos, cos], axis=-1)
    sinf = jnp.concatenate([-sin, sin], axis=-1)
    gam = 1.0 - np.exp2(-5.0 - np.arange(H_C, dtype=np.float64))
    ti = np.arange(t)
    d = ti[:, None] - ti[None, :]
    seg = np.where(d >= 0, gam[:, None, None] ** np.maximum(d, 0)[None], 0.0)
    seg = np.stack([np.kron(np.eye(g), s) for s in seg])
    pw = np.tile(gam[:, None] ** (ti + 1)[None], (1, g))
    dte = np.tile(np.where(ti < lv, gam[:, None] ** np.maximum(lv - 1 - ti, 0)[None], 0.0), (1, g))
    lanes = lambda a: jnp.asarray(np.broadcast_to(a[:, :, None], a.shape + (LANES,)), F32)
    return cosf, sinf, jnp.asarray(seg, F32), lanes(pw), lanes(dte)


def _odd_mixer(z, states, carriers, li, p, *, t, g, lv, pos0):
    nseq, length, _ = z.shape
    nchunks = length // t
    has_init = states is not None
    assert (lv == t or nchunks == 1) and nseq % g == 0
    tsub = min(GLA_SUB, t)
    n = g * t
    nsub = n if g > 1 else tsub
    cosf, sinf, seg, pw, dte = _ret_tables(t, g, lv, pos0, nchunks)
    full = lambda *shape: pl.BlockSpec(shape, lambda b, c: (0,) * len(shape))
    in_specs = [pl.BlockSpec((g, t, OD_PAD), lambda b, c: (b, c, 0))]
    args = [z]
    whole = has_init and not carriers
    nl = states[0].shape[0] if has_init else 0
    if has_init:
        in_specs += [_state_spec(li, nl, g, (H_C, HEAD_C, HEAD_C), whole),
                     _state_spec(li, nl, g, (H_D, HEAD_DK, HEAD_DV), whole)]
        args += list(states)
    in_specs += [pl.BlockSpec((n, HEAD_C), lambda b, c: (c, 0)),
                 pl.BlockSpec((n, HEAD_C), lambda b, c: (c, 0)),
                 full(H_C, n, n), full(H_C, n, LANES), full(H_C, n, LANES),
                 full(1, D_MIX), full(LANES, H_D * HEAD_DK), full(1, H_D * HEAD_DK), full(1, D_MIX),
                 full(3, nsub, nsub)]
    args += [cosf, sinf, seg, pw, dte, p['ret_norm_w'], p['gate_up'], p['gate_b'], p['gla_norm_w'],
             _seq_selectors(g if g > 1 else 1, tsub)]
    carry_specs, carry_args, carry_alias = _carry_specs(carriers, (1, 2))
    aliases = carry_alias(len(args))
    in_specs += carry_specs
    args += carry_args
    ret_shape, ret_spec = _state_out(states and states[0], li, nseq, g, (H_C, HEAD_C, HEAD_C), whole)
    gla_shape, gla_spec = _state_out(states and states[1], li, nseq, g, (H_D, HEAD_DK, HEAD_DV), whole)
    out_shape = (jax.ShapeDtypeStruct((nseq, length, D_MODEL), BF16), ret_shape, gla_shape)
    out_specs = (pl.BlockSpec((g, t, D_MODEL), lambda b, c: (b, c, 0)), ret_spec, gla_spec)
    gs = 1 if (has_init and nchunks == 1) else g
    scratch = [pltpu.VMEM((gs, H_C, HEAD_C, HEAD_C), F32), pltpu.VMEM((g, H_D, HEAD_DV, HEAD_DK), F32),
               pltpu.VMEM((n, H_D * HEAD_DK), F32), pltpu.VMEM((n, D_MODEL), F32)]
    y, ret, gla = pl.pallas_call(
        functools.partial(_odd_kernel, t=t, tsub=tsub, g=g, nchunks=nchunks, lv=lv, has_init=has_init,
                          ncarry=len(carriers), whole=(li, nl) if whole else None),
        grid=(nseq // g, nchunks),
        in_specs=in_specs, out_specs=out_specs, out_shape=out_shape, scratch_shapes=scratch,
        input_output_aliases=aliases,
        compiler_params=pltpu.CompilerParams(dimension_semantics=("arbitrary", "arbitrary"),
                                             vmem_limit_bytes=VMEM_LIMIT),
        name="odd_mixer",
    )(*args)
    return y, (ret, gla)


def _prep_proj(W):
    w = W['ev_w_in']
    n = w.shape[0]
    ev_in = jnp.concatenate([w[:, :, :RW_COLS], jnp.zeros((n, D_MODEL, RW_PAD - RW_COLS), F32),
                             w[:, :, RW_COLS:], jnp.zeros((n, D_MODEL, MB_PAD - MB_COLS), F32)], axis=2)
    w = W['od_w_in']
    n = w.shape[0]
    gd0 = OD_COLS - D_MIX - GLA_LORA
    od_in = jnp.concatenate([w[:, :, :gd0], w[:, :, gd0 + GLA_LORA:], w[:, :, gd0:gd0 + GLA_LORA],
                             jnp.zeros((n, D_MODEL, OD_PAD - OD_COLS), F32)], axis=2)
    return dict(ev_in=ev_in.astype(BF16), od_in=od_in.astype(BF16),
                ev_out=W['ev_w_out'].astype(BF16), od_out=W['od_w_out'].astype(BF16))


def _prep_even(i, W):
    mu = jnp.pad(W['rw_mu'][i], (0, RW_PAD - RW_COLS)).reshape(1, RW_PAD)
    rep = lambda v: jnp.repeat(v, HEAD_B)
    rows = [W['rw_w0'][i], W['rw_a0'][i], W['rw_k_k'][i], W['rw_k_a'][i], W['rw_r_k'][i].reshape(-1),
            W['rw_gn_w'][i], W['rw_gn_b'][i], rep(W['mb_d'][i]), W['mb_norm_w'][i]]
    vec = jnp.concatenate([jnp.stack(rows), jnp.zeros((EV_VEC_ROWS - len(rows), D_MIX), F32)], axis=0)
    wa = jnp.zeros((LANES, 2 * D_MIX), F32)
    wa = wa.at[:RW_LORA_W, :D_MIX].set(W['rw_w_up'][i]).at[RW_LORA_W:, D_MIX:].set(W['rw_a_up'][i])
    gup = jnp.pad(W['rw_g_up'][i], ((0, RW_PAD - 3 * D_MIX - LANES - RW_LORA_G), (0, 0)))
    pad_h = lambda v: jnp.pad(v, (0, LANES - H_B)).reshape(1, LANES)
    expand = np.zeros((LANES, D_MIX), np.float32)
    for h in range(H_B):
        expand[h, h * HEAD_B:(h + 1) * HEAD_B] = 1.0
    return dict(mu=mu, vec=vec, wa=wa.astype(BF16), gup=gup.astype(BF16), conv_w=W['mb_conv_w'][i],
                conv_b=W['mb_conv_b'][i].reshape(1, MB_CONV_CH), dt_bias=pad_h(W['mb_dt_bias'][i]),
                a_log=pad_h(W['mb_a_log'][i]), expand=jnp.asarray(expand, BF16))


def _prep_odd(i, W):
    gate_up = jnp.pad(W['gla_gate_up'][i], ((0, LANES - GLA_LORA), (0, 0)))
    return dict(ret_norm_w=W['ret_norm_w'][i].reshape(1, D_MIX), gate_up=gate_up.astype(BF16),
                gate_b=W['gla_gate_b'][i].reshape(1, H_D * HEAD_DK),
                gla_norm_w=W['gla_norm_w'][i].reshape(1, D_MIX))


def _trunk(h, mod, mc0, states, P, PW, FW, final_g, *, nseq, length, t, g, lv, pos0):
    outs = ([], [], [], [], [], [])
    padded = -(-length // t) * t
    carried = states is not None
    carriers = {True: (), False: ()}
    stacked_f32 = isinstance(FW, tuple)
    bf16_weights = []

    def ffn(h, mc, mod_base, k, final_g=None):
        if not stacked_f32:
            return _ffn(h, mod, mc, mod_base, FW[2 * mc['l'] + k], k, final_g)
        h, w16 = _ffn(h, mod, mc, mod_base, FW, k, final_g)
        bf16_weights.append(w16)
        return h

    for l in range(DEPTH):
        i = l // 2
        p = P[l]
        mc = dict(mc0, l=l)
        even = l % 2 == 0
        h = ffn(h, mc, 0, 0)
        z = _inproj(h, mod, mc, PW['ev_in' if even else 'od_in'], i).reshape(nseq, length, -1)
        if padded != length:
            z = jnp.pad(z, ((0, 0), (0, padded - length), (0, 0)))
        gl = max(1, g // 2) if carried and not carriers[even] else g
        if even:
            y, new = _even_mixer(z, states and states[:4], carriers[even], i, p, t=t, g=gl, lv=lv)
            slots, large = (0, 1, 2, 3), (1, 3)
        else:
            go = gl if min(GLA_SUB, t) == t else 1
            y, new = _odd_mixer(z, states and states[4:], carriers[even], i, p, t=t, g=go, lv=lv, pos0=pos0)
            slots, large = (4, 5), (4, 5)
        if carried:
            carriers[even] = tuple(n for k, n in zip(slots, new) if k in large)
        for k, n in zip(slots, new):
            if not (carried and k in large):
                outs[k].append(n)
        y = y[:, :length].reshape(nseq * length, D_MODEL)
        h = _outproj(y, PW['ev_out' if even else 'od_out'], i, h, mod, mc)
        h = ffn(h, mc, 6, 1, final_g=final_g if l == DEPTH - 1 else None)
    res = [jnp.stack(lst) if lst else None for lst in outs]
    if carried:
        res[1], res[3] = carriers[True]
        res[4], res[5] = carriers[False]
    res[3] = jnp.swapaxes(res[3], -1, -2)
    return h, tuple(res), bf16_weights


def kernel(x_prompt, x_sample, state_rwkv_shift, state_rwkv_wkv, state_mamba_conv, state_mamba_ssm,
           state_ret, state_gla, c_prompt, c_sample, ada_w, ada_b, ffn_wg, ffn_wu, ffn_wd, ev_w_in,
           ev_w_out, rw_mu, rw_w0, rw_w_up, rw_a0, rw_a_up, rw_g_up, rw_k_k, rw_k_a, rw_r_k, rw_gn_w,
           rw_gn_b, mb_conv_w, mb_conv_b, mb_dt_bias, mb_a_log, mb_d, mb_norm_w, od_w_in, od_w_out,
           ret_norm_w, gla_gate_up, gla_gate_b, gla_norm_w, final_g):
    W = dict(ev_w_in=ev_w_in, ev_w_out=ev_w_out, rw_mu=rw_mu, rw_w0=rw_w0, rw_w_up=rw_w_up,
             rw_a0=rw_a0, rw_a_up=rw_a_up, rw_g_up=rw_g_up, rw_k_k=rw_k_k, rw_k_a=rw_k_a,
             rw_r_k=rw_r_k, rw_gn_w=rw_gn_w, rw_gn_b=rw_gn_b, mb_conv_w=mb_conv_w,
             mb_conv_b=mb_conv_b, mb_dt_bias=mb_dt_bias, mb_a_log=mb_a_log, mb_d=mb_d,
             mb_norm_w=mb_norm_w, od_w_in=od_w_in, od_w_out=od_w_out, ret_norm_w=ret_norm_w,
             gla_gate_up=gla_gate_up, gla_gate_b=gla_gate_b, gla_norm_w=gla_norm_w)
    nb, seq, _ = x_prompt.shape
    db, dseq, _ = x_sample.shape
    P = [(_prep_even if l % 2 == 0 else _prep_odd)(l // 2, W) for l in range(DEPTH)]
    PW = _prep_proj(W)

    rows = db + nb
    rows_pad = -(-rows // SUBLANES) * SUBLANES
    c_all = jnp.concatenate([c_sample, c_prompt, jnp.zeros((rows_pad - rows, D_MODEL), F32)], axis=0)
    mod = _ada(c_all, ada_w, ada_b)

    states = (state_rwkv_shift, state_rwkv_wkv, state_mamba_conv, state_mamba_ssm, state_ret, state_gla)
    y_s, st_s, w16 = _trunk(x_sample.reshape(db * dseq, D_MODEL), mod, dict(row0=0, rps=dseq), states,
                            P, PW, (ffn_wg, ffn_wu, ffn_wd), final_g, nseq=db, length=dseq,
                            t=SAMPLE_CHUNK, g=SAMPLE_GROUP, lv=dseq, pos0=float(PAST_LEN))
    y_p, st_p, _ = _trunk(x_prompt.reshape(nb * seq, D_MODEL), mod, dict(row0=db, rps=seq), None,
                          P, PW, w16, final_g, nseq=nb, length=seq, t=PROMPT_CHUNK, g=PROMPT_GROUP, lv=PROMPT_CHUNK,
                          pos0=0.0)
    return (y_p.reshape(nb, seq, D_MODEL), y_s.reshape(db, dseq, D_MODEL)) + st_p + st_s
```

```python
import functools
import math

import numpy as np
import jax
import jax.numpy as jnp
from jax import lax
from jax.experimental import pallas as pl
from jax.experimental.pallas import tpu as pltpu

F32 = jnp.float32
BF16 = jnp.bfloat16

D_MODEL = 2048
DEPTH = 4
PAST_LEN = 16384
N_MOD = 9
D_FF = 5632

D_MIX = 1024
H_A, HEAD_A = 16, 64
RW_LORA_W, RW_LORA_A, RW_LORA_G = 64, 64, 160
RW_COLS = 3 * D_MIX + RW_LORA_W + RW_LORA_A + RW_LORA_G
RW_GN_EPS = 64e-5
H_B, HEAD_B, D_STATE, N_GROUPS, CONV_W = 16, 64, 128, 2, 4
MB_CONV_CH = D_MIX + 2 * N_GROUPS * D_STATE
MB_COLS = D_MIX + MB_CONV_CH + H_B
H_C, HEAD_C = 8, 128
ROPE_BASE = 10000.0
H_D, HEAD_DK, HEAD_DV, GLA_LORA = 4, 128, 256, 16
GLA_NORMALIZER = 16.0
OD_COLS = 4 * D_MIX + 2 * H_D * HEAD_DK + 2 * D_MIX + GLA_LORA

LANES = 128
SUBLANES = 8
VMEM_LIMIT = 60 * 1024 * 1024

RW_PAD = 3456
ZM = RW_PAD
MB_PAD = 2688
EV_PAD = RW_PAD + MB_PAD
OD_PAD = 7296
OD_GD = 7168

PROMPT_CHUNK = 64
PROMPT_GROUP = 2
GLA_SUB = 32
SAMPLE_CHUNK = 8
SAMPLE_GROUP = 8

ROW_TILE = 512
NORM_ROWS = 256
WEIGHT_RING = 3
FFN_ROW_TILE = 512


def _mm(a, b):
    return jnp.dot(a.astype(BF16), b.astype(BF16), preferred_element_type=F32)


def _mm_nt(a, b):
    return lax.dot_general(a.astype(BF16), b.astype(BF16), (((1,), (1,)), ((), ())),
                           preferred_element_type=F32)


def _mm_tn(a, b):
    return lax.dot_general(a.astype(BF16), b.astype(BF16), (((0,), (0,)), ((), ())),
                           preferred_element_type=F32)


def _split3(x):
    hi = x.astype(BF16)
    r = x - hi.astype(F32)
    mid = r.astype(BF16)
    lo = (r - mid.astype(F32)).astype(BF16)
    return hi, mid, lo


def _mm_sel(sel, x):
    s = sel.astype(BF16)
    hi, mid, lo = _split3(x)
    return (jnp.dot(s, hi, preferred_element_type=F32) + jnp.dot(s, mid, preferred_element_type=F32)
            + jnp.dot(s, lo, preferred_element_type=F32))


def _mm_tn_sel(x, sel):
    s = sel.astype(BF16)
    dn = (((0,), (0,)), ((), ()))
    hi, mid, lo = _split3(x)
    return (lax.dot_general(hi, s, dn, preferred_element_type=F32)
            + lax.dot_general(mid, s, dn, preferred_element_type=F32)
            + lax.dot_general(lo, s, dn, preferred_element_type=F32))


def _mm_xsel(x, sel):
    s = sel.astype(BF16)
    hi, mid, lo = _split3(x)
    return (jnp.dot(hi, s, preferred_element_type=F32) + jnp.dot(mid, s, preferred_element_type=F32)
            + jnp.dot(lo, s, preferred_element_type=F32))


def _sigmoid(x):
    return 1.0 / (1.0 + jnp.exp(-x))


def _silu(x):
    return x * _sigmoid(x)


def _softplus(x):
    return jnp.maximum(x, 0.0) + jnp.log(1.0 + jnp.exp(-jnp.abs(x)))


def _rms(x, eps):
    return x * lax.rsqrt(jnp.mean(x * x, axis=-1, keepdims=True) + eps)


def _mod_specs(mc, tm, idxs):
    l, rps, row0 = mc['l'], mc['rps'], mc['row0']
    specs = []
    for idx in idxs:
        if rps >= tm:
            specs.append(pl.BlockSpec(
                (None, None, SUBLANES, D_MODEL),
                lambda i, j, idx=idx: (l, idx, (row0 + i * tm // rps) // SUBLANES, 0)))
        else:
            assert row0 == 0
            specs.append(pl.BlockSpec((None, None, tm // rps, D_MODEL), lambda i, j, idx=idx: (l, idx, i, 0)))
    return specs


def _mod_val(ref, mc, tm):
    rps = mc['rps']
    if rps >= tm:
        r = (mc['row0'] + pl.program_id(0) * tm // rps) % SUBLANES
        return ref[pl.ds(r, 1), :]
    nseq = tm // rps
    row = lax.broadcasted_iota(jnp.int32, (tm, nseq), 0)
    first = lax.broadcasted_iota(jnp.int32, (tm, nseq), 1) * rps
    return _mm_sel((row >= first) & (row < first + rps), ref[...])


def _ada_kernel(c_ref, w_ref, b_ref, o_ref):
    x = _silu(c_ref[...]).astype(BF16)
    o_ref[...] = jnp.dot(x, w_ref[...].astype(BF16), preferred_element_type=F32) + b_ref[...]


def _ada(c, ada_w, ada_b):
    rows = c.shape[0]
    n = ada_w.shape[-1]
    tn = 1024
    per_mod = D_MODEL // tn
    return pl.pallas_call(
        _ada_kernel,
        grid=(DEPTH, n // tn),
        in_specs=[pl.BlockSpec((rows, D_MODEL), lambda l, j: (0, 0)),
                  pl.BlockSpec((None, D_MODEL, tn), lambda l, j: (l, 0, j)),
                  pl.BlockSpec((None, 1, tn), lambda l, j: (l, 0, j))],
        out_specs=pl.BlockSpec((None, None, rows, tn), lambda l, j: (l, j // per_mod, 0, j % per_mod)),
        out_shape=jax.ShapeDtypeStruct((DEPTH, N_MOD, rows, D_MODEL), F32),
        compiler_params=pltpu.CompilerParams(dimension_semantics=("arbitrary", "arbitrary"),
                                             vmem_limit_bytes=VMEM_LIMIT),
        name="ada",
    )(c, ada_w, ada_b.reshape(DEPTH, 1, n))


def _ffn_kernel(*refs, mc, tm, tf, ni, nj, final, emit):
    refs = list(refs)
    h_ref, sh_ref, sc_ref, g_ref, wg_ref, wu_ref, wd_ref = refs[:7]
    fin_ref = refs[7] if final else None
    o_ref = refs[7 + final]
    j = pl.program_id(1)

    if emit:
        xn_ref = refs[-1]
        wrefs = refs[8 + final:11 + final]
        for src, dst in zip((wg_ref, wu_ref, wd_ref), wrefs):
            dst[...] = src[...].astype(BF16)
        weights = lambda: tuple(w[...] for w in wrefs)
    else:
        xn_ref, gbuf, ubuf, dbuf, sem = refs[-5:]
        step = pl.program_id(0) * nj + j

        def copies(s):
            slot = s % WEIGHT_RING
            c0 = pl.multiple_of((s % nj) * tf, tf)
            return (pltpu.make_async_copy(wg_ref.at[:, pl.ds(c0, tf)], gbuf.at[slot], sem.at[0, slot]),
                    pltpu.make_async_copy(wu_ref.at[:, pl.ds(c0, tf)], ubuf.at[slot], sem.at[1, slot]),
                    pltpu.make_async_copy(wd_ref.at[pl.ds(c0, tf), :], dbuf.at[slot], sem.at[2, slot]))

        @pl.when(step == 0)
        def _():
            for s in range(WEIGHT_RING - 1):
                for c in copies(s):
                    c.start()

        @pl.when(step + (WEIGHT_RING - 1) < ni * nj)
        def _():
            for c in copies(step + (WEIGHT_RING - 1)):
                c.start()

        for c in copies(step):
            c.wait()
        slot = step % WEIGHT_RING
        weights = lambda: (gbuf[slot], ubuf[slot], dbuf[slot])

    def swiglu(xn):
        wg, wu, wd = weights()
        gate = jnp.dot(xn, wg, preferred_element_type=F32)
        up = jnp.dot(xn, wu, preferred_element_type=F32)
        act = (_silu(gate) * up).astype(BF16)
        return jnp.dot(act, wd, preferred_element_type=F32)

    @pl.when(j == 0)
    def _():
        scale = 1.0 + _mod_val(sc_ref, mc, tm)
        shift = _mod_val(sh_ref, mc, tm)
        per_row = scale.shape[0] > 1
        for r in range(tm // NORM_ROWS):
            rows = slice(r * NORM_ROWS, (r + 1) * NORM_ROWS)
            xn = (_rms(h_ref[rows, :], 1e-6) * (scale[rows] if per_row else scale)
                  + (shift[rows] if per_row else shift)).astype(BF16)
            xn_ref[rows, :] = xn
            o_ref[rows, :] = swiglu(xn)

    @pl.when((j > 0) & (j < nj - 1))
    def _():
        o_ref[...] += swiglu(xn_ref[...])

    @pl.when(j == nj - 1)
    def _():
        gate = 0.5 * (1.0 + _mod_val(g_ref, mc, tm))
        per_row = gate.shape[0] > 1
        for r in range(tm // NORM_ROWS):
            rows = slice(r * NORM_ROWS, (r + 1) * NORM_ROWS)
            acc = o_ref[rows, :] + swiglu(xn_ref[rows, :])
            out = h_ref[rows, :] + (gate[rows] if per_row else gate) * acc
            if final:
                out = _rms(out, 1e-6) * fin_ref[...]
            o_ref[rows, :] = out


def _ffn(h, mod, mc, mod_base, w, k, final_g=None):
    m = h.shape[0]
    l = mc['l']
    emit = w[0].ndim == 4
    tm = min(FFN_ROW_TILE, m)
    tf = 256 if emit else 512
    nj = D_FF // tf
    final = final_g is not None
    in_specs = [pl.BlockSpec((tm, D_MODEL), lambda i, j: (i, 0))]
    in_specs += _mod_specs(mc, tm, (mod_base, mod_base + 1, mod_base + 2))
    col = pl.BlockSpec((D_MODEL, tf), lambda i, j: (0, j))
    row = pl.BlockSpec((tf, D_MODEL), lambda i, j: (j, 0))
    if emit:
        assert m == tm
        in_specs += [pl.BlockSpec((None, None, D_MODEL, tf), lambda i, j: (l, k, 0, j)),
                     pl.BlockSpec((None, None, D_MODEL, tf), lambda i, j: (l, k, 0, j)),
                     pl.BlockSpec((None, None, tf, D_MODEL), lambda i, j: (l, k, j, 0))]
    else:
        in_specs += [pl.BlockSpec(memory_space=pl.ANY)] * 3
    args = [h, mod, mod, mod, *w]
    if final:
        in_specs.append(pl.BlockSpec((1, D_MODEL), lambda i, j: (0, 0)))
        args.append(final_g.reshape(1, D_MODEL))
    out_specs = [pl.BlockSpec((tm, D_MODEL), lambda i, j: (i, 0))]
    out_shape = [jax.ShapeDtypeStruct((m, D_MODEL), F32)]
    if emit:
        out_specs += [col, col, row]
        out_shape += [jax.ShapeDtypeStruct((D_MODEL, D_FF), BF16), jax.ShapeDtypeStruct((D_MODEL, D_FF), BF16),
                      jax.ShapeDtypeStruct((D_FF, D_MODEL), BF16)]
    scratch = [pltpu.VMEM((tm, D_MODEL), BF16)]
    if not emit:
        scratch += [pltpu.VMEM((WEIGHT_RING, D_MODEL, tf), BF16), pltpu.VMEM((WEIGHT_RING, D_MODEL, tf), BF16),
                    pltpu.VMEM((WEIGHT_RING, tf, D_MODEL), BF16), pltpu.SemaphoreType.DMA((3, WEIGHT_RING))]
    out = pl.pallas_call(
        functools.partial(_ffn_kernel, mc=mc, tm=tm, tf=tf, ni=m // tm, nj=nj, final=final, emit=emit),
        grid=(m // tm, nj),
        in_specs=in_specs, out_specs=out_specs, out_shape=out_shape,
        scratch_shapes=scratch,
        compiler_params=pltpu.CompilerParams(dimension_semantics=("arbitrary", "arbitrary"),
                                             vmem_limit_bytes=VMEM_LIMIT),
        name="ffn",
    )(*args)
    return (out[0], tuple(out[1:])) if emit else out[0]


def _inproj_kernel(h_ref, sh_ref, sc_ref, w_ref, o_ref, xn_ref, *, mc, tm):
    j = pl.program_id(1)

    @pl.when(j == 0)
    def _():
        scale = 1.0 + _mod_val(sc_ref, mc, tm)
        shift = _mod_val(sh_ref, mc, tm)
        per_row = scale.shape[0] > 1
        for r in range(tm // NORM_ROWS):
            rows = slice(r * NORM_ROWS, (r + 1) * NORM_ROWS)
            xn = (_rms(h_ref[rows, :], 1e-6) * (scale[rows] if per_row else scale)
                  + (shift[rows] if per_row else shift)).astype(BF16)
            xn_ref[rows, :] = xn
            o_ref[rows, :] = jnp.dot(xn, w_ref[...], preferred_element_type=F32)

    @pl.when(j > 0)
    def _():
        o_ref[...] = jnp.dot(xn_ref[...], w_ref[...], preferred_element_type=F32)


def _inproj(h, mod, mc, w, li):
    m = h.shape[0]
    n = w.shape[2]
    tm = min(ROW_TILE, m)
    tn = n // 3
    if n % (6 * LANES) == 0 and m % (2 * ROW_TILE) == 0:
        tm, tn = 2 * ROW_TILE, n // 6
    in_specs = [pl.BlockSpec((tm, D_MODEL), lambda i, j: (i, 0))]
    in_specs += _mod_specs(mc, tm, (3, 4))
    in_specs += [pl.BlockSpec((None, D_MODEL, tn), lambda i, j: (li, 0, j))]
    return pl.pallas_call(
        functools.partial(_inproj_kernel, mc=mc, tm=tm),
        grid=(m // tm, n // tn),
        in_specs=in_specs,
        out_specs=pl.BlockSpec((tm, tn), lambda i, j: (i, j)),
        out_shape=jax.ShapeDtypeStruct((m, n), F32),
        scratch_shapes=[pltpu.VMEM((tm, D_MODEL), BF16)],
        compiler_params=pltpu.CompilerParams(dimension_semantics=("arbitrary", "arbitrary"),
                                             vmem_limit_bytes=VMEM_LIMIT),
        name="inproj",
    )(h, mod, mod, w)


def _outproj_kernel(y_ref, w_ref, h_ref, g_ref, o_ref, *, mc, tm):
    mix = jnp.dot(y_ref[...], w_ref[...], preferred_element_type=F32)
    o_ref[...] = h_ref[...] + (1.0 + _mod_val(g_ref, mc, tm)) * mix


def _outproj(y, w, li, h, mod, mc):
    m = h.shape[0]
    tm = min(ROW_TILE, m)
    in_specs = [pl.BlockSpec((tm, D_MODEL), lambda i, j: (i, 0)),
                pl.BlockSpec((None, D_MODEL, D_MODEL), lambda i, j: (li, 0, 0)),
                pl.BlockSpec((tm, D_MODEL), lambda i, j: (i, 0))]
    in_specs += _mod_specs(mc, tm, (5,))
    return pl.pallas_call(
        functools.partial(_outproj_kernel, mc=mc, tm=tm),
        grid=(m // tm, 1),
        in_specs=in_specs,
        out_specs=pl.BlockSpec((tm, D_MODEL), lambda i, j: (i, 0)),
        out_shape=jax.ShapeDtypeStruct((m, D_MODEL), F32),
        compiler_params=pltpu.CompilerParams(dimension_semantics=("arbitrary", "arbitrary"),
                                             vmem_limit_bytes=VMEM_LIMIT),
        name="outproj",
    )(y, w, h, mod)


def _state_spec(li, nl, g, tail, whole):
    zeros = (0,) * len(tail)
    if whole:
        return pl.BlockSpec((nl, g) + tail, lambda b, c: (0, b) + zeros)
    return pl.BlockSpec((None, g) + tail, lambda b, c: (li, b) + zeros)


def _state_out(stacked, li, nseq, g, tail, whole):
    if stacked is None:
        zeros = (0,) * len(tail)
        return (jax.ShapeDtypeStruct((nseq,) + tail, F32),
                pl.BlockSpec((g,) + tail, lambda b, c: (b,) + zeros))
    return jax.ShapeDtypeStruct(stacked.shape, F32), _state_spec(li, stacked.shape[0], g, tail, whole)


def _layer_views(li, nl, pairs):
    for src, dst in pairs:
        for lj in range(nl):
            if lj != li:
                dst[lj] = src[lj]
    return [r.at[li] for pair in pairs for r in pair]


def _seq_masks(g, t):
    n = g * t
    r = lax.broadcasted_iota(jnp.int32, (n, n), 0)
    c = lax.broadcasted_iota(jnp.int32, (n, n), 1)
    if g == 1:
        return r > c, r >= c
    same = (r // t) == (c // t)
    return same & (r > c), same & (r >= c)


def _seq_selectors(g, t):
    r = np.arange(g * t)[:, None]
    c = np.arange(g * t)[None, :]
    same = (r // t) == (c // t)
    incl = same & (r >= c)
    last = c == (r // t) * t + (t - 1)
    return jnp.asarray(np.stack([incl, last, incl.T]), BF16)


EV_W0, EV_A0, EV_KK, EV_KA, EV_RK, EV_GNW, EV_GNB, EV_D, EV_NW = range(9)
EV_VEC_ROWS = 16


def _even_kernel(*refs, t, g, nchunks, lv, has_init, ncarry, whole):
    if has_init:
        z_ref, shift0_ref, wkv0_ref, conv0_ref, ssm0_ref, *rest = refs
    else:
        z_ref, *rest = refs
    mu_ref, vec_ref, wa_ref, gup_ref, cw_ref, cb_ref, dtb_ref, alog_ref, exp_ref, sel_ref, *rest = rest
    (y_ref, shift_o, wkv_o, conv_o, ssm_o,
     carry_ref, wkv_ref, ext_ref, ssm_ref,
     r_s, kp_s, v_s, kk_s, a_s, c_s, ld_s, g_s, cl_s, y_s) = rest[ncarry:]
    n = g * t
    ci = pl.program_id(1)
    masked = lv < t
    nlog = int(math.log2(t))
    seqs = range(g)
    hist = SUBLANES - (CONV_W - 1)
    direct = has_init and nchunks == 1
    if whole:
        assert direct
        wkv0_ref, wkv_o, ssm0_ref, ssm_o = _layer_views(*whole, [(wkv0_ref, wkv_o), (ssm0_ref, ssm_o)])
    wkv_src, wkv_dst = (wkv0_ref, wkv_o) if direct else (wkv_ref, wkv_ref)
    ssm_src, ssm_dst = (ssm0_ref, ssm_o) if direct else (ssm_ref, ssm_ref)

    @pl.when(ci == 0)
    def _():
        if has_init:
            carry_ref[:, :, 0:RW_COLS] = shift0_ref[...]
            carry_ref[:, :, RW_COLS:RW_PAD] = jnp.zeros((g, 1, RW_PAD - RW_COLS), F32)
            ext_ref[:, 0:hist, :] = jnp.zeros((g, hist, MB_CONV_CH), F32)
            ext_ref[:, hist:SUBLANES, :] = conv0_ref[...]
            if not direct:
                wkv_ref[...] = wkv0_ref[...]
                ssm_ref[...] = ssm0_ref[...]
        else:
            carry_ref[...] = jnp.zeros_like(carry_ref)
            wkv_ref[...] = jnp.zeros_like(wkv_ref)
            ext_ref[:, 0:SUBLANES, :] = jnp.zeros((g, SUBLANES, MB_CONV_CH), F32)
            ssm_ref[...] = jnp.zeros_like(ssm_ref)

    tok = lax.broadcasted_iota(jnp.int32, (n, 1), 0) % t
    valid = tok < lv
    strict, incl = _seq_masks(g, t)
    sel_incl, sel_last, sel_upper = sel_ref[0], sel_ref[1], sel_ref[2]

    def vec(i, lo=0, hi=D_MIX):
        return vec_ref[i:i + 1, lo:hi]

    def rows_of(x, s):
        return x[s * t:(s + 1) * t]

    def per_seq(fn):
        parts = [fn(s) for s in seqs]
        return parts[0] if g == 1 else jnp.concatenate(parts, axis=0)

    def shifted(a, b):
        cur = z_ref[:, :, a:b].reshape(n, b - a)
        carry = per_seq(lambda s: jnp.broadcast_to(carry_ref[s, :, a:b], (t, b - a)))
        prev = jnp.where(tok == 0, carry, pltpu.roll(cur, 1, axis=0))
        return cur + mu_ref[:, a:b] * (prev - cur)

    r_s[...] = shifted(0, D_MIX)
    k = shifted(D_MIX, 2 * D_MIX)
    v_s[...] = shifted(2 * D_MIX, 3 * D_MIX)
    lo = shifted(3 * D_MIX, 3 * D_MIX + LANES)
    lane = lax.broadcasted_iota(jnp.int32, (n, LANES), 1)
    wa = _mm(jnp.where(lane < RW_LORA_W, jnp.tanh(lo), lo), wa_ref[...])
    w_log = -_softplus(-(vec(EV_W0) + wa[:, 0:D_MIX])) - 0.5
    ld = -jnp.exp(w_log)
    a = _sigmoid(vec(EV_A0) + wa[:, D_MIX:2 * D_MIX])
    g_s[...] = _mm(_sigmoid(shifted(3 * D_MIX + LANES, RW_PAD)), gup_ref[...])
    kp = k * (1.0 + (a - 1.0) * vec(EV_KA))
    if masked:
        ld = jnp.where(valid, ld, 0.0)
        kp = jnp.where(valid, kp, 0.0)
        a = jnp.where(valid, a, 0.0)
    kk_s[...] = k * vec(EV_KK)
    kp_s[...] = kp
    a_s[...] = a
    ld_s[...] = ld
    c = _mm_sel(sel_incl, ld)
    c_s[...] = c
    cl_s[...] = _mm_sel(sel_last, c)
    carry_ref[...] = z_ref[:, lv - 1:lv, 0:RW_PAD]

    heads = range(H_A)
    sls = [slice(h * HEAD_A, (h + 1) * HEAD_A) for h in heads]
    rs = [r_s[:, sl] for sl in sls]
    kps = [kp_s[:, sl] for sl in sls]
    vs = [v_s[:, sl] for sl in sls]
    cs = [c_s[:, sl] for sl in sls]
    cls = [cl_s[:, sl] for sl in sls]
    kks, bs = [], []
    for sl in sls:
        kkr = kk_s[:, sl]
        nrm = jnp.sqrt(jnp.sum(kkr * kkr, axis=-1, keepdims=True))
        kks.append(kkr / jnp.maximum(nrm, 1e-12))
        bs.append(kks[-1] * a_s[:, sl])
    ecis = [jnp.exp(-c) for c in cs]
    kts = [kk * jnp.exp(c - ld_s[:, sl]) for kk, c, sl in zip(kks, cs, sls)]
    rts = [r * jnp.exp(c) for r, c in zip(rs, cs)]
    lhss = [jnp.concatenate([kt, rt], axis=0) for kt, rt in zip(kts, rts)]
    a1s = [_mm_nt(lhs, kp * eci) for lhs, kp, eci in zip(lhss, kps, ecis)]
    a2s = [_mm_nt(lhs, b * eci) for lhs, b, eci in zip(lhss, bs, ecis)]
    pms = [[_mm_nt(jnp.concatenate([rows_of(kt, s), rows_of(rt, s)], axis=0), wkv_src[s, h]) for s in seqs]
           for h, kt, rt in zip(heads, kts, rts)]
    pmk = [per_seq(lambda s: pm[s][0:t]) for pm in pms]
    pmr = [per_seq(lambda s: pm[s][t:2 * t]) for pm in pms]
    us = [pk + _mm(jnp.where(strict, a1[0:n], 0.0), v) for pk, a1, v in zip(pmk, a1s, vs)]
    ps = [jnp.where(strict, -a2[0:n], 0.0) for a2 in a2s]
    us = [u + _mm(p, u) for u, p in zip(us, ps)]
    for _ in range(nlog - 1):
        ps = [_mm(p, p) for p in ps]
        us = [u + _mm(p, u) for u, p in zip(us, ps)]
    ys = [pr + _mm(jnp.where(incl, a1[n:2 * n], 0.0), v) - _mm(jnp.where(incl, a2[n:2 * n], 0.0), u)
          for pr, a1, a2, v, u in zip(pmr, a1s, a2s, vs, us)]
    for h in heads:
        dec_end = jnp.exp(cls[h] - cs[h])
        kbar = kps[h] * dec_end
        bbar = bs[h] * dec_end
        for s in seqs:
            vu = jnp.concatenate([rows_of(vs[h], s), rows_of(us[h], s)], axis=0)
            kb = jnp.concatenate([rows_of(kbar, s), -rows_of(bbar, s)], axis=0)
            wkv_dst[s, h] = wkv_src[s, h] * jnp.exp(cls[h][s * t:s * t + 1]) + _mm_tn(vu, kb)
    for h, sl in zip(heads, sls):
        y, r, kp, v = ys[h], rs[h], kps[h], vs[h]
        mean = jnp.mean(y, axis=-1, keepdims=True)
        yc = y - mean
        var = jnp.mean(yc * yc, axis=-1, keepdims=True)
        yn = (yc * lax.rsqrt(var + RW_GN_EPS) * vec(EV_GNW, sl.start, sl.stop)
              + vec(EV_GNB, sl.start, sl.stop))
        bonus = jnp.sum(r * kp * vec(EV_RK, sl.start, sl.stop), axis=-1, keepdims=True) * v
        y_s[:, sl] = (yn + bonus) * g_s[:, sl]

    zg = z_ref[:, :, ZM:ZM + D_MIX].reshape(n, D_MIX)
    ext_ref[:, SUBLANES:SUBLANES + t, :] = z_ref[:, :, ZM + D_MIX:ZM + D_MIX + MB_CONV_CH]
    conv = cb_ref[...]
    for w in range(CONV_W):
        conv = conv + cw_ref[w:w + 1, :] * ext_ref[:, hist + w:hist + w + t, :].reshape(n, MB_CONV_CH)
    new_conv = ext_ref[:, SUBLANES + lv - (CONV_W - 1):SUBLANES + lv, :]
    ext_ref[:, hist:SUBLANES, :] = new_conv
    xbc = _silu(conv)
    xs = xbc[:, 0:D_MIX]
    dt = _softplus(z_ref[:, :, ZM + D_MIX + MB_CONV_CH:ZM + MB_PAD].reshape(n, LANES) + dtb_ref[...])
    logd = dt * (-jnp.exp(alog_ref[...]))
    if masked:
        logd = jnp.where(valid, logd, 0.0)
    c = _mm_sel(sel_incl, logd)
    c_t = _mm_tn_sel(logd, sel_upper)
    c_end = _mm_sel(sel_last, c)
    wide = _mm_xsel(jnp.concatenate([dt, c, c_end - c], axis=0), exp_ref[...])
    xdt = xs * wide[0:n]
    if masked:
        xdt = jnp.where(valid, xdt, 0.0)
    ec_w = jnp.exp(wide[n:2 * n])
    xdec = xdt * jnp.exp(wide[2 * n:3 * n])
    hpg = H_B // N_GROUPS
    bgs = [xbc[:, D_MIX + q * D_STATE:D_MIX + (q + 1) * D_STATE] for q in range(N_GROUPS)]
    cgs = [xbc[:, D_MIX + (N_GROUPS + q) * D_STATE:D_MIX + (N_GROUPS + q + 1) * D_STATE]
           for q in range(N_GROUPS)]
    scores = [_mm_nt(cg, bg) for cg, bg in zip(cgs, bgs)]
    segs = [jnp.where(incl, jnp.exp(jnp.where(incl, c[:, h:h + 1] - c_t[h:h + 1, :], 0.0)), 0.0)
            for h in range(H_B)]
    hsl = [slice(h * HEAD_B, (h + 1) * HEAD_B) for h in range(H_B)]
    intra = [_mm(scores[h // hpg] * segs[h], xdt[:, hsl[h]]) for h in range(H_B)]
    grp = lambda s, q: ssm_src[s, q * hpg:(q + 1) * hpg].reshape(hpg * HEAD_B, D_STATE)
    inter = [per_seq(lambda s: _mm_nt(rows_of(cgs[q], s), grp(s, q))) for q in range(N_GROUPS)]
    for q in range(N_GROUPS):
        for s in seqs:
            upd = _mm_tn(rows_of(xdec[:, q * hpg * HEAD_B:(q + 1) * hpg * HEAD_B], s), rows_of(bgs[q], s))
            last = s * t + t - 1
            for h in range(q * hpg, (q + 1) * hpg):
                ssm_dst[s, h] = (ssm_src[s, h] * jnp.exp(c[last:last + 1, h:h + 1])
                                 + upd[(h - q * hpg) * HEAD_B:(h - q * hpg + 1) * HEAD_B])
    o = jnp.concatenate(intra, axis=1) + jnp.concatenate(inter, axis=1) * ec_w
    yb = (o + vec(EV_D) * xs) * _silu(zg)
    y_s[:, D_MIX:2 * D_MIX] = _rms(yb, 1e-5) * vec(EV_NW)
    y_ref[...] = y_s[...].astype(BF16).reshape(g, t, D_MODEL)

    @pl.when(ci == nchunks - 1)
    def _():
        shift_o[...] = carry_ref[...]
        conv_o[...] = new_conv
        if not direct:
            wkv_o[...] = wkv_ref[...]
            ssm_o[...] = ssm_ref[...]


def _carry_specs(carriers, first_out):
    return ([pl.BlockSpec(memory_space=pl.ANY)] * len(carriers), list(carriers),
            lambda nargs: {nargs + i: first_out[i] for i in range(len(carriers))})


def _even_mixer(z, states, carriers, li, p, *, t, g, lv):
    nseq, length, _ = z.shape
    nchunks = length // t
    has_init = states is not None
    assert (lv == t or nchunks == 1) and nseq % g == 0
    full = lambda *shape: pl.BlockSpec(shape, lambda b, c: (0,) * len(shape))
    in_specs = [pl.BlockSpec((g, t, EV_PAD), lambda b, c: (b, c, 0))]
    args = [z]
    ssm_t = None
    whole = has_init and not carriers
    nl = states[1].shape[0] if has_init else 0
    if has_init:
        shift0, wkv0, conv0, ssm0 = states
        ssm_t = jnp.swapaxes(ssm0, 3, 4)
        in_specs += [pl.BlockSpec((None, g, 1, RW_COLS), lambda b, c: (li, b, 0, 0)),
                     _state_spec(li, nl, g, (H_A, HEAD_A, HEAD_A), whole),
                     pl.BlockSpec((None, g, CONV_W - 1, MB_CONV_CH), lambda b, c: (li, b, 0, 0)),
                     _state_spec(li, nl, g, (H_B, HEAD_B, D_STATE), whole)]
        args += [shift0.reshape(-1, nseq, 1, RW_COLS), wkv0, conv0, ssm_t]
    in_specs += [full(1, RW_PAD), full(EV_VEC_ROWS, D_MIX), full(LANES, 2 * D_MIX),
                 full(RW_PAD - 3 * D_MIX - LANES, D_MIX), full(CONV_W, MB_CONV_CH),
                 full(1, MB_CONV_CH), full(1, LANES), full(1, LANES), full(LANES, D_MIX),
                 full(3, g * t, g * t)]
    args += [p['mu'], p['vec'], p['wa'], p['gup'], p['conv_w'], p['conv_b'], p['dt_bias'],
             p['a_log'], p['expand'], _seq_selectors(g, t)]
    carry_specs, carry_args, carry_alias = _carry_specs(carriers, (2, 4))
    aliases = carry_alias(len(args))
    in_specs += carry_specs
    args += carry_args
    wkv_shape, wkv_spec = _state_out(states and states[1], li, nseq, g, (H_A, HEAD_A, HEAD_A), whole)
    ssm_shape, ssm_spec = _state_out(ssm_t, li, nseq, g, (H_B, HEAD_B, D_STATE), whole)
    out_shape = (jax.ShapeDtypeStruct((nseq, length, D_MODEL), BF16),
                 jax.ShapeDtypeStruct((nseq, 1, RW_PAD), F32), wkv_shape,
                 jax.ShapeDtypeStruct((nseq, CONV_W - 1, MB_CONV_CH), F32), ssm_shape)
    out_specs = (pl.BlockSpec((g, t, D_MODEL), lambda b, c: (b, c, 0)),
                 pl.BlockSpec((g, 1, RW_PAD), lambda b, c: (b, 0, 0)), wkv_spec,
                 pl.BlockSpec((g, CONV_W - 1, MB_CONV_CH), lambda b, c: (b, 0, 0)), ssm_spec)
    n = g * t
    wide = lambda: pltpu.VMEM((n, D_MIX), F32)
    gs = 1 if (has_init and nchunks == 1) else g
    scratch = [pltpu.VMEM((g, 1, RW_PAD), F32), pltpu.VMEM((gs, H_A, HEAD_A, HEAD_A), F32),
               pltpu.VMEM((g, t + SUBLANES, MB_CONV_CH), F32), pltpu.VMEM((gs, H_B, HEAD_B, D_STATE), F32)]
    scratch += [wide() for _ in range(9)] + [pltpu.VMEM((n, D_MODEL), F32)]
    y, shift, wkv, conv, ssm = pl.pallas_call(
        functools.partial(_even_kernel, t=t, g=g, nchunks=nchunks, lv=lv, has_init=has_init,
                          ncarry=len(carriers), whole=(li, nl) if whole else None),
        grid=(nseq // g, nchunks),
        in_specs=in_specs, out_specs=out_specs, out_shape=out_shape, scratch_shapes=scratch,
        input_output_aliases=aliases,
        compiler_params=pltpu.CompilerParams(dimension_semantics=("arbitrary", "arbitrary"),
                                             vmem_limit_bytes=VMEM_LIMIT),
        name="even_mixer",
    )(*args)
    return y, (shift[:, 0, :RW_COLS], wkv, conv, ssm)


def _odd_kernel(*refs, t, tsub, g, nchunks, lv, has_init, ncarry, whole):
    if has_init:
        z_ref, ret0_ref, gla0_ref, *rest = refs
    else:
        z_ref, *rest = refs
    cos_ref, sin_ref, seg_ref, pw_ref, dte_ref, rnw_ref, gup_ref, gb_ref, gnw_ref, sel_ref, *rest = rest
    y_ref, ret_o, gla_o, ret_ref, gla_ref, la_s, y_s = rest[ncarry:]
    n = g * t
    ci = pl.program_id(1)
    masked = lv < t
    seqs = range(g)
    direct = has_init and nchunks == 1
    if whole:
        assert direct
        ret0_ref, ret_o, gla0_ref, gla_o = _layer_views(*whole, [(ret0_ref, ret_o), (gla0_ref, gla_o)])
    ret_src, ret_dst = (ret0_ref, ret_o) if direct else (ret_ref, ret_ref)

    @pl.when(ci == 0)
    def _():
        if has_init:
            if not direct:
                ret_ref[...] = ret0_ref[...]
            for s in seqs:
                for h in range(H_D):
                    gla_ref[s, h] = jnp.transpose(gla0_ref[s, h])
        else:
            ret_ref[...] = jnp.zeros_like(ret_ref)
            gla_ref[...] = jnp.zeros_like(gla_ref)

    valid = lax.broadcasted_iota(jnp.int32, (n, 1), 0) % t < lv

    def zcols(a, b):
        return z_ref[:, :, a:b].reshape(n, b - a)

    def per_seq(fn, rows):
        parts = [fn(s, slice(s * rows, (s + 1) * rows)) for s in seqs]
        return parts[0] if g == 1 else jnp.concatenate(parts, axis=0)

    cosf = cos_ref[...]
    sinf = sin_ref[...]

    def rotary(x):
        return x * cosf + pltpu.roll(x, HEAD_C // 2, axis=1) * sinf

    hc = range(H_C)
    qs = [rotary(zcols(h * HEAD_C, (h + 1) * HEAD_C)) * HEAD_C ** -0.5 for h in hc]
    ks = [rotary(zcols(D_MIX + h * HEAD_C, D_MIX + (h + 1) * HEAD_C)) for h in hc]
    if masked:
        ks = [jnp.where(valid, k, 0.0) for k in ks]
    vs = [zcols(2 * D_MIX + h * HEAD_C, 2 * D_MIX + (h + 1) * HEAD_C) for h in hc]
    sc = [_mm_nt(q, k) * seg_ref[h] for h, q, k in zip(hc, qs, ks)]
    inter = [per_seq(lambda s, rw: _mm(qs[h][rw], ret_src[s, h]), t) * pw_ref[h] for h in hc]
    outs = [_mm(s_, v) + i for s_, v, i in zip(sc, vs, inter)]
    for h in hc:
        gamma = 1.0 - 2.0 ** (-5.0 - h)
        kd = ks[h] * dte_ref[h]
        for s in seqs:
            rw = slice(s * t, (s + 1) * t)
            ret_dst[s, h] = ret_src[s, h] * gamma ** lv + _mm_tn(kd[rw], vs[h][rw])
    for h in hc:
        sl = slice(h * HEAD_C, (h + 1) * HEAD_C)
        rg = zcols(3 * D_MIX + h * HEAD_C, 3 * D_MIX + (h + 1) * HEAD_C)
        y_s[:, sl] = _rms(outs[h], 1e-6) * rnw_ref[:, sl] * _silu(rg)

    gq0 = 4 * D_MIX
    gk0 = gq0 + H_D * HEAD_DK
    gv0 = gk0 + H_D * HEAD_DK
    gr0 = gv0 + D_MIX
    x = _mm(zcols(OD_GD, OD_GD + LANES), gup_ref[...]) + gb_ref[...]
    la = -_softplus(-x) * (1.0 / GLA_NORMALIZER)
    if masked:
        la = jnp.where(valid, la, 0.0)
    la_s[...] = la
    assert g == 1 or tsub == t
    incl = _seq_masks(g, tsub)[1]
    hd = range(H_D)
    for u in range(t // tsub):
        rs = slice(0, n) if g > 1 else slice(u * tsub, (u + 1) * tsub)

        def cols(a, b, rs=rs):
            return z_ref[0, rs, a:b] if g == 1 else zcols(a, b)

        cum_all = _mm_sel(sel_ref[0], la_s[rs, :])
        cums = [cum_all[:, h * HEAD_DK:(h + 1) * HEAD_DK] for h in hd]
        qes = [cols(gq0 + h * HEAD_DK, gq0 + (h + 1) * HEAD_DK) * HEAD_DK ** -0.5 * jnp.exp(cums[h])
               for h in hd]
        ks = [cols(gk0 + h * HEAD_DK, gk0 + (h + 1) * HEAD_DK) for h in hd]
        if masked:
            ks = [jnp.where(valid[rs], k, 0.0) for k in ks]
        vs = [cols(gv0 + h * HEAD_DV, gv0 + (h + 1) * HEAD_DV) for h in hd]
        scores = [jnp.where(incl, _mm_nt(qes[h], ks[h] * jnp.exp(-cums[h])), 0.0) for h in hd]
        inter = [per_seq(lambda s, rw: _mm_nt(qes[h][rw], gla_ref[s, h]), tsub) for h in hd]
        outs = [_mm(scores[h], vs[h]) + inter[h] for h in hd]
        for h in hd:
            for s in seqs:
                rw = slice(s * tsub, (s + 1) * tsub)
                cum = cums[h][rw]
                last = cum[tsub - 1:tsub]
                gla_ref[s, h] = (gla_ref[s, h] * jnp.exp(last)
                                 + _mm_tn(vs[h][rw], ks[h][rw] * jnp.exp(last - cum)))
        for h in hd:
            gr = cols(gr0 + h * HEAD_DV, gr0 + (h + 1) * HEAD_DV)
            ys = slice(D_MIX + h * HEAD_DV, D_MIX + (h + 1) * HEAD_DV)
            y_s[rs, ys] = _rms(outs[h], 1e-6) * gnw_ref[:, h * HEAD_DV:(h + 1) * HEAD_DV] * _silu(gr)
    y_ref[...] = y_s[...].reshape(g, t, D_MODEL).astype(BF16)

    @pl.when(ci == nchunks - 1)
    def _():
        if not direct:
            ret_o[...] = ret_ref[...]
        for s in seqs:
            for h in range(H_D):
                gla_o[s, h] = jnp.transpose(gla_ref[s, h])


def _ret_tables(t, g, lv, pos0, nchunks):
    half = HEAD_C // 2
    inv = ROPE_BASE ** (-jnp.arange(half, dtype=F32) / half)
    pos = pos0 + jnp.arange(t * nchunks, dtype=F32).reshape(nchunks, 1, t)
    pos = jnp.broadcast_to(pos, (nchunks, g, t)).reshape(-1)
    ang = pos[:, None] * inv[None, :]
    cos, sin = jnp.cos(ang), jnp.sin(ang)
    cosf = jnp.concatenate([cos, cos], axis=-1)
    sinf = jnp.concatenate([-sin, sin], axis=-1)
    gam = 1.0 - np.exp2(-5.0 - np.arange(H_C, dtype=np.float64))
    ti = np.arange(t)
    d = ti[:, None] - ti[None, :]
    seg = np.where(d >= 0, gam[:, None, None] ** np.maximum(d, 0)[None], 0.0)
    seg = np.stack([np.kron(np.eye(g), s) for s in seg])
    pw = np.tile(gam[:, None] ** (ti + 1)[None], (1, g))
    dte = np.tile(np.where(ti < lv, gam[:, None] ** np.maximum(lv - 1 - ti, 0)[None], 0.0), (1, g))
    lanes = lambda a: jnp.asarray(np.broadcast_to(a[:, :, None], a.shape + (LANES,)), F32)
    return cosf, sinf, jnp.asarray(seg, F32), lanes(pw), lanes(dte)


def _odd_mixer(z, states, carriers, li, p, *, t, g, lv, pos0):
    nseq, length, _ = z.shape
    nchunks = length // t
    has_init = states is not None
    assert (lv == t or nchunks == 1) and nseq % g == 0
    tsub = min(GLA_SUB, t)
    n = g * t
    nsub = n if g > 1 else tsub
    cosf, sinf, seg, pw, dte = _ret_tables(t, g, lv, pos0, nchunks)
    full = lambda *shape: pl.BlockSpec(shape, lambda b, c: (0,) * len(shape))
    in_specs = [pl.BlockSpec((g, t, OD_PAD), lambda b, c: (b, c, 0))]
    args = [z]
    whole = has_init and not carriers
    nl = states[0].shape[0] if has_init else 0
    if has_init:
        in_specs += [_state_spec(li, nl, g, (H_C, HEAD_C, HEAD_C), whole),
                     _state_spec(li, nl, g, (H_D, HEAD_DK, HEAD_DV), whole)]
        args += list(states)
    in_specs += [pl.BlockSpec((n, HEAD_C), lambda b, c: (c, 0)),
                 pl.BlockSpec((n, HEAD_C), lambda b, c: (c, 0)),
                 full(H_C, n, n), full(H_C, n, LANES), full(H_C, n, LANES),
                 full(1, D_MIX), full(LANES, H_D * HEAD_DK), full(1, H_D * HEAD_DK), full(1, D_MIX),
                 full(3, nsub, nsub)]
    args += [cosf, sinf, seg, pw, dte, p['ret_norm_w'], p['gate_up'], p['gate_b'], p['gla_norm_w'],
             _seq_selectors(g if g > 1 else 1, tsub)]
    carry_specs, carry_args, carry_alias = _carry_specs(carriers, (1, 2))
    aliases = carry_alias(len(args))
    in_specs += carry_specs
    args += carry_args
    ret_shape, ret_spec = _state_out(states and states[0], li, nseq, g, (H_C, HEAD_C, HEAD_C), whole)
    gla_shape, gla_spec = _state_out(states and states[1], li, nseq, g, (H_D, HEAD_DK, HEAD_DV), whole)
    out_shape = (jax.ShapeDtypeStruct((nseq, length, D_MODEL), BF16), ret_shape, gla_shape)
    out_specs = (pl.BlockSpec((g, t, D_MODEL), lambda b, c: (b, c, 0)), ret_spec, gla_spec)
    gs = 1 if (has_init and nchunks == 1) else g
    scratch = [pltpu.VMEM((gs, H_C, HEAD_C, HEAD_C), F32), pltpu.VMEM((g, H_D, HEAD_DV, HEAD_DK), F32),
               pltpu.VMEM((n, H_D * HEAD_DK), F32), pltpu.VMEM((n, D_MODEL), F32)]
    y, ret, gla = pl.pallas_call(
        functools.partial(_odd_kernel, t=t, tsub=tsub, g=g, nchunks=nchunks, lv=lv, has_init=has_init,
                          ncarry=len(carriers), whole=(li, nl) if whole else None),
        grid=(nseq // g, nchunks),
        in_specs=in_specs, out_specs=out_specs, out_shape=out_shape, scratch_shapes=scratch,
        input_output_aliases=aliases,
        compiler_params=pltpu.CompilerParams(dimension_semantics=("arbitrary", "arbitrary"),
                                             vmem_limit_bytes=VMEM_LIMIT),
        name="odd_mixer",
    )(*args)
    return y, (ret, gla)


def _prep_proj(W):
    w = W['ev_w_in']
    n = w.shape[0]
    ev_in = jnp.concatenate([w[:, :, :RW_COLS], jnp.zeros((n, D_MODEL, RW_PAD - RW_COLS), F32),
                             w[:, :, RW_COLS:], jnp.zeros((n, D_MODEL, MB_PAD - MB_COLS), F32)], axis=2)
    w = W['od_w_in']
    n = w.shape[0]
    gd0 = OD_COLS - D_MIX - GLA_LORA
    od_in = jnp.concatenate([w[:, :, :gd0], w[:, :, gd0 + GLA_LORA:], w[:, :, gd0:gd0 + GLA_LORA],
                             jnp.zeros((n, D_MODEL, OD_PAD - OD_COLS), F32)], axis=2)
    return dict(ev_in=ev_in.astype(BF16), od_in=od_in.astype(BF16),
                ev_out=W['ev_w_out'].astype(BF16), od_out=W['od_w_out'].astype(BF16))


def _prep_even(i, W):
    mu = jnp.pad(W['rw_mu'][i], (0, RW_PAD - RW_COLS)).reshape(1, RW_PAD)
    rep = lambda v: jnp.repeat(v, HEAD_B)
    rows = [W['rw_w0'][i], W['rw_a0'][i], W['rw_k_k'][i], W['rw_k_a'][i], W['rw_r_k'][i].reshape(-1),
            W['rw_gn_w'][i], W['rw_gn_b'][i], rep(W['mb_d'][i]), W['mb_norm_w'][i]]
    vec = jnp.concatenate([jnp.stack(rows), jnp.zeros((EV_VEC_ROWS - len(rows), D_MIX), F32)], axis=0)
    wa = jnp.zeros((LANES, 2 * D_MIX), F32)
    wa = wa.at[:RW_LORA_W, :D_MIX].set(W['rw_w_up'][i]).at[RW_LORA_W:, D_MIX:].set(W['rw_a_up'][i])
    gup = jnp.pad(W['rw_g_up'][i], ((0, RW_PAD - 3 * D_MIX - LANES - RW_LORA_G), (0, 0)))
    pad_h = lambda v: jnp.pad(v, (0, LANES - H_B)).reshape(1, LANES)
    expand = np.zeros((LANES, D_MIX), np.float32)
    for h in range(H_B):
        expand[h, h * HEAD_B:(h + 1) * HEAD_B] = 1.0
    return dict(mu=mu, vec=vec, wa=wa.astype(BF16), gup=gup.astype(BF16), conv_w=W['mb_conv_w'][i],
                conv_b=W['mb_conv_b'][i].reshape(1, MB_CONV_CH), dt_bias=pad_h(W['mb_dt_bias'][i]),
                a_log=pad_h(W['mb_a_log'][i]), expand=jnp.asarray(expand, BF16))


def _prep_odd(i, W):
    gate_up = jnp.pad(W['gla_gate_up'][i], ((0, LANES - GLA_LORA), (0, 0)))
    return dict(ret_norm_w=W['ret_norm_w'][i].reshape(1, D_MIX), gate_up=gate_up.astype(BF16),
                gate_b=W['gla_gate_b'][i].reshape(1, H_D * HEAD_DK),
                gla_norm_w=W['gla_norm_w'][i].reshape(1, D_MIX))


def _trunk(h, mod, mc0, states, P, PW, FW, final_g, *, nseq, length, t, g, lv, pos0):
    outs = ([], [], [], [], [], [])
    padded = -(-length // t) * t
    carried = states is not None
    carriers = {True: (), False: ()}
    stacked_f32 = isinstance(FW, tuple)
    bf16_weights = []

    def ffn(h, mc, mod_base, k, final_g=None):
        if not stacked_f32:
            return _ffn(h, mod, mc, mod_base, FW[2 * mc['l'] + k], k, final_g)
        h, w16 = _ffn(h, mod, mc, mod_base, FW, k, final_g)
        bf16_weights.append(w16)
        return h

    for l in range(DEPTH):
        i = l // 2
        p = P[l]
        mc = dict(mc0, l=l)
        even = l % 2 == 0
        h = ffn(h, mc, 0, 0)
        z = _inproj(h, mod, mc, PW['ev_in' if even else 'od_in'], i).reshape(nseq, length, -1)
        if padded != length:
            z = jnp.pad(z, ((0, 0), (0, padded - length), (0, 0)))
        gl = max(1, g // 2) if carried and not carriers[even] else g
        if even:
            y, new = _even_mixer(z, states and states[:4], carriers[even], i, p, t=t, g=gl, lv=lv)
            slots, large = (0, 1, 2, 3), (1, 3)
        else:
            go = gl if min(GLA_SUB, t) == t else 1
            y, new = _odd_mixer(z, states and states[4:], carriers[even], i, p, t=t, g=go, lv=lv, pos0=pos0)
            slots, large = (4, 5), (4, 5)
        if carried:
            carriers[even] = tuple(n for k, n in zip(slots, new) if k in large)
        for k, n in zip(slots, new):
            if not (carried and k in large):
                outs[k].append(n)
        y = y[:, :length].reshape(nseq * length, D_MODEL)
        h = _outproj(y, PW['ev_out' if even else 'od_out'], i, h, mod, mc)
        h = ffn(h, mc, 6, 1, final_g=final_g if l == DEPTH - 1 else None)
    res = [jnp.stack(lst) if lst else None for lst in outs]
    if carried:
        res[1], res[3] = carriers[True]
        res[4], res[5] = carriers[False]
    res[3] = jnp.swapaxes(res[3], -1, -2)
    return h, tuple(res), bf16_weights


def kernel(x_prompt, x_sample, state_rwkv_shift, state_rwkv_wkv, state_mamba_conv, state_mamba_ssm,
           state_ret, state_gla, c_prompt, c_sample, ada_w, ada_b, ffn_wg, ffn_wu, ffn_wd, ev_w_in,
           ev_w_out, rw_mu, rw_w0, rw_w_up, rw_a0, rw_a_up, rw_g_up, rw_k_k, rw_k_a, rw_r_k, rw_gn_w,
           rw_gn_b, mb_conv_w, mb_conv_b, mb_dt_bias, mb_a_log, mb_d, mb_norm_w, od_w_in, od_w_out,
           ret_norm_w, gla_gate_up, gla_gate_b, gla_norm_w, final_g):
    W = dict(ev_w_in=ev_w_in, ev_w_out=ev_w_out, rw_mu=rw_mu, rw_w0=rw_w0, rw_w_up=rw_w_up,
             rw_a0=rw_a0, rw_a_up=rw_a_up, rw_g_up=rw_g_up, rw_k_k=rw_k_k, rw_k_a=rw_k_a,
             rw_r_k=rw_r_k, rw_gn_w=rw_gn_w, rw_gn_b=rw_gn_b, mb_conv_w=mb_conv_w,
             mb_conv_b=mb_conv_b, mb_dt_bias=mb_dt_bias, mb_a_log=mb_a_log, mb_d=mb_d,
             mb_norm_w=mb_norm_w, od_w_in=od_w_in, od_w_out=od_w_out, ret_norm_w=ret_norm_w,
             gla_gate_up=gla_gate_up, gla_gate_b=gla_gate_b, gla_norm_w=gla_norm_w)
    nb, seq, _ = x_prompt.shape
    db, dseq, _ = x_sample.shape
    P = [(_prep_even if l % 2 == 0 else _prep_odd)(l // 2, W) for l in range(DEPTH)]
    PW = _prep_proj(W)

    rows = db + nb
    rows_pad = -(-rows // SUBLANES) * SUBLANES
    c_all = jnp.concatenate([c_sample, c_prompt, jnp.zeros((rows_pad - rows, D_MODEL), F32)], axis=0)
    mod = _ada(c_all, ada_w, ada_b)

    states = (state_rwkv_shift, state_rwkv_wkv, state_mamba_conv, state_mamba_ssm, state_ret, state_gla)
    y_s, st_s, w16 = _trunk(x_sample.reshape(db * dseq, D_MODEL), mod, dict(row0=0, rps=dseq), states,
                            P, PW, (ffn_wg, ffn_wu, ffn_wd), final_g, nseq=db, length=dseq,
                            t=SAMPLE_CHUNK, g=SAMPLE_GROUP, lv=dseq, pos0=float(PAST_LEN))
    y_p, st_p, _ = _trunk(x_prompt.reshape(nb * seq, D_MODEL), mod, dict(row0=db, rps=seq), None,
                          P, PW, w16, final_g, nseq=nb, length=seq, t=PROMPT_CHUNK, g=PROMPT_GROUP, lv=PROMPT_CHUNK,
                          pos0=0.0)
    return (y_p.reshape(nb, seq, D_MODEL), y_s.reshape(db, dseq, D_MODEL)) + st_p + st_s
```
